```python
import math
import jax
import jax.numpy as jnp
from jax import lax
import numpy as np

D_MODEL = 1024
BATCH = 4
SEQ = 4096
DEPTH = 1
DEC_BATCH = 32
DEC_SEQ = 8
PAST_LEN = 16384
PAGE_SIZE = 128

A_GROUPS = ((128, 1), (512, 4), (2048, 16))
A_HEADS = 4
A_HEAD_DIM = 128
A_QKV = len(A_GROUPS) * A_HEADS * A_HEAD_DIM
A_OUT = A_HEADS * A_HEAD_DIM
Q_BLOCK = 128
R_HEADS = 4
R_KEY_DIM = 128
R_VAL_DIM = 256
R_QK = R_HEADS * R_KEY_DIM
R_V = R_HEADS * R_VAL_DIM
R_CHUNK = 128
ROPE_BASE = 10000.0
EPS = 1e-5
NEG_INF = -1e30
DEEPNORM_ALPHA = (2 * DEPTH) ** 0.25
DEEPNORM_BETA = (8 * DEPTH) ** -0.25
SPLIT_SIZES = (A_QKV, A_QKV, A_QKV, A_OUT, R_QK, R_QK, R_V, R_V, D_MODEL, D_MODEL)
D_IN = sum(SPLIT_SIZES)
SPLIT_POINTS = tuple(int(s) for s in np.cumsum(SPLIT_SIZES)[:-1])

kernel_name = 'dilated_retention_hybrid_step'


def alibi_slopes(n):
    return jnp.exp2(-8.0 * jnp.arange(1, n + 1, dtype=jnp.float32) / n)


def rotary(x, pos):
    half = x.shape[-1] // 2
    inv = 1.0 / (ROPE_BASE ** jnp.linspace(0.0, 1.0, half, dtype=jnp.float32))
    ang = pos.astype(jnp.float32)[:, None] * inv[None, :]
    cos = jnp.cos(ang)[None, :, None, :]
    sin = jnp.sin(ang)[None, :, None, :]
    x1, x2 = x[..., :half], x[..., half:]
    return jnp.concatenate([x1 * cos - x2 * sin, x1 * sin + x2 * cos], axis=-1)


def dilated_mixture_attention(q, ext_k, ext_v, buf_lens):
    B, T, G, H, dh = q.shape
    qb = Q_BLOCK if T % Q_BLOCK == 0 else T
    n_blk = T // qb
    slopes = alibi_slopes(G * H).reshape(G, H)
    scale = dh ** -0.5

    def block(bi):
        t0 = bi * qb
        qblk = lax.dynamic_slice_in_dim(q, t0, qb, axis=1).astype(jnp.float32)
        lses, outs = [], []
        for g, (w, d) in enumerate(A_GROUPS):
            nk = w // d + 1
            dist = jnp.arange(nk, dtype=jnp.int32) * d
            idx = buf_lens[g] + t0 + jnp.arange(qb, dtype=jnp.int32)[:, None] - dist[None, :]
            valid = idx >= 0
            idx = jnp.maximum(idx, 0)
            kg = jnp.take(ext_k[g], idx, axis=1).astype(jnp.float32)
            vg = jnp.take(ext_v[g], idx, axis=1).astype(jnp.float32)
            s = (jnp.einsum('bqhd,bqkhd->bhqk', qblk[:, :, g], kg) * scale
                 - slopes[g][None, :, None, None] * dist.astype(jnp.float32)[None, None, None, :])
            s = jnp.where(valid[None, None], s, NEG_INF)
            lse = jax.nn.logsumexp(s, axis=-1)
            p = jnp.exp(s - lse[..., None])
            outs.append(jnp.einsum('bhqk,bqkhd->bqhd', p, vg))
            lses.append(lse)
        wts = jax.nn.softmax(jnp.stack(lses, axis=0), axis=0)
        return jnp.einsum('gbhq,gbqhd->bqhd', wts, jnp.stack(outs, axis=0))

    o = lax.map(block, jnp.arange(n_blk, dtype=jnp.int32))
    return o.transpose(1, 0, 2, 3, 4).reshape(B, T, H, dh)


def retention_chunkwise(q, k, v, s0):
    B, T, H, dk = q.shape
    dv = v.shape[-1]
    c = R_CHUNK if T % R_CHUNK == 0 else T
    n = T // c
    lg = jnp.log1p(-jnp.exp2(-5.0 - jnp.arange(H, dtype=jnp.float32)))
    i = jnp.arange(c, dtype=jnp.float32)
    diff = i[:, None] - i[None, :]
    causal = diff >= 0
    dmat = jnp.where(causal[None], jnp.exp(jnp.where(causal, diff, 0.0)[None] * lg[:, None, None]), 0.0)
    q_decay = jnp.exp((i[:, None] + 1.0) * lg[None, :])
    k_decay = jnp.exp((c - 1.0 - i)[:, None] * lg[None, :])
    chunk_decay = jnp.exp(c * lg)

    def to_chunks(a):
        return a.reshape(B, n, c, H, a.shape[-1]).transpose(1, 0, 2, 3, 4)

    def step(S, qkv):
        qc, kc, vc = qkv
        att = jnp.einsum('bihd,bjhd->bhij', qc, kc) * dmat[None]
        o = (jnp.einsum('bhij,bjhe->bihe', att, vc)
             + jnp.einsum('bihd,bhde->bihe', qc, S) * q_decay[None, :, :, None])
        S = (chunk_decay[None, :, None, None] * S
             + jnp.einsum('bjhd,bjhe->bhde', kc * k_decay[None, :, :, None], vc))
        return S, o

    S, o = lax.scan(step, s0, (to_chunks(q), to_chunks(k), to_chunks(v)))
    return o.transpose(1, 0, 2, 3, 4).reshape(B, T, H, dv), S


def hybrid_layer(x, pos0, bufs, s0, w_in, w_pa, w_pb, w_o, gn_g, ln_g, ln_b):
    B, T, _ = x.shape
    G = len(A_GROUPS)
    h = x @ w_in
    qa, ka, va, za, qr, kr, vr, zr, ga, gb = jnp.split(h, SPLIT_POINTS, axis=-1)
    qa = qa.reshape(B, T, G, A_HEADS, A_HEAD_DIM)
    ka = ka.reshape(B, T, G, A_HEADS, A_HEAD_DIM)
    va = va.reshape(B, T, G, A_HEADS, A_HEAD_DIM)
    ext_k, ext_v, lens, new_bufs = [], [], [], []
    for g, (w, d) in enumerate(A_GROUPS):
        kv_new = jnp.stack([ka[:, :, g], va[:, :, g]], axis=2)
        if bufs is None:
            kv_all = kv_new
            lens.append(0)
        else:
            kv_all = jnp.concatenate([bufs[g].astype(kv_new.dtype), kv_new], axis=1)
            lens.append(bufs[g].shape[1])
        ext_k.append(kv_all[:, :, 0])
        ext_v.append(kv_all[:, :, 1])
        keep = min(w, kv_all.shape[1])
        new_bufs.append(kv_all[:, kv_all.shape[1] - keep:])
    o_a = dilated_mixture_attention(qa, ext_k, ext_v, lens)
    pos = pos0 + jnp.arange(T, dtype=jnp.int32)
    qr = rotary(qr.reshape(B, T, R_HEADS, R_KEY_DIM).astype(jnp.float32), pos)
    kr = rotary(kr.reshape(B, T, R_HEADS, R_KEY_DIM).astype(jnp.float32), pos) * (R_KEY_DIM ** -0.5)
    vr = vr.reshape(B, T, R_HEADS, R_VAL_DIM).astype(jnp.float32)
    o_r, s_new = retention_chunkwise(qr, kr, vr, s0.astype(jnp.float32))
    mu = jnp.mean(o_r, axis=-1, keepdims=True)
    var = jnp.mean(jnp.square(o_r - mu), axis=-1, keepdims=True)
    o_r = ((o_r - mu) * lax.rsqrt(var + EPS)).reshape(B, T, R_V) * gn_g.astype(jnp.float32)
    y_a = (jax.nn.silu(za) * o_a.reshape(B, T, A_OUT).astype(x.dtype)) @ w_pa
    y_b = (jax.nn.silu(zr) * o_r.astype(x.dtype)) @ w_pb
    out = (jax.nn.sigmoid(ga) * y_a + jax.nn.sigmoid(gb) * y_b) @ w_o
    z = (DEEPNORM_ALPHA * x + out).astype(jnp.float32)
    zm = jnp.mean(z, axis=-1, keepdims=True)
    zv = jnp.mean(jnp.square(z - zm), axis=-1, keepdims=True)
    y = ((z - zm) * lax.rsqrt(zv + EPS) * ln_g.astype(jnp.float32) + ln_b.astype(jnp.float32)).astype(x.dtype)
    return y, new_bufs[0], new_bufs[1], new_bufs[2], s_new.astype(x.dtype)


def setup_inputs(seed: int = 0) -> dict:
    key = jax.random.key(seed)
    ks = jax.random.split(key, 16)
    f32 = jnp.float32
    x_prompt = jax.random.normal(ks[0], (BATCH, SEQ, D_MODEL), f32)
    x_sample = jax.random.normal(ks[1], (DEC_BATCH, DEC_SEQ, D_MODEL), f32)
    caches = [jax.random.normal(ks[2 + g], (DEC_BATCH, min(w, PAST_LEN), 2, A_HEADS, A_HEAD_DIM), f32)
              for g, (w, _) in enumerate(A_GROUPS)]
    state_ret = 0.5 * jax.random.normal(ks[5], (DEC_BATCH, R_HEADS, R_KEY_DIM, R_VAL_DIM), f32)
    offs = np.concatenate([[0], np.cumsum(SPLIT_SIZES)])
    col_scale = np.ones((D_IN,), np.float32)
    col_scale[offs[2]:offs[3]] = DEEPNORM_BETA
    col_scale[offs[6]:offs[7]] = DEEPNORM_BETA
    w_in = jax.random.normal(ks[6], (D_MODEL, D_IN), f32) * (D_MODEL ** -0.5) * jnp.asarray(col_scale)
    w_pa = jax.random.normal(ks[7], (A_OUT, D_MODEL), f32) * (A_OUT ** -0.5) * DEEPNORM_BETA
    w_pb = jax.random.normal(ks[8], (R_V, D_MODEL), f32) * (R_V ** -0.5) * DEEPNORM_BETA
    w_o = jax.random.normal(ks[9], (D_MODEL, D_MODEL), f32) * (D_MODEL ** -0.5) * DEEPNORM_BETA
    gn_g = 1.0 + 0.02 * jax.random.normal(ks[10], (R_V,), f32)
    ln_g = 1.0 + 0.02 * jax.random.normal(ks[11], (D_MODEL,), f32)
    ln_b = 0.02 * jax.random.normal(ks[12], (D_MODEL,), f32)
    return {'x_prompt': x_prompt, 'x_sample': x_sample,
            'cache_kv_w128': caches[0], 'cache_kv_w512': caches[1], 'cache_kv_w2048': caches[2],
            'state_ret': state_ret,
            'w_in': w_in, 'w_pa': w_pa, 'w_pb': w_pb, 'w_o': w_o,
            'gn_g': gn_g, 'ln_g': ln_g, 'ln_b': ln_b}


def reference(x_prompt, x_sample, cache_kv_w128, cache_kv_w512, cache_kv_w2048, state_ret,
              w_in, w_pa, w_pb, w_o, gn_g, ln_g, ln_b):
    y_p, x_p = x_prompt, x_prompt
    y_s, x_s = x_sample, x_sample
    for layer in range(DEPTH):
        s0_p = jnp.zeros((x_p.shape[0], R_HEADS, R_KEY_DIM, R_VAL_DIM), jnp.float32)
        y_p, kv128_p, kv512_p, kv2048_p, ret_p = hybrid_layer(
            x_p, 0, None, s0_p, w_in, w_pa, w_pb, w_o, gn_g, ln_g, ln_b)
        y_s, kv128_s, kv512_s, kv2048_s, ret_s = hybrid_layer(
            x_s, PAST_LEN, (cache_kv_w128, cache_kv_w512, cache_kv_w2048), state_ret,
            w_in, w_pa, w_pb, w_o, gn_g, ln_g, ln_b)
        x_p, x_s = y_p, y_s
    return (y_p, y_s, kv128_p, kv512_p, kv2048_p, ret_p, kv128_s, kv512_s, kv2048_s, ret_s)
```

```python
import functools

import numpy as np
import jax
import jax.numpy as jnp
from jax import lax
from jax.experimental import pallas as pl
from jax.experimental.pallas import tpu as pltpu

D_MODEL = 1024
PAST_LEN = 16384
A_GROUPS = ((128, 1), (512, 4), (2048, 16))
A_HEADS = 4
A_HEAD_DIM = 128
A_GW = A_HEADS * A_HEAD_DIM
A_QKV = len(A_GROUPS) * A_GW
A_OUT = A_GW
N_DIST = 128
R_HEADS = 4
R_KEY_DIM = 128
R_VAL_DIM = 256
R_QK = R_HEADS * R_KEY_DIM
R_V = R_HEADS * R_VAL_DIM
R_CHUNK = 128
ROPE_BASE = 10000.0
EPS = 1e-5
NEG_INF = -1e30
DEEPNORM_ALPHA = 2.0 ** 0.25
_SPLIT = (A_QKV, A_QKV, A_QKV, A_OUT, R_QK, R_QK, R_V, R_V, D_MODEL, D_MODEL)
_OFF = tuple(int(v) for v in np.concatenate([[0], np.cumsum(_SPLIT)]))

BF16 = jnp.bfloat16
F32 = jnp.float32
VMEM_LIMIT = 56 * 1024 * 1024

_NT = (((1,), (1,)), ((), ()))
_TN = (((0,), (0,)), ((), ()))


def _dot(a, b):
  return jnp.dot(a, b, preferred_element_type=F32)


def _dot_nt(a, b):
  return lax.dot_general(a, b, _NT, preferred_element_type=F32)


def _dot_tn(a, b):
  return lax.dot_general(a, b, _TN, preferred_element_type=F32)


def _params(sem):
  return pltpu.CompilerParams(dimension_semantics=sem, vmem_limit_bytes=VMEM_LIMIT)


def _mm_kernel(x_ref, w_ref, o_ref):
  o_ref[...] = _dot(x_ref[...].astype(BF16), w_ref[...]).astype(o_ref.dtype)


def _matmul(x, w, out_dtype, tm, tn):
  m, k = x.shape
  n = w.shape[1]
  return pl.pallas_call(
      _mm_kernel,
      grid=(m // tm, n // tn),
      in_specs=[pl.BlockSpec((tm, k), lambda i, j: (i, 0)),
                pl.BlockSpec((k, tn), lambda i, j: (0, j))],
      out_specs=pl.BlockSpec((tm, tn), lambda i, j: (i, j)),
      out_shape=jax.ShapeDtypeStruct((m, n), out_dtype),
      compiler_params=_params(("parallel", "arbitrary")),
      name="proj_matmul",
  )(x, w)


def _proj_heads_kernel(x_ref, w_ref, o_ref):
  res = _dot(x_ref[...].astype(BF16), w_ref[...])
  for s in range(o_ref.shape[0]):
    o_ref[s] = res[:, s * 128:(s + 1) * 128].astype(o_ref.dtype)


def _proj_heads(xp, w, tm):
  b, d, ls, k = xp.shape
  ns = w.shape[1] // 128
  return pl.pallas_call(
      _proj_heads_kernel,
      grid=(b, d, ls // tm),
      in_specs=[pl.BlockSpec((None, None, tm, k), lambda bi, r, i: (bi, r, i, 0)),
                pl.BlockSpec((k, ns * 128), lambda bi, r, i: (0, 0))],
      out_specs=pl.BlockSpec((None, None, ns, tm, 128), lambda bi, r, i: (bi, r, 0, i, 0)),
      out_shape=jax.ShapeDtypeStruct((b, d, ns, ls, 128), BF16),
      compiler_params=_params(("parallel", "parallel", "arbitrary")),
      name="proj_heads",
  )(xp, w)


ATT_BQ = 256


def _attn_prompt_kernel(slopes_ref, *refs, seq):
  qkv = refs[:9]
  o_ref = refs[9]
  oacc = refs[10:13]
  lacc = refs[13:16]
  bias_ref = refs[16]
  h = pl.program_id(1)
  scale = A_HEAD_DIM ** -0.5

  for g, (_, d) in enumerate(A_GROUPS):
    q_ref, k_ref, v_ref = qkv[3 * g:3 * g + 3]
    ls = seq // d
    bq = min(ATT_BQ, ls)
    nqb = ls // bq
    klen = min(bq + N_DIST, ls)
    slope = slopes_ref[g, h]
    qi = lax.broadcasted_iota(jnp.int32, (bq, klen), 0)
    kj = lax.broadcasted_iota(jnp.int32, (bq, klen), 1)
    for which, off in enumerate((0, N_DIST)):
      delta = qi - kj + off
      valid = (delta >= 0) & (delta <= N_DIST)
      dist = (delta * d).astype(F32)
      bias_ref[2 * g + which, :bq, :klen] = jnp.where(valid, -slope * dist, NEG_INF)

    def block(idx, carry, g=g, d=d, bq=bq, nqb=nqb, klen=klen,
              q_ref=q_ref, k_ref=k_ref, v_ref=v_ref):
      r = idx // nqb
      qb = idx % nqb
      q0 = pl.multiple_of(qb * bq, bq)
      k0 = pl.multiple_of(jnp.maximum(qb * bq - N_DIST, 0), N_DIST)
      q = q_ref[r, pl.ds(q0, bq), :]
      k = k_ref[r, pl.ds(k0, klen), :]
      v = v_ref[r, pl.ds(k0, klen), :]
      bias = bias_ref[2 * g + jnp.minimum(qb, 1), :bq, :klen]
      s = _dot_nt(q, k) * scale + bias
      m = jnp.max(s, axis=-1, keepdims=True)
      p = jnp.exp(s - m)
      l = jnp.sum(p, axis=-1, keepdims=True)
      o = _dot(p.astype(BF16), v) / l
      lse = m + jnp.log(l)
      t0 = r + q0 * d
      if d == 1:
        oacc[g][pl.ds(q0, bq), :] = o
        lacc[g][pl.ds(q0, bq), :] = jnp.broadcast_to(lse, (bq, 128))
      else:
        oacc[g][pl.ds(t0, bq, stride=d), :] = o
        lacc[g][pl.ds(t0, bq, stride=d), :] = jnp.broadcast_to(lse, (bq, 128))
      return carry

    lax.fori_loop(0, d * nqb, block, 0)

  mrows = 512

  def merge(i, carry):
    r0 = pl.multiple_of(i * mrows, mrows)
    l0 = lacc[0][pl.ds(r0, mrows), :]
    l1 = lacc[1][pl.ds(r0, mrows), :]
    l2 = lacc[2][pl.ds(r0, mrows), :]
    mx = jnp.maximum(jnp.maximum(l0, l1), l2)
    e0 = jnp.exp(l0 - mx)
    e1 = jnp.exp(l1 - mx)
    e2 = jnp.exp(l2 - mx)
    den = e0 + e1 + e2
    acc = (e0 * oacc[0][pl.ds(r0, mrows), :] + e1 * oacc[1][pl.ds(r0, mrows), :]
           + e2 * oacc[2][pl.ds(r0, mrows), :])
    o_ref[pl.ds(r0, mrows), :] = acc / den
    return carry

  lax.fori_loop(0, seq // mrows, merge, 0)


def _attn_prompt(qkvs, slopes, batch, seq):
  in_specs = [pl.BlockSpec(memory_space=pltpu.SMEM)]
  args = [slopes]
  for g, (_, d) in enumerate(A_GROUPS):
    ls = seq // d
    for part in range(3):
      in_specs.append(pl.BlockSpec(
          (None, d, None, ls, 128),
          lambda b, h, part=part: (b, 0, part * A_HEADS + h, 0, 0)))
      args.append(qkvs[g])
  scratch = ([pltpu.VMEM((seq, 128), F32) for _ in range(6)]
             + [pltpu.VMEM((6, ATT_BQ, ATT_BQ + N_DIST), F32)])
  return pl.pallas_call(
      functools.partial(_attn_prompt_kernel, seq=seq),
      grid=(batch, A_HEADS),
      in_specs=in_specs,
      out_specs=pl.BlockSpec((None, seq, 128), lambda b, h: (b, 0, h)),
      out_shape=jax.ShapeDtypeStruct((batch, seq, A_OUT), F32),
      scratch_shapes=scratch,
      compiler_params=_params(("parallel", "arbitrary")),
      name="attn_prompt",
  )(*args)


def _attn_sample_kernel(slopes_ref, q_ref, c0_ref, c1_ref, c2_ref, n0_ref, n1_ref, n2_ref,
                        h0_ref, h1_ref, h2_ref,
                        o_ref, u0_ref, u1_ref, u2_ref, sems, *, tq):
  b = pl.program_id(0)
  cache_refs = (c0_ref, c1_ref, c2_ref)
  new_refs = (n0_ref, n1_ref, n2_ref)
  hbm_refs = (h0_ref, h1_ref, h2_ref)
  out_refs = (u0_ref, u1_ref, u2_ref)

  copies = []
  for g, (w, _) in enumerate(A_GROUPS):
    body = pltpu.make_async_copy(hbm_refs[g].at[b, pl.ds(tq, w - tq)],
                                 out_refs[g].at[b, pl.ds(0, w - tq)], sems.at[2 * g])
    tail = pltpu.make_async_copy(new_refs[g].at[0], out_refs[g].at[b, pl.ds(w - tq, tq)],
                                 sems.at[2 * g + 1])
    body.start()
    tail.start()
    copies += [body, tail]

  scale = A_HEAD_DIM ** -0.5
  nrow = N_DIST * 8 + tq * 8
  ri = lax.broadcasted_iota(jnp.int32, (nrow, 128), 0)
  cj = lax.broadcasted_iota(jnp.int32, (nrow, 128), 1)
  slot = ri % 8
  key = ri // 8
  tcol = cj // A_HEADS
  hcol = cj % A_HEADS
  col_ok = cj < tq * A_HEADS
  lse_g = []
  ot_g = []
  for g, (w, d) in enumerate(A_GROUPS):
    qg = q_ref[g]
    xn = new_refs[g][0].reshape(tq * 8, 128)
    nres = min(d, tq)
    xs = []
    s = jnp.zeros((nrow, 128), F32)
    col1 = lax.broadcasted_iota(jnp.int32, (128, 1), 0) // A_HEADS
    for rho in range(nres):
      xc = cache_refs[g][:, rho, :, :].reshape(N_DIST * 8, 128)
      x = jnp.concatenate([xc, xn], axis=0).astype(BF16)
      xs.append(x)
      qm = jnp.where(col1 % d == rho, qg, 0.0).astype(BF16)
      s = s + _dot_nt(x, qm)
    qt = tcol // d
    nidx = key - N_DIST
    in_cache = key < N_DIST
    valid_c = in_cache & (key >= qt)
    dist_c = d * (N_DIST + qt - key)
    valid_n = (~in_cache) & (nidx <= tcol) & ((tcol - nidx) % d == 0)
    dist_n = tcol - nidx
    valid = (valid_c | valid_n) & (slot == hcol) & col_ok
    dist = jnp.where(in_cache, dist_c, dist_n).astype(F32)
    slope = jnp.zeros((nrow, 128), F32)
    for hh in range(A_HEADS):
      slope = jnp.where(hcol == hh, slopes_ref[g, hh], slope)
    s = jnp.where(valid, s * scale - slope * dist, NEG_INF)
    m = jnp.max(s, axis=0, keepdims=True)
    p = jnp.where(valid, jnp.exp(s - m), 0.0)
    l = jnp.sum(p, axis=0, keepdims=True)
    l = jnp.where(l > 0.0, l, 1.0)
    pn = p / l
    pv = pltpu.roll(pn, 4, 0)
    ot = jnp.zeros((128, 128), F32)
    col_row = lax.broadcasted_iota(jnp.int32, (1, 128), 1) // A_HEADS
    for rho in range(nres):
      pm = jnp.where(col_row % d == rho, pv, 0.0).astype(BF16)
      ot = ot + _dot_tn(xs[rho], pm)
    ot_g.append(ot)
    lse_g.append(m + jnp.log(l))
  mx = jnp.maximum(jnp.maximum(lse_g[0], lse_g[1]), lse_g[2])
  es = [jnp.exp(v - mx) for v in lse_g]
  den = es[0] + es[1] + es[2]
  ot = (es[0] * ot_g[0] + es[1] * ot_g[1] + es[2] * ot_g[2]) / den
  o_ref[...] = ot.T

  for c in copies:
    c.wait()


def _attn_sample(q, caches, news, slopes, tq):
  batch = q.shape[0]
  in_specs = [pl.BlockSpec(memory_space=pltpu.SMEM),
              pl.BlockSpec((None, 3, 128, 128), lambda b: (b, 0, 0, 0))]
  args = [slopes, q]
  for g, (w, d) in enumerate(A_GROUPS):
    in_specs.append(pl.BlockSpec((None, N_DIST, min(d, tq), 8, 128), lambda b: (b, 0, 0, 0, 0)))
    args.append(caches[g].reshape(batch, N_DIST, d, 8, 128))
  for g in range(3):
    in_specs.append(pl.BlockSpec((1, tq, 8, 128), lambda b: (b, 0, 0, 0)))
    args.append(news[g])
  for g in range(3):
    in_specs.append(pl.BlockSpec(memory_space=pl.ANY))
    args.append(caches[g])
  out_shape = [jax.ShapeDtypeStruct((batch, 128, 128), F32)]
  out_specs = [pl.BlockSpec((None, 128, 128), lambda b: (b, 0, 0))]
  for g, (w, _) in enumerate(A_GROUPS):
    out_shape.append(jax.ShapeDtypeStruct((batch, w, 8, 128), F32))
    out_specs.append(pl.BlockSpec(memory_space=pl.ANY))
  return pl.pallas_call(
      functools.partial(_attn_sample_kernel, tq=tq),
      grid=(batch,),
      in_specs=in_specs,
      out_specs=out_specs,
      out_shape=out_shape,
      scratch_shapes=[pltpu.SemaphoreType.DMA((6,))],
      compiler_params=_params(("arbitrary",)),
      name="attn_sample",
  )(*args)


def _retention_kernel(q_ref, k_ref, v_ref, cos_ref, sin_ref, dmat_ref, qdec_ref, kdec_ref,
                      cdec_ref, s0_ref, gn_ref, o_ref, sout_ref, state, *, chunk):
  j = pl.program_id(1)

  @pl.when(j == 0)
  def _():
    state[...] = s0_ref[...]

  nchunk = q_ref.shape[0] // chunk
  kscale = R_KEY_DIM ** -0.5
  for ci in range(nchunk):
    rows = pl.ds(ci * chunk, chunk)
    cosv = cos_ref[rows, :]
    sinv = sin_ref[rows, :]
    for h in range(R_HEADS):
      qh = q_ref[rows, h * 128:(h + 1) * 128]
      kh = k_ref[rows, h * 128:(h + 1) * 128]
      qrot = qh * cosv + pltpu.roll(qh, 64, 1) * sinv
      krot = (kh * cosv + pltpu.roll(kh, 64, 1) * sinv) * kscale
      qb = qrot.astype(BF16)
      vh = v_ref[rows, h * 256:(h + 1) * 256].astype(BF16)
      att = _dot_nt(qb, krot.astype(BF16)) * dmat_ref[h]
      sh = state[h]
      o = _dot(att.astype(BF16), vh) + _dot(qb, sh.astype(BF16)) * qdec_ref[h]
      state[h] = cdec_ref[h] * sh + _dot_tn((krot * kdec_ref[h]).astype(BF16), vh)
      mu = jnp.mean(o, axis=-1, keepdims=True)
      var = jnp.mean(jnp.square(o - mu), axis=-1, keepdims=True)
      on = (o - mu) * lax.rsqrt(var + EPS) * gn_ref[:, h * 256:(h + 1) * 256]
      o_ref[rows, h * 256:(h + 1) * 256] = on

  @pl.when(j == pl.num_programs(1) - 1)
  def _():
    sout_ref[...] = state[...]


def _retention(qkv, cos, sin, tables, s0, gn, rows_per_step, chunk):
  batch, t, _ = qkv.shape
  dmat, qdec, kdec, cdec = tables
  tc = rows_per_step
  const3 = lambda b, j: (0, 0, 0)
  return pl.pallas_call(
      functools.partial(_retention_kernel, chunk=chunk),
      grid=(batch, t // tc),
      in_specs=[pl.BlockSpec((None, tc, R_QK), lambda b, j: (b, j, 0)),
                pl.BlockSpec((None, tc, R_QK), lambda b, j: (b, j, 1)),
                pl.BlockSpec((None, tc, R_V), lambda b, j: (b, j, 1)),
                pl.BlockSpec((tc, 128), lambda b, j: (j, 0)),
                pl.BlockSpec((tc, 128), lambda b, j: (j, 0)),
                pl.BlockSpec(dmat.shape, const3),
                pl.BlockSpec(qdec.shape, const3),
                pl.BlockSpec(kdec.shape, const3),
                pl.BlockSpec(cdec.shape, const3),
                pl.BlockSpec((None, R_HEADS, R_KEY_DIM, R_VAL_DIM), lambda b, j: (b, 0, 0, 0)),
                pl.BlockSpec((1, R_V), lambda b, j: (0, 0))],
      out_specs=[pl.BlockSpec((None, tc, R_V), lambda b, j: (b, j, 0)),
                 pl.BlockSpec((None, R_HEADS, R_KEY_DIM, R_VAL_DIM), lambda b, j: (b, 0, 0, 0))],
      out_shape=[jax.ShapeDtypeStruct((batch, t, R_V), F32),
                 jax.ShapeDtypeStruct((batch, R_HEADS, R_KEY_DIM, R_VAL_DIM), F32)],
      scratch_shapes=[pltpu.VMEM((R_HEADS, R_KEY_DIM, R_VAL_DIM), F32)],
      compiler_params=_params(("parallel", "arbitrary")),
      name="retention",
  )(qkv, qkv, qkv, cos, sin, dmat, qdec, kdec, cdec, s0, gn)


def _retention_tables(c_true, c_pad):
  lg = jnp.log1p(-jnp.exp2(-5.0 - jnp.arange(R_HEADS, dtype=F32)))
  i = jnp.arange(c_pad, dtype=F32)
  live = i < c_true
  diff = i[:, None] - i[None, :]
  causal = (diff >= 0) & live[:, None] & live[None, :]
  dmat = jnp.where(causal[None], jnp.exp(jnp.where(causal, diff, 0.0)[None] * lg[:, None, None]), 0.0)
  qdec = jnp.where(live[None, :], jnp.exp((i[None, :] + 1.0) * lg[:, None]), 0.0)
  kdec = jnp.where(live[None, :], jnp.exp((c_true - 1.0 - i)[None, :] * lg[:, None]), 0.0)
  cdec = jnp.exp(c_true * lg)
  qdec = jnp.broadcast_to(qdec[:, :, None], (R_HEADS, c_pad, R_VAL_DIM))
  kdec = jnp.broadcast_to(kdec[:, :, None], (R_HEADS, c_pad, R_KEY_DIM))
  cdec = jnp.broadcast_to(cdec[:, None, None], (R_HEADS, R_KEY_DIM, R_VAL_DIM))
  return dmat, qdec, kdec, cdec


def _rope_tables(pos):
  half = R_KEY_DIM // 2
  inv = 1.0 / (ROPE_BASE ** jnp.linspace(0.0, 1.0, half, dtype=F32))
  ang = pos.astype(F32)[:, None] * inv[None, :]
  cos = jnp.cos(ang)
  sin = jnp.sin(ang)
  return jnp.concatenate([cos, cos], axis=-1), jnp.concatenate([-sin, sin], axis=-1)


def _output_kernel(x_ref, oa_ref, or_ref, wz_ref, wpa_ref, wpb_ref, wo_ref, lng_ref, lnb_ref, y_ref):
  x = x_ref[...]
  z = _dot(x.astype(BF16), wz_ref[...])
  za = z[:, :A_OUT]
  zr = z[:, A_OUT:A_OUT + R_V]
  ga = z[:, A_OUT + R_V:A_OUT + R_V + D_MODEL]
  gb = z[:, A_OUT + R_V + D_MODEL:]
  ya = _dot((jax.nn.silu(za) * oa_ref[...]).astype(BF16), wpa_ref[...])
  yb = _dot((jax.nn.silu(zr) * or_ref[...]).astype(BF16), wpb_ref[...])
  mix = jax.nn.sigmoid(ga) * ya + jax.nn.sigmoid(gb) * yb
  out = _dot(mix.astype(BF16), wo_ref[...])
  zz = DEEPNORM_ALPHA * x + out
  zm = jnp.mean(zz, axis=-1, keepdims=True)
  zv = jnp.mean(jnp.square(zz - zm), axis=-1, keepdims=True)
  y_ref[...] = (zz - zm) * lax.rsqrt(zv + EPS) * lng_ref[...] + lnb_ref[...]


def _output(x, oa, orr, wz, wpa, wpb, wo, lng, lnb, tm):
  m = x.shape[0]
  row = lambda w: pl.BlockSpec((tm, w), lambda i: (i, 0))
  full = lambda a: pl.BlockSpec(a.shape, lambda i: (0, 0), pipeline_mode=pl.Buffered(1))
  return pl.pallas_call(
      _output_kernel,
      grid=(m // tm,),
      in_specs=[row(D_MODEL), row(A_OUT), row(R_V), full(wz), full(wpa), full(wpb), full(wo),
                full(lng), full(lnb)],
      out_specs=row(D_MODEL),
      out_shape=jax.ShapeDtypeStruct((m, D_MODEL), F32),
      compiler_params=_params(("parallel",)),
      name="gates_output",
  )(x, oa, orr, wz, wpa, wpb, wo, lng, lnb)


def _alibi_slopes():
  n = len(A_GROUPS) * A_HEADS
  return jnp.exp2(-8.0 * jnp.arange(1, n + 1, dtype=F32) / n).reshape(len(A_GROUPS), A_HEADS)


def _w_cols(w_in, *ranges):
  return jnp.concatenate([w_in[:, a:b] for a, b in ranges], axis=1).astype(BF16)


def _group_cols(g):
  return [(_OFF[p] + g * A_GW, _OFF[p] + (g + 1) * A_GW) for p in range(3)]


def kernel(x_prompt, x_sample, cache_kv_w128, cache_kv_w512, cache_kv_w2048, state_ret,
           w_in, w_pa, w_pb, w_o, gn_g, ln_g, ln_b):
  bp, tp, _ = x_prompt.shape
  bs, ts, _ = x_sample.shape
  slopes = _alibi_slopes()
  w_ret = _w_cols(w_in, (_OFF[4], _OFF[7]))
  w_gate = _w_cols(w_in, (_OFF[3], _OFF[4]), (_OFF[7], _OFF[10]))
  w_grp = [_w_cols(w_in, *_group_cols(g)) for g in range(3)]
  wpa = w_pa.astype(BF16)
  wpb = w_pb.astype(BF16)
  wo = w_o.astype(BF16)
  gn = gn_g.reshape(1, R_V)
  lng = ln_g.reshape(1, D_MODEL)
  lnb = ln_b.reshape(1, D_MODEL)

  xp2 = x_prompt.reshape(bp * tp, D_MODEL)
  qkvs = []
  for g, (w, d) in enumerate(A_GROUPS):
    ls = tp // d
    xperm = x_prompt.reshape(bp, ls, d, D_MODEL).transpose(0, 2, 1, 3).astype(BF16)
    qkvs.append(_proj_heads(xperm, w_grp[g], min(ls, 512)))
  oa_p = _attn_prompt(qkvs, slopes, bp, tp).reshape(bp * tp, A_OUT)
  wmax = max(w for w, _ in A_GROUPS)
  w_kv = jnp.concatenate([w_grp[g][:, A_GW:] for g in range(3)], axis=1)
  kv_tail = _matmul(x_prompt[:, tp - wmax:].reshape(bp * wmax, D_MODEL), w_kv, F32, 512, 1024)
  kv_tail = kv_tail.reshape(bp, wmax, 3, 2, A_HEADS, A_HEAD_DIM)
  kv_p = [kv_tail[:, wmax - min(w, tp):, g] for g, (w, _) in enumerate(A_GROUPS)]

  ret_p = _matmul(xp2, w_ret, F32, 512, R_QK)
  cos_p, sin_p = _rope_tables(jnp.arange(tp, dtype=jnp.int32))
  or_p, s_p = _retention(
      ret_p.reshape(bp, tp, 2 * R_QK + R_V), cos_p, sin_p, _retention_tables(R_CHUNK, R_CHUNK),
      jnp.zeros((bp, R_HEADS, R_KEY_DIM, R_VAL_DIM), F32), gn, 512, R_CHUNK)
  y_p = _output(xp2, oa_p, or_p.reshape(bp * tp, R_V), w_gate, wpa, wpb, wo, lng, lnb, 256)

  xs2 = x_sample.reshape(bs * ts, D_MODEL)
  w_s = jnp.concatenate(w_grp + [w_ret], axis=1)
  hs = _matmul(xs2, w_s, F32, bs * ts, 1664)
  caches = (cache_kv_w128, cache_kv_w512, cache_kv_w2048)
  q_s, news = [], []
  for g in range(3):
    base = g * 3 * A_GW
    qg = hs[:, base:base + A_GW].reshape(bs, ts * A_HEADS, A_HEAD_DIM)
    q_s.append(jnp.pad(qg, ((0, 0), (0, 128 - ts * A_HEADS), (0, 0))))
    news.append(hs[:, base + A_GW:base + 3 * A_GW].reshape(bs, ts, 8, 128))
  q_s = jnp.stack(q_s, axis=1)
  caches8 = [c.reshape(bs, c.shape[1], 8, 128) for c in caches]
  oa_s, u0, u1, u2 = _attn_sample(q_s, caches8, news, slopes, ts)
  oa_s = oa_s[:, :ts * A_HEADS].reshape(bs * ts, A_OUT)
  kv_s = [u.reshape(bs, u.shape[1], 2, A_HEADS, A_HEAD_DIM) for u in (u0, u1, u2)]

  rb = 3 * 3 * A_GW
  cpad = R_CHUNK
  ret_s = jnp.pad(hs[:, rb:].reshape(bs, ts, 2 * R_QK + R_V), ((0, 0), (0, cpad - ts), (0, 0)))
  cos_s, sin_s = _rope_tables(PAST_LEN + jnp.arange(ts, dtype=jnp.int32))
  padtab = lambda a: jnp.pad(a, ((0, cpad - ts), (0, 0)))
  or_s, s_s = _retention(
      ret_s, padtab(cos_s), padtab(sin_s),
      _retention_tables(ts, cpad), state_ret.astype(F32), gn, cpad, cpad)
  or_s = or_s[:, :ts].reshape(bs * ts, R_V)
  y_s = _output(xs2, oa_s, or_s, w_gate, wpa, wpb, wo, lng, lnb, 256)

  return (y_p.reshape(bp, tp, D_MODEL), y_s.reshape(bs, ts, D_MODEL),
          kv_p[0], kv_p[1], kv_p[2], s_p,
          kv_s[0], kv_s[1], kv_s[2], s_s)
```

```python
import functools

import numpy as np
import jax
import jax.numpy as jnp
from jax import lax
from jax.experimental import pallas as pl
from jax.experimental.pallas import tpu as pltpu

D_MODEL = 1024
PAST_LEN = 16384
A_GROUPS = ((128, 1), (512, 4), (2048, 16))
A_HEADS = 4
A_HEAD_DIM = 128
A_GW = A_HEADS * A_HEAD_DIM
A_QKV = len(A_GROUPS) * A_GW
A_OUT = A_GW
N_DIST = 128
R_HEADS = 4
R_KEY_DIM = 128
R_VAL_DIM = 256
R_QK = R_HEADS * R_KEY_DIM
R_V = R_HEADS * R_VAL_DIM
R_CHUNK = 128
ROPE_BASE = 10000.0
EPS = 1e-5
NEG_INF = -1e30
DEEPNORM_ALPHA = 2.0 ** 0.25
_SPLIT = (A_QKV, A_QKV, A_QKV, A_OUT, R_QK, R_QK, R_V, R_V, D_MODEL, D_MODEL)
_OFF = tuple(int(v) for v in np.concatenate([[0], np.cumsum(_SPLIT)]))

BF16 = jnp.bfloat16
F32 = jnp.float32
VMEM_LIMIT = 56 * 1024 * 1024

_NT = (((1,), (1,)), ((), ()))
_TN = (((0,), (0,)), ((), ()))


def _dot(a, b):
  return jnp.dot(a, b, preferred_element_type=F32)


def _dot_nt(a, b):
  return lax.dot_general(a, b, _NT, preferred_element_type=F32)


def _dot_tn(a, b):
  return lax.dot_general(a, b, _TN, preferred_element_type=F32)


def _params(sem):
  return pltpu.CompilerParams(dimension_semantics=sem, vmem_limit_bytes=VMEM_LIMIT)


def _mm_kernel(x_ref, w_ref, o_ref):
  o_ref[...] = _dot(x_ref[...].astype(BF16), w_ref[...]).astype(o_ref.dtype)


def _matmul(x, w, out_dtype, tm, tn):
  m, k = x.shape
  n = w.shape[1]
  return pl.pallas_call(
      _mm_kernel,
      grid=(m // tm, n // tn),
      in_specs=[pl.BlockSpec((tm, k), lambda i, j: (i, 0)),
                pl.BlockSpec((k, tn), lambda i, j: (0, j))],
      out_specs=pl.BlockSpec((tm, tn), lambda i, j: (i, j)),
      out_shape=jax.ShapeDtypeStruct((m, n), out_dtype),
      compiler_params=_params(("parallel", "arbitrary")),
      name="proj_matmul",
  )(x, w)


def _proj_heads_kernel(x_ref, w_ref, o_ref):
  res = _dot(x_ref[...].astype(BF16), w_ref[...])
  for s in range(o_ref.shape[0]):
    o_ref[s] = res[:, s * 128:(s + 1) * 128].astype(o_ref.dtype)


def _proj_heads(xp, w, tm):
  b, d, ls, k = xp.shape
  ns = w.shape[1] // 128
  return pl.pallas_call(
      _proj_heads_kernel,
      grid=(b, d, ls // tm),
      in_specs=[pl.BlockSpec((None, None, tm, k), lambda bi, r, i: (bi, r, i, 0)),
                pl.BlockSpec((k, ns * 128), lambda bi, r, i: (0, 0))],
      out_specs=pl.BlockSpec((None, None, ns, tm, 128), lambda bi, r, i: (bi, r, 0, i, 0)),
      out_shape=jax.ShapeDtypeStruct((b, d, ns, ls, 128), BF16),
      compiler_params=_params(("parallel", "parallel", "arbitrary")),
      name="proj_heads",
  )(xp, w)


ATT_BQ = 256


def _attn_prompt_kernel(slopes_ref, *refs, seq):
  qkv = refs[:9]
  o_ref = refs[9]
  oacc = refs[10:13]
  lacc = refs[13:16]
  bias_ref = refs[16]
  h = pl.program_id(1)
  scale = A_HEAD_DIM ** -0.5

  for g, (_, d) in enumerate(A_GROUPS):
    q_ref, k_ref, v_ref = qkv[3 * g:3 * g + 3]
    ls = seq // d
    bq = min(ATT_BQ, ls)
    nqb = ls // bq
    klen = min(bq + N_DIST, ls)
    slope = slopes_ref[g, h]
    qi = lax.broadcasted_iota(jnp.int32, (bq, klen), 0)
    kj = lax.broadcasted_iota(jnp.int32, (bq, klen), 1)
    for which, off in enumerate((0, N_DIST)):
      delta = qi - kj + off
      valid = (delta >= 0) & (delta <= N_DIST)
      dist = (delta * d).astype(F32)
      bias_ref[2 * g + which, :bq, :klen] = jnp.where(valid, -slope * dist, NEG_INF)

    def block(idx, carry, g=g, d=d, bq=bq, nqb=nqb, klen=klen,
              q_ref=q_ref, k_ref=k_ref, v_ref=v_ref):
      r = idx // nqb
      qb = idx % nqb
      q0 = pl.multiple_of(qb * bq, bq)
      k0 = pl.multiple_of(jnp.maximum(qb * bq - N_DIST, 0), N_DIST)
      q = q_ref[r, pl.ds(q0, bq), :]
      k = k_ref[r, pl.ds(k0, klen), :]
      v = v_ref[r, pl.ds(k0, klen), :]
      bias = bias_ref[2 * g + jnp.minimum(qb, 1), :bq, :klen]
      s = _dot_nt(q, k) * scale + bias
      m = jnp.max(s, axis=-1, keepdims=True)
      p = jnp.exp(s - m)
      l = jnp.sum(p, axis=-1, keepdims=True)
      o = _dot(p.astype(BF16), v) / l
      lse = m + jnp.log(l)
      t0 = r + q0 * d
      if d == 1:
        oacc[g][pl.ds(q0, bq), :] = o
        lacc[g][pl.ds(q0, bq), :] = jnp.broadcast_to(lse, (bq, 128))
      else:
        oacc[g][pl.ds(t0, bq, stride=d), :] = o
        lacc[g][pl.ds(t0, bq, stride=d), :] = jnp.broadcast_to(lse, (bq, 128))
      return carry

    lax.fori_loop(0, d * nqb, block, 0, unroll=4)

  mrows = 512

  def merge(i, carry):
    r0 = pl.multiple_of(i * mrows, mrows)
    l0 = lacc[0][pl.ds(r0, mrows), :]
    l1 = lacc[1][pl.ds(r0, mrows), :]
    l2 = lacc[2][pl.ds(r0, mrows), :]
    mx = jnp.maximum(jnp.maximum(l0, l1), l2)
    e0 = jnp.exp(l0 - mx)
    e1 = jnp.exp(l1 - mx)
    e2 = jnp.exp(l2 - mx)
    den = e0 + e1 + e2
    acc = (e0 * oacc[0][pl.ds(r0, mrows), :] + e1 * oacc[1][pl.ds(r0, mrows), :]
           + e2 * oacc[2][pl.ds(r0, mrows), :])
    o_ref[pl.ds(r0, mrows), :] = acc / den
    return carry

  lax.fori_loop(0, seq // mrows, merge, 0)


def _attn_prompt(qkvs, slopes, batch, seq):
  in_specs = [pl.BlockSpec(memory_space=pltpu.SMEM)]
  args = [slopes]
  for g, (_, d) in enumerate(A_GROUPS):
    ls = seq // d
    for part in range(3):
      in_specs.append(pl.BlockSpec(
          (None, d, None, ls, 128),
          lambda b, h, part=part: (b, 0, part * A_HEADS + h, 0, 0)))
      args.append(qkvs[g])
  scratch = ([pltpu.VMEM((seq, 128), F32) for _ in range(6)]
             + [pltpu.VMEM((6, ATT_BQ, ATT_BQ + N_DIST), F32)])
  return pl.pallas_call(
      functools.partial(_attn_prompt_kernel, seq=seq),
      grid=(batch, A_HEADS),
      in_specs=in_specs,
      out_specs=pl.BlockSpec((None, seq, 128), lambda b, h: (b, 0, h)),
      out_shape=jax.ShapeDtypeStruct((batch, seq, A_OUT), F32),
      scratch_shapes=scratch,
      compiler_params=_params(("parallel", "arbitrary")),
      name="attn_prompt",
  )(*args)


def _attn_sample_kernel(slopes_ref, q_ref, c0_ref, c1_ref, c2_ref, n0_ref, n1_ref, n2_ref,
                        o_ref, u0_ref, u1_ref, u2_ref, sems, *, tq):
  b = pl.program_id(0)
  cache_refs = (c0_ref, c1_ref, c2_ref)
  new_refs = (n0_ref, n1_ref, n2_ref)
  out_refs = (u0_ref, u1_ref, u2_ref)

  copies = []
  for g, (w, _) in enumerate(A_GROUPS):
    body = pltpu.make_async_copy(cache_refs[g].at[0, pl.ds(tq, w - tq)],
                                 out_refs[g].at[b, pl.ds(0, w - tq)], sems.at[2 * g])
    tail = pltpu.make_async_copy(new_refs[g].at[0], out_refs[g].at[b, pl.ds(w - tq, tq)],
                                 sems.at[2 * g + 1])
    body.start()
    tail.start()
    copies += [body, tail]

  scale = A_HEAD_DIM ** -0.5
  nrow = N_DIST * 8 + tq * 8
  ri = lax.broadcasted_iota(jnp.int32, (nrow, 128), 0)
  cj = lax.broadcasted_iota(jnp.int32, (nrow, 128), 1)
  slot = ri % 8
  key = ri // 8
  tcol = cj // A_HEADS
  hcol = cj % A_HEADS
  col_ok = cj < tq * A_HEADS
  lse_g = []
  ot_g = []
  for g, (w, d) in enumerate(A_GROUPS):
    qg = q_ref[g]
    xn = new_refs[g][0].reshape(tq * 8, 128)
    nres = min(d, tq)
    xs = []
    s = jnp.zeros((nrow, 128), F32)
    col1 = lax.broadcasted_iota(jnp.int32, (128, 1), 0) // A_HEADS
    for rho in range(nres):
      if d == 1:
        xc = cache_refs[g][0]
      else:
        xc = cache_refs[g][0, pl.ds(rho, N_DIST, stride=d), :, :]
      xc = xc.reshape(N_DIST * 8, 128)
      x = jnp.concatenate([xc, xn], axis=0).astype(BF16)
      xs.append(x)
      qm = jnp.where(col1 % d == rho, qg, 0.0).astype(BF16)
      s = s + _dot_nt(x, qm)
    qt = tcol // d
    nidx = key - N_DIST
    in_cache = key < N_DIST
    valid_c = in_cache & (key >= qt)
    dist_c = d * (N_DIST + qt - key)
    valid_n = (~in_cache) & (nidx <= tcol) & ((tcol - nidx) % d == 0)
    dist_n = tcol - nidx
    valid = (valid_c | valid_n) & (slot == hcol) & col_ok
    dist = jnp.where(in_cache, dist_c, dist_n).astype(F32)
    slope = jnp.zeros((nrow, 128), F32)
    for hh in range(A_HEADS):
      slope = jnp.where(hcol == hh, slopes_ref[g, hh], slope)
    s = jnp.where(valid, s * scale - slope * dist, NEG_INF)
    m = jnp.max(s, axis=0, keepdims=True)
    p = jnp.where(valid, jnp.exp(s - m), 0.0)
    l = jnp.sum(p, axis=0, keepdims=True)
    l = jnp.where(l > 0.0, l, 1.0)
    pn = p / l
    pv = pltpu.roll(pn, 4, 0)
    ot = jnp.zeros((128, 128), F32)
    col_row = lax.broadcasted_iota(jnp.int32, (1, 128), 1) // A_HEADS
    for rho in range(nres):
      pm = jnp.where(col_row % d == rho, pv, 0.0).astype(BF16)
      ot = ot + _dot_tn(xs[rho], pm)
    ot_g.append(ot)
    lse_g.append(m + jnp.log(l))
  mx = jnp.maximum(jnp.maximum(lse_g[0], lse_g[1]), lse_g[2])
  es = [jnp.exp(v - mx) for v in lse_g]
  den = es[0] + es[1] + es[2]
  ot = (es[0] * ot_g[0] + es[1] * ot_g[1] + es[2] * ot_g[2]) / den
  o_ref[...] = ot.T

  for c in copies:
    c.wait()


def _attn_sample(q, caches, news, slopes, tq):
  batch = q.shape[0]
  in_specs = [pl.BlockSpec(memory_space=pltpu.SMEM),
              pl.BlockSpec((None, 3, 128, 128), lambda b: (b, 0, 0, 0))]
  args = [slopes, q]
  for g, (w, d) in enumerate(A_GROUPS):
    in_specs.append(pl.BlockSpec((1, w, 8, 128), lambda b: (b, 0, 0, 0)))
    args.append(caches[g])
  for g in range(3):
    in_specs.append(pl.BlockSpec((1, tq, 8, 128), lambda b: (b, 0, 0, 0)))
    args.append(news[g])
  out_shape = [jax.ShapeDtypeStruct((batch, 128, 128), F32)]
  out_specs = [pl.BlockSpec((None, 128, 128), lambda b: (b, 0, 0))]
  for g, (w, _) in enumerate(A_GROUPS):
    out_shape.append(jax.ShapeDtypeStruct((batch, w, 8, 128), F32))
    out_specs.append(pl.BlockSpec(memory_space=pl.ANY))
  return pl.pallas_call(
      functools.partial(_attn_sample_kernel, tq=tq),
      grid=(batch,),
      in_specs=in_specs,
      out_specs=out_specs,
      out_shape=out_shape,
      scratch_shapes=[pltpu.SemaphoreType.DMA((6,))],
      compiler_params=_params(("arbitrary",)),
      name="attn_sample",
  )(*args)


def _retention_kernel(q_ref, k_ref, v_ref, cos_ref, sin_ref, dmat_ref, qdec_ref, kdec_ref,
                      cdec_ref, s0_ref, gn_ref, o_ref, sout_ref, state, *, chunk):
  j = pl.program_id(1)

  @pl.when(j == 0)
  def _():
    state[...] = s0_ref[...]

  nchunk = q_ref.shape[0] // chunk
  kscale = R_KEY_DIM ** -0.5
  for ci in range(nchunk):
    rows = pl.ds(ci * chunk, chunk)
    cosv = cos_ref[rows, :]
    sinv = sin_ref[rows, :]
    for h in range(R_HEADS):
      qh = q_ref[rows, h * 128:(h + 1) * 128]
      kh = k_ref[rows, h * 128:(h + 1) * 128]
      qrot = qh * cosv + pltpu.roll(qh, 64, 1) * sinv
      krot = (kh * cosv + pltpu.roll(kh, 64, 1) * sinv) * kscale
      qb = qrot.astype(BF16)
      vh = v_ref[rows, h * 256:(h + 1) * 256].astype(BF16)
      att = _dot_nt(qb, krot.astype(BF16)) * dmat_ref[h]
      sh = state[h]
      o = _dot(att.astype(BF16), vh) + _dot(qb, sh.astype(BF16)) * qdec_ref[h]
      state[h] = cdec_ref[h] * sh + _dot_tn((krot * kdec_ref[h]).astype(BF16), vh)
      mu = jnp.mean(o, axis=-1, keepdims=True)
      var = jnp.mean(jnp.square(o - mu), axis=-1, keepdims=True)
      on = (o - mu) * lax.rsqrt(var + EPS) * gn_ref[:, h * 256:(h + 1) * 256]
      o_ref[rows, h * 256:(h + 1) * 256] = on

  @pl.when(j == pl.num_programs(1) - 1)
  def _():
    sout_ref[...] = state[...]


def _retention(qkv, cos, sin, tables, s0, gn, rows_per_step, chunk):
  batch, t, _ = qkv.shape
  dmat, qdec, kdec, cdec = tables
  tc = rows_per_step
  const3 = lambda b, j: (0, 0, 0)
  return pl.pallas_call(
      functools.partial(_retention_kernel, chunk=chunk),
      grid=(batch, t // tc),
      in_specs=[pl.BlockSpec((None, tc, R_QK), lambda b, j: (b, j, 0)),
                pl.BlockSpec((None, tc, R_QK), lambda b, j: (b, j, 1)),
                pl.BlockSpec((None, tc, R_V), lambda b, j: (b, j, 1)),
                pl.BlockSpec((tc, 128), lambda b, j: (j, 0)),
                pl.BlockSpec((tc, 128), lambda b, j: (j, 0)),
                pl.BlockSpec(dmat.shape, const3),
                pl.BlockSpec(qdec.shape, const3),
                pl.BlockSpec(kdec.shape, const3),
                pl.BlockSpec(cdec.shape, const3),
                pl.BlockSpec((None, R_HEADS, R_KEY_DIM, R_VAL_DIM), lambda b, j: (b, 0, 0, 0)),
                pl.BlockSpec((1, R_V), lambda b, j: (0, 0))],
      out_specs=[pl.BlockSpec((None, tc, R_V), lambda b, j: (b, j, 0)),
                 pl.BlockSpec((None, R_HEADS, R_KEY_DIM, R_VAL_DIM), lambda b, j: (b, 0, 0, 0))],
      out_shape=[jax.ShapeDtypeStruct((batch, t, R_V), F32),
                 jax.ShapeDtypeStruct((batch, R_HEADS, R_KEY_DIM, R_VAL_DIM), F32)],
      scratch_shapes=[pltpu.VMEM((R_HEADS, R_KEY_DIM, R_VAL_DIM), F32)],
      compiler_params=_params(("parallel", "arbitrary")),
      name="retention",
  )(qkv, qkv, qkv, cos, sin, dmat, qdec, kdec, cdec, s0, gn)


def _retention_tables(c_true, c_pad):
  lg = jnp.log1p(-jnp.exp2(-5.0 - jnp.arange(R_HEADS, dtype=F32)))
  i = jnp.arange(c_pad, dtype=F32)
  live = i < c_true
  diff = i[:, None] - i[None, :]
  causal = (diff >= 0) & live[:, None] & live[None, :]
  dmat = jnp.where(causal[None], jnp.exp(jnp.where(causal, diff, 0.0)[None] * lg[:, None, None]), 0.0)
  qdec = jnp.where(live[None, :], jnp.exp((i[None, :] + 1.0) * lg[:, None]), 0.0)
  kdec = jnp.where(live[None, :], jnp.exp((c_true - 1.0 - i)[None, :] * lg[:, None]), 0.0)
  cdec = jnp.exp(c_true * lg)
  qdec = jnp.broadcast_to(qdec[:, :, None], (R_HEADS, c_pad, R_VAL_DIM))
  kdec = jnp.broadcast_to(kdec[:, :, None], (R_HEADS, c_pad, R_KEY_DIM))
  cdec = jnp.broadcast_to(cdec[:, None, None], (R_HEADS, R_KEY_DIM, R_VAL_DIM))
  return dmat, qdec, kdec, cdec


def _rope_tables(pos):
  half = R_KEY_DIM // 2
  inv = 1.0 / (ROPE_BASE ** jnp.linspace(0.0, 1.0, half, dtype=F32))
  ang = pos.astype(F32)[:, None] * inv[None, :]
  cos = jnp.cos(ang)
  sin = jnp.sin(ang)
  return jnp.concatenate([cos, cos], axis=-1), jnp.concatenate([-sin, sin], axis=-1)


def _output_kernel(x_ref, oa_ref, or_ref, wz_ref, wpa_ref, wpb_ref, wo_ref, lng_ref, lnb_ref, y_ref):
  x = x_ref[...]
  z = _dot(x.astype(BF16), wz_ref[...])
  za = z[:, :A_OUT]
  zr = z[:, A_OUT:A_OUT + R_V]
  ga = z[:, A_OUT + R_V:A_OUT + R_V + D_MODEL]
  gb = z[:, A_OUT + R_V + D_MODEL:]
  ya = _dot((jax.nn.silu(za) * oa_ref[...]).astype(BF16), wpa_ref[...])
  yb = _dot((jax.nn.silu(zr) * or_ref[...]).astype(BF16), wpb_ref[...])
  mix = jax.nn.sigmoid(ga) * ya + jax.nn.sigmoid(gb) * yb
  out = _dot(mix.astype(BF16), wo_ref[...])
  zz = DEEPNORM_ALPHA * x + out
  zm = jnp.mean(zz, axis=-1, keepdims=True)
  zv = jnp.mean(jnp.square(zz - zm), axis=-1, keepdims=True)
  y_ref[...] = (zz - zm) * lax.rsqrt(zv + EPS) * lng_ref[...] + lnb_ref[...]


def _output(x, oa, orr, wz, wpa, wpb, wo, lng, lnb, tm):
  m = x.shape[0]
  row = lambda w: pl.BlockSpec((tm, w), lambda i: (i, 0))
  full = lambda a: pl.BlockSpec(a.shape, lambda i: (0, 0), pipeline_mode=pl.Buffered(1))
  return pl.pallas_call(
      _output_kernel,
      grid=(m // tm,),
      in_specs=[row(D_MODEL), row(A_OUT), row(R_V), full(wz), full(wpa), full(wpb), full(wo),
                full(lng), full(lnb)],
      out_specs=row(D_MODEL),
      out_shape=jax.ShapeDtypeStruct((m, D_MODEL), F32),
      compiler_params=_params(("parallel",)),
      name="gates_output",
  )(x, oa, orr, wz, wpa, wpb, wo, lng, lnb)


def _alibi_slopes():
  n = len(A_GROUPS) * A_HEADS
  return jnp.exp2(-8.0 * jnp.arange(1, n + 1, dtype=F32) / n).reshape(len(A_GROUPS), A_HEADS)


def _w_cols(w_in, *ranges):
  return jnp.concatenate([w_in[:, a:b] for a, b in ranges], axis=1).astype(BF16)


def _group_cols(g):
  return [(_OFF[p] + g * A_GW, _OFF[p] + (g + 1) * A_GW) for p in range(3)]


def kernel(x_prompt, x_sample, cache_kv_w128, cache_kv_w512, cache_kv_w2048, state_ret,
           w_in, w_pa, w_pb, w_o, gn_g, ln_g, ln_b):
  bp, tp, _ = x_prompt.shape
  bs, ts, _ = x_sample.shape
  slopes = _alibi_slopes()
  w_ret = _w_cols(w_in, (_OFF[4], _OFF[7]))
  w_gate = _w_cols(w_in, (_OFF[3], _OFF[4]), (_OFF[7], _OFF[10]))
  w_grp = [_w_cols(w_in, *_group_cols(g)) for g in range(3)]
  wpa = w_pa.astype(BF16)
  wpb = w_pb.astype(BF16)
  wo = w_o.astype(BF16)
  gn = gn_g.reshape(1, R_V)
  lng = ln_g.reshape(1, D_MODEL)
  lnb = ln_b.reshape(1, D_MODEL)

  xp2 = x_prompt.reshape(bp * tp, D_MODEL)
  qkvs = []
  for g, (w, d) in enumerate(A_GROUPS):
    ls = tp // d
    xperm = x_prompt.reshape(bp, ls, d, D_MODEL).transpose(0, 2, 1, 3).astype(BF16)
    qkvs.append(_proj_heads(xperm, w_grp[g], min(ls, 512)))
  oa_p = _attn_prompt(qkvs, slopes, bp, tp).reshape(bp * tp, A_OUT)
  wmax = max(w for w, _ in A_GROUPS)
  w_kv = jnp.concatenate([w_grp[g][:, A_GW:] for g in range(3)], axis=1)
  kv_tail = _matmul(x_prompt[:, tp - wmax:].reshape(bp * wmax, D_MODEL), w_kv, F32, 512, 1024)
  kv_tail = kv_tail.reshape(bp, wmax, 3, 2, A_HEADS, A_HEAD_DIM)
  kv_p = [kv_tail[:, wmax - min(w, tp):, g] for g, (w, _) in enumerate(A_GROUPS)]

  ret_p = _matmul(xp2, w_ret, F32, 512, R_QK)
  cos_p, sin_p = _rope_tables(jnp.arange(tp, dtype=jnp.int32))
  or_p, s_p = _retention(
      ret_p.reshape(bp, tp, 2 * R_QK + R_V), cos_p, sin_p, _retention_tables(R_CHUNK, R_CHUNK),
      jnp.zeros((bp, R_HEADS, R_KEY_DIM, R_VAL_DIM), F32), gn, 512, R_CHUNK)
  y_p = _output(xp2, oa_p, or_p.reshape(bp * tp, R_V), w_gate, wpa, wpb, wo, lng, lnb, 256)

  xs2 = x_sample.reshape(bs * ts, D_MODEL)
  w_s = jnp.concatenate(w_grp + [w_ret], axis=1)
  hs = _matmul(xs2, w_s, F32, bs * ts, 1664)
  caches = (cache_kv_w128, cache_kv_w512, cache_kv_w2048)
  q_s, news = [], []
  for g in range(3):
    base = g * 3 * A_GW
    qg = hs[:, base:base + A_GW].reshape(bs, ts * A_HEADS, A_HEAD_DIM)
    q_s.append(jnp.pad(qg, ((0, 0), (0, 128 - ts * A_HEADS), (0, 0))))
    news.append(hs[:, base + A_GW:base + 3 * A_GW].reshape(bs, ts, 8, 128))
  q_s = jnp.stack(q_s, axis=1)
  caches8 = [c.reshape(bs, c.shape[1], 8, 128) for c in caches]
  oa_s, u0, u1, u2 = _attn_sample(q_s, caches8, news, slopes, ts)
  oa_s = oa_s[:, :ts * A_HEADS].reshape(bs * ts, A_OUT)
  kv_s = [u.reshape(bs, u.shape[1], 2, A_HEADS, A_HEAD_DIM) for u in (u0, u1, u2)]

  rb = 3 * 3 * A_GW
  cpad = R_CHUNK
  ret_s = jnp.pad(hs[:, rb:].reshape(bs, ts, 2 * R_QK + R_V), ((0, 0), (0, cpad - ts), (0, 0)))
  cos_s, sin_s = _rope_tables(PAST_LEN + jnp.arange(ts, dtype=jnp.int32))
  padtab = lambda a: jnp.pad(a, ((0, cpad - ts), (0, 0)))
  or_s, s_s = _retention(
      ret_s, padtab(cos_s), padtab(sin_s),
      _retention_tables(ts, cpad), state_ret.astype(F32), gn, cpad, cpad)
  or_s = or_s[:, :ts].reshape(bs * ts, R_V)
  y_s = _output(xs2, oa_s, or_s, w_gate, wpa, wpb, wo, lng, lnb, 256)

  return (y_p.reshape(bp, tp, D_MODEL), y_s.reshape(bs, ts, D_MODEL),
          kv_p[0], kv_p[1], kv_p[2], s_p,
          kv_s[0], kv_s[1], kv_s[2], s_s)
```

```python
import functools

import numpy as np
import jax
import jax.numpy as jnp
from jax import lax
from jax.experimental import pallas as pl
from jax.experimental.pallas import tpu as pltpu

D_MODEL = 1024
PAST_LEN = 16384
A_GROUPS = ((128, 1), (512, 4), (2048, 16))
A_HEADS = 4
A_HEAD_DIM = 128
A_GW = A_HEADS * A_HEAD_DIM
A_QKV = len(A_GROUPS) * A_GW
A_OUT = A_GW
N_DIST = 128
R_HEADS = 4
R_KEY_DIM = 128
R_VAL_DIM = 256
R_QK = R_HEADS * R_KEY_DIM
R_V = R_HEADS * R_VAL_DIM
R_CHUNK = 128
ROPE_BASE = 10000.0
EPS = 1e-5
NEG_INF = -1e30
DEEPNORM_ALPHA = 2.0 ** 0.25
_SPLIT = (A_QKV, A_QKV, A_QKV, A_OUT, R_QK, R_QK, R_V, R_V, D_MODEL, D_MODEL)
_OFF = tuple(int(v) for v in np.concatenate([[0], np.cumsum(_SPLIT)]))

BF16 = jnp.bfloat16
F32 = jnp.float32
VMEM_LIMIT = 56 * 1024 * 1024

_NT = (((1,), (1,)), ((), ()))
_TN = (((0,), (0,)), ((), ()))


def _dot(a, b):
  return jnp.dot(a, b, preferred_element_type=F32)


def _dot_nt(a, b):
  return lax.dot_general(a, b, _NT, preferred_element_type=F32)


def _dot_tn(a, b):
  return lax.dot_general(a, b, _TN, preferred_element_type=F32)


def _params(sem):
  return pltpu.CompilerParams(dimension_semantics=sem, vmem_limit_bytes=VMEM_LIMIT)


def _mm_kernel(x_ref, w_ref, o_ref):
  o_ref[...] = _dot(x_ref[...].astype(BF16), w_ref[...]).astype(o_ref.dtype)


def _matmul(x, w, col0, ncols, out_dtype, tm, tn):
  m, k = x.shape
  c0 = col0 // tn
  return pl.pallas_call(
      _mm_kernel,
      grid=(m // tm, ncols // tn),
      in_specs=[pl.BlockSpec((tm, k), lambda i, j: (i, 0)),
                pl.BlockSpec((k, tn), lambda i, j: (0, c0 + j))],
      out_specs=pl.BlockSpec((tm, tn), lambda i, j: (i, j)),
      out_shape=jax.ShapeDtypeStruct((m, ncols), out_dtype),
      compiler_params=_params(("parallel", "arbitrary")),
      name="proj_matmul",
  )(x, w)


PROJ_TM = 512


def _proj_group_kernel(x_ref, wq_ref, wk_ref, wv_ref, o_ref, tail_ref, *scratch,
                       d, first_tail, tail_rows):
  i = pl.program_id(1)
  tm = x_ref.shape[0]
  n = tm // d
  x = x_ref[...]
  xb = x.astype(BF16)
  if d == 1:
    perm = xb
  else:
    xs_ref, xp_ref = scratch
    for c in range(D_MODEL // 128):
      xs_ref[c] = x[:, c * 128:(c + 1) * 128]
    for r in range(d):
      for c in range(D_MODEL // 128):
        xp_ref[r * n:(r + 1) * n, c * 128:(c + 1) * 128] = (
            xs_ref[c, pl.ds(r, n, stride=d), :].astype(BF16))
    perm = xp_ref[...]
  nat = []
  for part, w_ref in enumerate((wq_ref, wk_ref, wv_ref)):
    res = _dot(perm, w_ref[...])
    nat.append(res)
    for r in range(d):
      for h in range(A_HEADS):
        o_ref[r, part * A_HEADS + h] = res[r * n:(r + 1) * n, h * 128:(h + 1) * 128].astype(BF16)

  @pl.when(i >= first_tail)
  def _():
    if d == 1:
      kv = nat[1:]
    else:
      kv = [_dot(xb, wk_ref[...]), _dot(xb, wv_ref[...])]
    for c in range(2 * A_HEADS):
      src = kv[c // A_HEADS][tm - tail_rows:, (c % A_HEADS) * 128:(c % A_HEADS + 1) * 128]
      tail_ref[pl.ds(c, tail_rows, stride=2 * A_HEADS), :] = src


def _proj_group(x2, w16, g, batch, seq):
  w, d = A_GROUPS[g]
  w = min(w, seq)
  tm = PROJ_TM
  nblk = seq // tm
  tail_rows = min(tm, w)
  first_tail = nblk - max(w // tm, 1)
  ls = seq // d
  wspec = lambda part: pl.BlockSpec((D_MODEL, A_GW), lambda b, i: (0, 3 * part + g))
  scratch = []
  if d > 1:
    scratch = [pltpu.VMEM((D_MODEL // 128, tm, 128), F32), pltpu.VMEM((tm, D_MODEL), BF16)]
  return pl.pallas_call(
      functools.partial(_proj_group_kernel, d=d, first_tail=first_tail, tail_rows=tail_rows),
      grid=(batch, nblk),
      in_specs=[pl.BlockSpec((tm, D_MODEL), lambda b, i: (b * nblk + i, 0)),
                wspec(0), wspec(1), wspec(2)],
      out_specs=[pl.BlockSpec((None, d, 3 * A_HEADS, tm // d, 128), lambda b, i: (b, 0, 0, i, 0)),
                 pl.BlockSpec((None, tail_rows * 8, 128),
                              lambda b, i: (b, jnp.maximum(i - first_tail, 0), 0))],
      out_shape=[jax.ShapeDtypeStruct((batch, d, 3 * A_HEADS, ls, 128), BF16),
                 jax.ShapeDtypeStruct((batch, w * 8, 128), F32)],
      scratch_shapes=scratch,
      compiler_params=_params(("parallel", "arbitrary")),
      name="proj_group",
  )(x2, w16, w16, w16)


ATT_BQ = 256


def _attn_prompt_kernel(slopes_ref, *refs, seq):
  qkv = refs[:9]
  o_ref = refs[9]
  oacc = refs[10:13]
  lacc = refs[13:16]
  bias_ref = refs[16]
  h = pl.program_id(1)
  scale = A_HEAD_DIM ** -0.5

  for g, (_, d) in enumerate(A_GROUPS):
    q_ref, k_ref, v_ref = qkv[3 * g:3 * g + 3]
    ls = seq // d
    bq = min(ATT_BQ, ls)
    nqb = ls // bq
    klen = min(bq + N_DIST, ls)
    slope = slopes_ref[g, h]
    qi = lax.broadcasted_iota(jnp.int32, (bq, klen), 0)
    kj = lax.broadcasted_iota(jnp.int32, (bq, klen), 1)
    for which, off in enumerate((0, N_DIST)):
      delta = qi - kj + off
      valid = (delta >= 0) & (delta <= N_DIST)
      dist = (delta * d).astype(F32)
      bias_ref[2 * g + which, :bq, :klen] = jnp.where(valid, -slope * dist, NEG_INF)

    def block(idx, carry, g=g, d=d, bq=bq, nqb=nqb, klen=klen,
              q_ref=q_ref, k_ref=k_ref, v_ref=v_ref):
      r = idx // nqb
      qb = idx % nqb
      q0 = pl.multiple_of(qb * bq, bq)
      k0 = pl.multiple_of(jnp.maximum(qb * bq - N_DIST, 0), N_DIST)
      q = q_ref[r, pl.ds(q0, bq), :]
      k = k_ref[r, pl.ds(k0, klen), :]
      v = v_ref[r, pl.ds(k0, klen), :]
      bias = bias_ref[2 * g + jnp.minimum(qb, 1), :bq, :klen]
      s = _dot_nt(q, k) * scale + bias
      m = jnp.max(s, axis=-1, keepdims=True)
      p = jnp.exp(s - m)
      l = jnp.sum(p, axis=-1, keepdims=True)
      o = _dot(p.astype(BF16), v) / l
      lse = m + jnp.log(l)
      t0 = r + q0 * d
      if d == 1:
        oacc[g][pl.ds(q0, bq), :] = o
        lacc[g][pl.ds(q0, bq), :] = jnp.broadcast_to(lse, (bq, 128))
      else:
        oacc[g][pl.ds(t0, bq, stride=d), :] = o
        lacc[g][pl.ds(t0, bq, stride=d), :] = jnp.broadcast_to(lse, (bq, 128))
      return carry

    lax.fori_loop(0, d * nqb, block, 0, unroll=4)

  mrows = 512

  def merge(i, carry):
    r0 = pl.multiple_of(i * mrows, mrows)
    l0 = lacc[0][pl.ds(r0, mrows), :]
    l1 = lacc[1][pl.ds(r0, mrows), :]
    l2 = lacc[2][pl.ds(r0, mrows), :]
    mx = jnp.maximum(jnp.maximum(l0, l1), l2)
    e0 = jnp.exp(l0 - mx)
    e1 = jnp.exp(l1 - mx)
    e2 = jnp.exp(l2 - mx)
    den = e0 + e1 + e2
    acc = (e0 * oacc[0][pl.ds(r0, mrows), :] + e1 * oacc[1][pl.ds(r0, mrows), :]
           + e2 * oacc[2][pl.ds(r0, mrows), :])
    o_ref[pl.ds(r0, mrows), :] = acc / den
    return carry

  lax.fori_loop(0, seq // mrows, merge, 0)


def _attn_prompt(qkvs, slopes, batch, seq):
  in_specs = [pl.BlockSpec(memory_space=pltpu.SMEM)]
  args = [slopes]
  for g, (_, d) in enumerate(A_GROUPS):
    ls = seq // d
    for part in range(3):
      in_specs.append(pl.BlockSpec(
          (None, d, None, ls, 128),
          lambda b, h, part=part: (b, 0, part * A_HEADS + h, 0, 0)))
      args.append(qkvs[g])
  scratch = ([pltpu.VMEM((seq, 128), F32) for _ in range(6)]
             + [pltpu.VMEM((6, ATT_BQ, ATT_BQ + N_DIST), F32)])
  return pl.pallas_call(
      functools.partial(_attn_prompt_kernel, seq=seq),
      grid=(batch, A_HEADS),
      in_specs=in_specs,
      out_specs=pl.BlockSpec((None, seq, 128), lambda b, h: (b, 0, h)),
      out_shape=jax.ShapeDtypeStruct((batch, seq, A_OUT), F32),
      scratch_shapes=scratch,
      compiler_params=_params(("parallel", "arbitrary")),
      name="attn_prompt",
  )(*args)


def _attn_sample_kernel(slopes_ref, q_ref, c0_ref, c1_ref, c2_ref, n0_ref, n1_ref, n2_ref,
                        o_ref, u0_ref, u1_ref, u2_ref, sems, *, tq):
  b = pl.program_id(0)
  cache_refs = (c0_ref, c1_ref, c2_ref)
  new_refs = (n0_ref, n1_ref, n2_ref)
  out_refs = (u0_ref, u1_ref, u2_ref)

  copies = []
  for g, (w, _) in enumerate(A_GROUPS):
    body = pltpu.make_async_copy(cache_refs[g].at[0, pl.ds(tq, w - tq)],
                                 out_refs[g].at[b, pl.ds(0, w - tq)], sems.at[2 * g])
    tail = pltpu.make_async_copy(new_refs[g].at[0], out_refs[g].at[b, pl.ds(w - tq, tq)],
                                 sems.at[2 * g + 1])
    body.start()
    tail.start()
    copies += [body, tail]

  scale = A_HEAD_DIM ** -0.5
  nrow = N_DIST * 8 + tq * 8
  ri = lax.broadcasted_iota(jnp.int32, (nrow, 128), 0)
  cj = lax.broadcasted_iota(jnp.int32, (nrow, 128), 1)
  slot = ri % 8
  key = ri // 8
  tcol = cj // A_HEADS
  hcol = cj % A_HEADS
  col_ok = cj < tq * A_HEADS
  lse_g = []
  ot_g = []
  for g, (w, d) in enumerate(A_GROUPS):
    qg = q_ref[g]
    xn = new_refs[g][0].reshape(tq * 8, 128)
    nres = min(d, tq)
    xs = []
    s = jnp.zeros((nrow, 128), F32)
    col1 = lax.broadcasted_iota(jnp.int32, (128, 1), 0) // A_HEADS
    for rho in range(nres):
      if d == 1:
        xc = cache_refs[g][0]
      else:
        xc = cache_refs[g][0, pl.ds(rho, N_DIST, stride=d), :, :]
      xc = xc.reshape(N_DIST * 8, 128)
      x = jnp.concatenate([xc, xn], axis=0).astype(BF16)
      xs.append(x)
      qm = jnp.where(col1 % d == rho, qg, 0.0).astype(BF16)
      s = s + _dot_nt(x, qm)
    qt = tcol // d
    nidx = key - N_DIST
    in_cache = key < N_DIST
    valid_c = in_cache & (key >= qt)
    dist_c = d * (N_DIST + qt - key)
    valid_n = (~in_cache) & (nidx <= tcol) & ((tcol - nidx) % d == 0)
    dist_n = tcol - nidx
    valid = (valid_c | valid_n) & (slot == hcol) & col_ok
    dist = jnp.where(in_cache, dist_c, dist_n).astype(F32)
    slope = jnp.zeros((nrow, 128), F32)
    for hh in range(A_HEADS):
      slope = jnp.where(hcol == hh, slopes_ref[g, hh], slope)
    s = jnp.where(valid, s * scale - slope * dist, NEG_INF)
    m = jnp.max(s, axis=0, keepdims=True)
    p = jnp.where(valid, jnp.exp(s - m), 0.0)
    l = jnp.sum(p, axis=0, keepdims=True)
    l = jnp.where(l > 0.0, l, 1.0)
    pn = p / l
    pv = pltpu.roll(pn, 4, 0)
    ot = jnp.zeros((128, 128), F32)
    col_row = lax.broadcasted_iota(jnp.int32, (1, 128), 1) // A_HEADS
    for rho in range(nres):
      pm = jnp.where(col_row % d == rho, pv, 0.0).astype(BF16)
      ot = ot + _dot_tn(xs[rho], pm)
    ot_g.append(ot)
    lse_g.append(m + jnp.log(l))
  mx = jnp.maximum(jnp.maximum(lse_g[0], lse_g[1]), lse_g[2])
  es = [jnp.exp(v - mx) for v in lse_g]
  den = es[0] + es[1] + es[2]
  ot = (es[0] * ot_g[0] + es[1] * ot_g[1] + es[2] * ot_g[2]) / den
  o_ref[...] = ot.T

  for c in copies:
    c.wait()


def _attn_sample(q, caches, news, slopes, tq):
  batch = q.shape[0]
  in_specs = [pl.BlockSpec(memory_space=pltpu.SMEM),
              pl.BlockSpec((None, 3, 128, 128), lambda b: (b, 0, 0, 0))]
  args = [slopes, q]
  for g, (w, d) in enumerate(A_GROUPS):
    in_specs.append(pl.BlockSpec((1, w, 8, 128), lambda b: (b, 0, 0, 0)))
    args.append(caches[g])
  for g in range(3):
    in_specs.append(pl.BlockSpec((1, tq, 8, 128), lambda b: (b, 0, 0, 0)))
    args.append(news[g])
  out_shape = [jax.ShapeDtypeStruct((batch, 128, 128), F32)]
  out_specs = [pl.BlockSpec((None, 128, 128), lambda b: (b, 0, 0))]
  for g, (w, _) in enumerate(A_GROUPS):
    out_shape.append(jax.ShapeDtypeStruct((batch, w, 8, 128), F32))
    out_specs.append(pl.BlockSpec(memory_space=pl.ANY))
  return pl.pallas_call(
      functools.partial(_attn_sample_kernel, tq=tq),
      grid=(batch,),
      in_specs=in_specs,
      out_specs=out_specs,
      out_shape=out_shape,
      scratch_shapes=[pltpu.SemaphoreType.DMA((6,))],
      compiler_params=_params(("arbitrary",)),
      name="attn_sample",
  )(*args)


def _retention_kernel(q_ref, k_ref, v_ref, cos_ref, sin_ref, dmat_ref, qdec_ref, kdec_ref,
                      cdec_ref, s0_ref, gn_ref, o_ref, sout_ref, state, *, chunk):
  j = pl.program_id(1)

  @pl.when(j == 0)
  def _():
    state[...] = s0_ref[...]

  nchunk = q_ref.shape[0] // chunk
  kscale = R_KEY_DIM ** -0.5
  for ci in range(nchunk):
    rows = pl.ds(ci * chunk, chunk)
    cosv = cos_ref[rows, :]
    sinv = sin_ref[rows, :]
    for h in range(R_HEADS):
      qh = q_ref[rows, h * 128:(h + 1) * 128]
      kh = k_ref[rows, h * 128:(h + 1) * 128]
      qrot = qh * cosv + pltpu.roll(qh, 64, 1) * sinv
      krot = (kh * cosv + pltpu.roll(kh, 64, 1) * sinv) * kscale
      qb = qrot.astype(BF16)
      vh = v_ref[rows, h * 256:(h + 1) * 256].astype(BF16)
      att = _dot_nt(qb, krot.astype(BF16)) * dmat_ref[h]
      sh = state[h]
      o = _dot(att.astype(BF16), vh) + _dot(qb, sh.astype(BF16)) * qdec_ref[h]
      state[h] = cdec_ref[h] * sh + _dot_tn((krot * kdec_ref[h]).astype(BF16), vh)
      mu = jnp.mean(o, axis=-1, keepdims=True)
      var = jnp.mean(jnp.square(o - mu), axis=-1, keepdims=True)
      on = (o - mu) * lax.rsqrt(var + EPS) * gn_ref[:, h * 256:(h + 1) * 256]
      o_ref[rows, h * 256:(h + 1) * 256] = on

  @pl.when(j == pl.num_programs(1) - 1)
  def _():
    sout_ref[...] = state[...]


def _retention(qkv, cos, sin, tables, s0, gn, rows_per_step, chunk):
  batch, t, _ = qkv.shape
  dmat, qdec, kdec, cdec = tables
  tc = rows_per_step
  const3 = lambda b, j: (0, 0, 0)
  return pl.pallas_call(
      functools.partial(_retention_kernel, chunk=chunk),
      grid=(batch, t // tc),
      in_specs=[pl.BlockSpec((None, tc, R_QK), lambda b, j: (b, j, 0)),
                pl.BlockSpec((None, tc, R_QK), lambda b, j: (b, j, 1)),
                pl.BlockSpec((None, tc, R_V), lambda b, j: (b, j, 1)),
                pl.BlockSpec((tc, 128), lambda b, j: (j, 0)),
                pl.BlockSpec((tc, 128), lambda b, j: (j, 0)),
                pl.BlockSpec(dmat.shape, const3),
                pl.BlockSpec(qdec.shape, const3),
                pl.BlockSpec(kdec.shape, const3),
                pl.BlockSpec(cdec.shape, const3),
                pl.BlockSpec((None, R_HEADS, R_KEY_DIM, R_VAL_DIM), lambda b, j: (b, 0, 0, 0)),
                pl.BlockSpec((1, R_V), lambda b, j: (0, 0))],
      out_specs=[pl.BlockSpec((None, tc, R_V), lambda b, j: (b, j, 0)),
                 pl.BlockSpec((None, R_HEADS, R_KEY_DIM, R_VAL_DIM), lambda b, j: (b, 0, 0, 0))],
      out_shape=[jax.ShapeDtypeStruct((batch, t, R_V), F32),
                 jax.ShapeDtypeStruct((batch, R_HEADS, R_KEY_DIM, R_VAL_DIM), F32)],
      scratch_shapes=[pltpu.VMEM((R_HEADS, R_KEY_DIM, R_VAL_DIM), F32)],
      compiler_params=_params(("parallel", "arbitrary")),
      name="retention",
  )(qkv, qkv, qkv, cos, sin, dmat, qdec, kdec, cdec, s0, gn)


def _retention_tables(c_true, c_pad):
  lg = jnp.log1p(-jnp.exp2(-5.0 - jnp.arange(R_HEADS, dtype=F32)))
  i = jnp.arange(c_pad, dtype=F32)
  live = i < c_true
  diff = i[:, None] - i[None, :]
  causal = (diff >= 0) & live[:, None] & live[None, :]
  dmat = jnp.where(causal[None], jnp.exp(jnp.where(causal, diff, 0.0)[None] * lg[:, None, None]), 0.0)
  qdec = jnp.where(live[None, :], jnp.exp((i[None, :] + 1.0) * lg[:, None]), 0.0)
  kdec = jnp.where(live[None, :], jnp.exp((c_true - 1.0 - i)[None, :] * lg[:, None]), 0.0)
  cdec = jnp.exp(c_true * lg)
  qdec = jnp.broadcast_to(qdec[:, :, None], (R_HEADS, c_pad, R_VAL_DIM))
  kdec = jnp.broadcast_to(kdec[:, :, None], (R_HEADS, c_pad, R_KEY_DIM))
  cdec = jnp.broadcast_to(cdec[:, None, None], (R_HEADS, R_KEY_DIM, R_VAL_DIM))
  return dmat, qdec, kdec, cdec


def _rope_tables(pos):
  half = R_KEY_DIM // 2
  inv = 1.0 / (ROPE_BASE ** jnp.linspace(0.0, 1.0, half, dtype=F32))
  ang = pos.astype(F32)[:, None] * inv[None, :]
  cos = jnp.cos(ang)
  sin = jnp.sin(ang)
  return jnp.concatenate([cos, cos], axis=-1), jnp.concatenate([-sin, sin], axis=-1)


def _output_kernel(x_ref, oa_ref, or_ref, wza_ref, wzr_ref, wga_ref, wgb_ref, wpa_ref, wpb_ref,
                   wo_ref, lng_ref, lnb_ref, y_ref):
  x = x_ref[...]
  xb = x.astype(BF16)
  za = _dot(xb, wza_ref[...])
  ya = _dot((jax.nn.silu(za) * oa_ref[...]).astype(BF16), wpa_ref[...])
  zr = _dot(xb, wzr_ref[...])
  yb = _dot((jax.nn.silu(zr) * or_ref[...]).astype(BF16), wpb_ref[...])
  ga = _dot(xb, wga_ref[...])
  gb = _dot(xb, wgb_ref[...])
  mix = jax.nn.sigmoid(ga) * ya + jax.nn.sigmoid(gb) * yb
  out = _dot(mix.astype(BF16), wo_ref[...])
  zz = DEEPNORM_ALPHA * x + out
  zm = jnp.mean(zz, axis=-1, keepdims=True)
  zv = jnp.mean(jnp.square(zz - zm), axis=-1, keepdims=True)
  y_ref[...] = (zz - zm) * lax.rsqrt(zv + EPS) * lng_ref[...] + lnb_ref[...]


def _output(x, oa, orr, w16, wpa, wpb, wo, lng, lnb, tm):
  m = x.shape[0]
  row = lambda w: pl.BlockSpec((tm, w), lambda i: (i, 0))
  full = lambda a: pl.BlockSpec(a.shape, lambda i: (0, 0), pipeline_mode=pl.Buffered(1))
  wcol = lambda off, width: pl.BlockSpec((D_MODEL, width), lambda i: (0, off // width),
                                         pipeline_mode=pl.Buffered(1))
  return pl.pallas_call(
      _output_kernel,
      grid=(m // tm,),
      in_specs=[row(D_MODEL), row(A_OUT), row(R_V),
                wcol(_OFF[3], A_OUT), wcol(_OFF[7], R_V), wcol(_OFF[8], D_MODEL),
                wcol(_OFF[9], D_MODEL),
                full(wpa), full(wpb), full(wo), full(lng), full(lnb)],
      out_specs=row(D_MODEL),
      out_shape=jax.ShapeDtypeStruct((m, D_MODEL), F32),
      compiler_params=_params(("parallel",)),
      name="gates_output",
  )(x, oa, orr, w16, w16, w16, w16, wpa, wpb, wo, lng, lnb)


def _alibi_slopes():
  n = len(A_GROUPS) * A_HEADS
  return jnp.exp2(-8.0 * jnp.arange(1, n + 1, dtype=F32) / n).reshape(len(A_GROUPS), A_HEADS)


def kernel(x_prompt, x_sample, cache_kv_w128, cache_kv_w512, cache_kv_w2048, state_ret,
           w_in, w_pa, w_pb, w_o, gn_g, ln_g, ln_b):
  bp, tp, _ = x_prompt.shape
  bs, ts, _ = x_sample.shape
  slopes = _alibi_slopes()
  w16 = w_in.astype(BF16)
  n_ret = 2 * R_QK + R_V
  wpa = w_pa.astype(BF16)
  wpb = w_pb.astype(BF16)
  wo = w_o.astype(BF16)
  gn = gn_g.reshape(1, R_V)
  lng = ln_g.reshape(1, D_MODEL)
  lnb = ln_b.reshape(1, D_MODEL)

  xp2 = x_prompt.reshape(bp * tp, D_MODEL)
  qkvs, kv_p = [], []
  for g, (w, _) in enumerate(A_GROUPS):
    qkv, tail = _proj_group(xp2, w16, g, bp, tp)
    qkvs.append(qkv)
    kv_p.append(tail.reshape(bp, min(w, tp), 2, A_HEADS, A_HEAD_DIM))
  oa_p = _attn_prompt(qkvs, slopes, bp, tp).reshape(bp * tp, A_OUT)

  ret_p = _matmul(xp2, w16, _OFF[4], n_ret, F32, 512, R_QK)
  cos_p, sin_p = _rope_tables(jnp.arange(tp, dtype=jnp.int32))
  or_p, s_p = _retention(
      ret_p.reshape(bp, tp, n_ret), cos_p, sin_p, _retention_tables(R_CHUNK, R_CHUNK),
      jnp.zeros((bp, R_HEADS, R_KEY_DIM, R_VAL_DIM), F32), gn, 512, R_CHUNK)
  y_p = _output(xp2, oa_p, or_p.reshape(bp * tp, R_V), w16, wpa, wpb, wo, lng, lnb, 256)

  xs2 = x_sample.reshape(bs * ts, D_MODEL)
  hs = _matmul(xs2, w16, 0, _OFF[7], F32, bs * ts, A_GW)
  caches = (cache_kv_w128, cache_kv_w512, cache_kv_w2048)
  q_s, news = [], []
  for g in range(3):
    col = lambda part: hs[:, _OFF[part] + g * A_GW:_OFF[part] + (g + 1) * A_GW]
    qg = col(0).reshape(bs, ts * A_HEADS, A_HEAD_DIM)
    q_s.append(jnp.pad(qg, ((0, 0), (0, 128 - ts * A_HEADS), (0, 0))))
    news.append(jnp.concatenate([col(1), col(2)], axis=1).reshape(bs, ts, 8, 128))
  q_s = jnp.stack(q_s, axis=1)
  caches8 = [c.reshape(bs, c.shape[1], 8, 128) for c in caches]
  oa_s, u0, u1, u2 = _attn_sample(q_s, caches8, news, slopes, ts)
  oa_s = oa_s[:, :ts * A_HEADS].reshape(bs * ts, A_OUT)
  kv_s = [u.reshape(bs, u.shape[1], 2, A_HEADS, A_HEAD_DIM) for u in (u0, u1, u2)]

  cpad = R_CHUNK
  ret_s = jnp.pad(hs[:, _OFF[4]:].reshape(bs, ts, n_ret), ((0, 0), (0, cpad - ts), (0, 0)))
  cos_s, sin_s = _rope_tables(PAST_LEN + jnp.arange(ts, dtype=jnp.int32))
  padtab = lambda a: jnp.pad(a, ((0, cpad - ts), (0, 0)))
  or_s, s_s = _retention(
      ret_s, padtab(cos_s), padtab(sin_s),
      _retention_tables(ts, cpad), state_ret.astype(F32), gn, cpad, cpad)
  or_s = or_s[:, :ts].reshape(bs * ts, R_V)
  y_s = _output(xs2, oa_s, or_s, w16, wpa, wpb, wo, lng, lnb, 256)

  return (y_p.reshape(bp, tp, D_MODEL), y_s.reshape(bs, ts, D_MODEL),
          kv_p[0], kv_p[1], kv_p[2], s_p,
          kv_s[0], kv_s[1], kv_s[2], s_s)
```

```python
import functools

import numpy as np
import jax
import jax.numpy as jnp
from jax import lax
from jax.experimental import pallas as pl
from jax.experimental.pallas import tpu as pltpu

D_MODEL = 1024
PAST_LEN = 16384
A_GROUPS = ((128, 1), (512, 4), (2048, 16))
A_HEADS = 4
A_HEAD_DIM = 128
A_GW = A_HEADS * A_HEAD_DIM
A_QKV = len(A_GROUPS) * A_GW
A_OUT = A_GW
N_DIST = 128
R_HEADS = 4
R_KEY_DIM = 128
R_VAL_DIM = 256
R_QK = R_HEADS * R_KEY_DIM
R_V = R_HEADS * R_VAL_DIM
R_CHUNK = 128
ROPE_BASE = 10000.0
EPS = 1e-5
NEG_INF = -1e30
DEEPNORM_ALPHA = 2.0 ** 0.25
_SPLIT = (A_QKV, A_QKV, A_QKV, A_OUT, R_QK, R_QK, R_V, R_V, D_MODEL, D_MODEL)
_OFF = tuple(int(v) for v in np.concatenate([[0], np.cumsum(_SPLIT)]))

BF16 = jnp.bfloat16
F32 = jnp.float32
VMEM_LIMIT = 56 * 1024 * 1024

_NT = (((1,), (1,)), ((), ()))
_TN = (((0,), (0,)), ((), ()))


def _dot(a, b):
  return jnp.dot(a, b, preferred_element_type=F32)


def _dot_nt(a, b):
  return lax.dot_general(a, b, _NT, preferred_element_type=F32)


def _dot_tn(a, b):
  return lax.dot_general(a, b, _TN, preferred_element_type=F32)


def _params(sem):
  return pltpu.CompilerParams(dimension_semantics=sem, vmem_limit_bytes=VMEM_LIMIT)


def _mm_kernel(x_ref, w_ref, o_ref):
  o_ref[...] = _dot(x_ref[...].astype(BF16), w_ref[...]).astype(o_ref.dtype)


def _matmul(x, w, col0, ncols, out_dtype, tm, tn):
  m, k = x.shape
  c0 = col0 // tn
  return pl.pallas_call(
      _mm_kernel,
      grid=(m // tm, ncols // tn),
      in_specs=[pl.BlockSpec((tm, k), lambda i, j: (i, 0)),
                pl.BlockSpec((k, tn), lambda i, j: (0, c0 + j))],
      out_specs=pl.BlockSpec((tm, tn), lambda i, j: (i, j)),
      out_shape=jax.ShapeDtypeStruct((m, ncols), out_dtype),
      compiler_params=_params(("parallel", "arbitrary")),
      name="proj_matmul",
  )(x, w)


PROJ_TM = 512


def _proj_group_kernel(x_ref, wq_ref, wk_ref, wv_ref, o_ref, tail_ref, *scratch,
                       d, first_tail, tail_rows):
  i = pl.program_id(1)
  tm = x_ref.shape[0]
  n = tm // d
  x = x_ref[...]
  xb = x.astype(BF16)
  if d == 1:
    perm = xb
  else:
    xs_ref, xp_ref = scratch
    for c in range(D_MODEL // 128):
      xs_ref[c] = x[:, c * 128:(c + 1) * 128]
    for r in range(d):
      for c in range(D_MODEL // 128):
        xp_ref[r * n:(r + 1) * n, c * 128:(c + 1) * 128] = (
            xs_ref[c, pl.ds(r, n, stride=d), :].astype(BF16))
    perm = xp_ref[...]
  nat = []
  for part, w_ref in enumerate((wq_ref, wk_ref, wv_ref)):
    res = _dot(perm, w_ref[...])
    nat.append(res)
    for r in range(d):
      for h in range(A_HEADS):
        o_ref[r, part * A_HEADS + h] = res[r * n:(r + 1) * n, h * 128:(h + 1) * 128].astype(BF16)

  @pl.when(i >= first_tail)
  def _():
    if d == 1:
      kv = nat[1:]
    else:
      kv = [_dot(xb, wk_ref[...]), _dot(xb, wv_ref[...])]
    for c in range(2 * A_HEADS):
      src = kv[c // A_HEADS][tm - tail_rows:, (c % A_HEADS) * 128:(c % A_HEADS + 1) * 128]
      tail_ref[pl.ds(c, tail_rows, stride=2 * A_HEADS), :] = src


def _proj_group(x2, w16, g, batch, seq):
  w, d = A_GROUPS[g]
  w = min(w, seq)
  tm = PROJ_TM
  nblk = seq // tm
  tail_rows = min(tm, w)
  first_tail = nblk - max(w // tm, 1)
  ls = seq // d
  wspec = lambda part: pl.BlockSpec((D_MODEL, A_GW), lambda b, i: (0, 3 * part + g))
  scratch = []
  if d > 1:
    scratch = [pltpu.VMEM((D_MODEL // 128, tm, 128), F32), pltpu.VMEM((tm, D_MODEL), BF16)]
  return pl.pallas_call(
      functools.partial(_proj_group_kernel, d=d, first_tail=first_tail, tail_rows=tail_rows),
      grid=(batch, nblk),
      in_specs=[pl.BlockSpec((tm, D_MODEL), lambda b, i: (b * nblk + i, 0)),
                wspec(0), wspec(1), wspec(2)],
      out_specs=[pl.BlockSpec((None, d, 3 * A_HEADS, tm // d, 128), lambda b, i: (b, 0, 0, i, 0)),
                 pl.BlockSpec((None, tail_rows * 8, 128),
                              lambda b, i: (b, jnp.maximum(i - first_tail, 0), 0))],
      out_shape=[jax.ShapeDtypeStruct((batch, d, 3 * A_HEADS, ls, 128), BF16),
                 jax.ShapeDtypeStruct((batch, w * 8, 128), F32)],
      scratch_shapes=scratch,
      compiler_params=_params(("parallel", "arbitrary")),
      name="proj_group",
  )(x2, w16, w16, w16)


ATT_BQ = 256


def _attn_prompt_kernel(slopes_ref, *refs, seq):
  qkv = refs[:9]
  o_ref = refs[9]
  oacc = refs[10:13]
  lacc = refs[13:16]
  bias_ref = refs[16]
  h = pl.program_id(1)
  scale = A_HEAD_DIM ** -0.5

  for g, (_, d) in enumerate(A_GROUPS):
    q_ref, k_ref, v_ref = qkv[3 * g:3 * g + 3]
    ls = seq // d
    bq = min(ATT_BQ, ls)
    nqb = ls // bq
    klen = min(bq + N_DIST, ls)
    slope = slopes_ref[g, h]
    qi = lax.broadcasted_iota(jnp.int32, (bq, klen), 0)
    kj = lax.broadcasted_iota(jnp.int32, (bq, klen), 1)
    for which, off in enumerate((0, N_DIST)):
      delta = qi - kj + off
      valid = (delta >= 0) & (delta <= N_DIST)
      dist = (delta * d).astype(F32)
      bias_ref[2 * g + which, :bq, :klen] = jnp.where(valid, -slope * dist, NEG_INF)

    def block(idx, carry, g=g, d=d, bq=bq, nqb=nqb, klen=klen,
              q_ref=q_ref, k_ref=k_ref, v_ref=v_ref):
      r = idx // nqb
      qb = idx % nqb
      q0 = pl.multiple_of(qb * bq, bq)
      k0 = pl.multiple_of(jnp.maximum(qb * bq - N_DIST, 0), N_DIST)
      q = q_ref[r, pl.ds(q0, bq), :]
      k = k_ref[r, pl.ds(k0, klen), :]
      v = v_ref[r, pl.ds(k0, klen), :]
      bias = bias_ref[2 * g + jnp.minimum(qb, 1), :bq, :klen]
      s = _dot_nt(q, k) * scale + bias
      m = jnp.max(s, axis=-1, keepdims=True)
      p = jnp.exp(s - m)
      l = jnp.sum(p, axis=-1, keepdims=True)
      o = _dot(p.astype(BF16), v) / l
      lse = m + jnp.log(l)
      t0 = r + q0 * d
      if d == 1:
        oacc[g][pl.ds(q0, bq), :] = o
        lacc[g][pl.ds(q0, bq), :] = jnp.broadcast_to(lse, (bq, 128))
      else:
        oacc[g][pl.ds(t0, bq, stride=d), :] = o
        lacc[g][pl.ds(t0, bq, stride=d), :] = jnp.broadcast_to(lse, (bq, 128))
      return carry

    lax.fori_loop(0, d * nqb, block, 0, unroll=4)

  mrows = 512

  def merge(i, carry):
    r0 = pl.multiple_of(i * mrows, mrows)
    l0 = lacc[0][pl.ds(r0, mrows), :]
    l1 = lacc[1][pl.ds(r0, mrows), :]
    l2 = lacc[2][pl.ds(r0, mrows), :]
    mx = jnp.maximum(jnp.maximum(l0, l1), l2)
    e0 = jnp.exp(l0 - mx)
    e1 = jnp.exp(l1 - mx)
    e2 = jnp.exp(l2 - mx)
    den = e0 + e1 + e2
    acc = (e0 * oacc[0][pl.ds(r0, mrows), :] + e1 * oacc[1][pl.ds(r0, mrows), :]
           + e2 * oacc[2][pl.ds(r0, mrows), :])
    o_ref[pl.ds(r0, mrows), :] = acc / den
    return carry

  lax.fori_loop(0, seq // mrows, merge, 0)


def _attn_prompt(qkvs, slopes, batch, seq):
  in_specs = [pl.BlockSpec(memory_space=pltpu.SMEM)]
  args = [slopes]
  for g, (_, d) in enumerate(A_GROUPS):
    ls = seq // d
    for part in range(3):
      in_specs.append(pl.BlockSpec(
          (None, d, None, ls, 128),
          lambda b, h, part=part: (b, 0, part * A_HEADS + h, 0, 0)))
      args.append(qkvs[g])
  scratch = ([pltpu.VMEM((seq, 128), F32) for _ in range(6)]
             + [pltpu.VMEM((6, ATT_BQ, ATT_BQ + N_DIST), F32)])
  return pl.pallas_call(
      functools.partial(_attn_prompt_kernel, seq=seq),
      grid=(batch, A_HEADS),
      in_specs=in_specs,
      out_specs=pl.BlockSpec((None, seq, 128), lambda b, h: (b, 0, h)),
      out_shape=jax.ShapeDtypeStruct((batch, seq, A_OUT), F32),
      scratch_shapes=scratch,
      compiler_params=_params(("parallel", "arbitrary")),
      name="attn_prompt",
  )(*args)


def _attn_sample_kernel(slopes_ref, q_ref, c0_ref, c1_ref, c2_ref, n0_ref, n1_ref, n2_ref,
                        o_ref, u0_ref, u1_ref, u2_ref, sems, *, tq):
  b = pl.program_id(0)
  cache_refs = (c0_ref, c1_ref, c2_ref)
  new_refs = (n0_ref, n1_ref, n2_ref)
  out_refs = (u0_ref, u1_ref, u2_ref)

  copies = []
  for g, (w, _) in enumerate(A_GROUPS):
    body = pltpu.make_async_copy(cache_refs[g].at[0, pl.ds(tq, w - tq)],
                                 out_refs[g].at[b, pl.ds(0, w - tq)], sems.at[2 * g])
    tail = pltpu.make_async_copy(new_refs[g].at[0], out_refs[g].at[b, pl.ds(w - tq, tq)],
                                 sems.at[2 * g + 1])
    body.start()
    tail.start()
    copies += [body, tail]

  scale = A_HEAD_DIM ** -0.5
  nrow = N_DIST * 8 + tq * 8
  ri = lax.broadcasted_iota(jnp.int32, (nrow, 128), 0)
  cj = lax.broadcasted_iota(jnp.int32, (nrow, 128), 1)
  slot = ri % 8
  key = ri // 8
  tcol = cj // A_HEADS
  hcol = cj % A_HEADS
  col_ok = cj < tq * A_HEADS
  lse_g = []
  ot_g = []
  for g, (w, d) in enumerate(A_GROUPS):
    qg = q_ref[g]
    xn = new_refs[g][0].reshape(tq * 8, 128)
    nres = min(d, tq)
    xs = []
    s = jnp.zeros((nrow, 128), F32)
    col1 = lax.broadcasted_iota(jnp.int32, (128, 1), 0) // A_HEADS
    for rho in range(nres):
      if d == 1:
        xc = cache_refs[g][0]
      else:
        xc = cache_refs[g][0, pl.ds(rho, N_DIST, stride=d), :, :]
      xc = xc.reshape(N_DIST * 8, 128)
      x = jnp.concatenate([xc, xn], axis=0).astype(BF16)
      xs.append(x)
      qm = jnp.where(col1 % d == rho, qg, 0.0).astype(BF16)
      s = s + _dot_nt(x, qm)
    qt = tcol // d
    nidx = key - N_DIST
    in_cache = key < N_DIST
    valid_c = in_cache & (key >= qt)
    dist_c = d * (N_DIST + qt - key)
    valid_n = (~in_cache) & (nidx <= tcol) & ((tcol - nidx) % d == 0)
    dist_n = tcol - nidx
    valid = (valid_c | valid_n) & (slot == hcol) & col_ok
    dist = jnp.where(in_cache, dist_c, dist_n).astype(F32)
    slope = jnp.zeros((nrow, 128), F32)
    for hh in range(A_HEADS):
      slope = jnp.where(hcol == hh, slopes_ref[g, hh], slope)
    s = jnp.where(valid, s * scale - slope * dist, NEG_INF)
    m = jnp.max(s, axis=0, keepdims=True)
    p = jnp.where(valid, jnp.exp(s - m), 0.0)
    l = jnp.sum(p, axis=0, keepdims=True)
    l = jnp.where(l > 0.0, l, 1.0)
    pn = p / l
    pv = pltpu.roll(pn, 4, 0)
    ot = jnp.zeros((128, 128), F32)
    col_row = lax.broadcasted_iota(jnp.int32, (1, 128), 1) // A_HEADS
    for rho in range(nres):
      pm = jnp.where(col_row % d == rho, pv, 0.0).astype(BF16)
      ot = ot + _dot_tn(xs[rho], pm)
    ot_g.append(ot)
    lse_g.append(m + jnp.log(l))
  mx = jnp.maximum(jnp.maximum(lse_g[0], lse_g[1]), lse_g[2])
  es = [jnp.exp(v - mx) for v in lse_g]
  den = es[0] + es[1] + es[2]
  ot = (es[0] * ot_g[0] + es[1] * ot_g[1] + es[2] * ot_g[2]) / den
  o_ref[...] = ot.T

  for c in copies:
    c.wait()


def _attn_sample(q, caches, news, slopes, tq):
  batch = q.shape[0]
  in_specs = [pl.BlockSpec(memory_space=pltpu.SMEM),
              pl.BlockSpec((None, 3, 128, 128), lambda b: (b, 0, 0, 0))]
  args = [slopes, q]
  for g, (w, d) in enumerate(A_GROUPS):
    in_specs.append(pl.BlockSpec((1, w, 8, 128), lambda b: (b, 0, 0, 0)))
    args.append(caches[g])
  for g in range(3):
    in_specs.append(pl.BlockSpec((1, tq, 8, 128), lambda b: (b, 0, 0, 0)))
    args.append(news[g])
  out_shape = [jax.ShapeDtypeStruct((batch, 128, 128), F32)]
  out_specs = [pl.BlockSpec((None, 128, 128), lambda b: (b, 0, 0))]
  for g, (w, _) in enumerate(A_GROUPS):
    out_shape.append(jax.ShapeDtypeStruct((batch, w, 8, 128), F32))
    out_specs.append(pl.BlockSpec(memory_space=pl.ANY))
  return pl.pallas_call(
      functools.partial(_attn_sample_kernel, tq=tq),
      grid=(batch,),
      in_specs=in_specs,
      out_specs=out_specs,
      out_shape=out_shape,
      scratch_shapes=[pltpu.SemaphoreType.DMA((6,))],
      compiler_params=_params(("arbitrary",)),
      name="attn_sample",
  )(*args)


def _retention_kernel(q_ref, k_ref, v_ref, cos_ref, sin_ref, dmat_ref, qdec_ref, kdec_ref,
                      cdec_ref, s0_ref, gn_ref, o_ref, sout_ref, state, *, chunk):
  j = pl.program_id(1)

  @pl.when(j == 0)
  def _():
    state[...] = s0_ref[...]

  nchunk = q_ref.shape[0] // chunk
  kscale = R_KEY_DIM ** -0.5
  for ci in range(nchunk):
    rows = pl.ds(ci * chunk, chunk)
    cosv = cos_ref[rows, :]
    sinv = sin_ref[rows, :]
    for h in range(R_HEADS):
      qh = q_ref[rows, h * 128:(h + 1) * 128]
      kh = k_ref[rows, h * 128:(h + 1) * 128]
      qrot = qh * cosv + pltpu.roll(qh, 64, 1) * sinv
      krot = (kh * cosv + pltpu.roll(kh, 64, 1) * sinv) * kscale
      qb = qrot.astype(BF16)
      vh = v_ref[rows, h * 256:(h + 1) * 256].astype(BF16)
      att = _dot_nt(qb, krot.astype(BF16)) * dmat_ref[h]
      sh = state[h]
      o = _dot(att.astype(BF16), vh) + _dot(qb, sh.astype(BF16)) * qdec_ref[h]
      state[h] = cdec_ref[h] * sh + _dot_tn((krot * kdec_ref[h]).astype(BF16), vh)
      mu = jnp.mean(o, axis=-1, keepdims=True)
      var = jnp.mean(jnp.square(o - mu), axis=-1, keepdims=True)
      on = (o - mu) * lax.rsqrt(var + EPS) * gn_ref[:, h * 256:(h + 1) * 256]
      o_ref[rows, h * 256:(h + 1) * 256] = on

  @pl.when(j == pl.num_programs(1) - 1)
  def _():
    sout_ref[...] = state[...]


def _retention(qkv, cos, sin, tables, s0, gn, rows_per_step, chunk):
  batch, t, _ = qkv.shape
  dmat, qdec, kdec, cdec = tables
  tc = rows_per_step
  const3 = lambda b, j: (0, 0, 0)
  return pl.pallas_call(
      functools.partial(_retention_kernel, chunk=chunk),
      grid=(batch, t // tc),
      in_specs=[pl.BlockSpec((None, tc, R_QK), lambda b, j: (b, j, 0)),
                pl.BlockSpec((None, tc, R_QK), lambda b, j: (b, j, 1)),
                pl.BlockSpec((None, tc, R_V), lambda b, j: (b, j, 1)),
                pl.BlockSpec((tc, 128), lambda b, j: (j, 0)),
                pl.BlockSpec((tc, 128), lambda b, j: (j, 0)),
                pl.BlockSpec(dmat.shape, const3),
                pl.BlockSpec(qdec.shape, const3),
                pl.BlockSpec(kdec.shape, const3),
                pl.BlockSpec(cdec.shape, const3),
                pl.BlockSpec((None, R_HEADS, R_KEY_DIM, R_VAL_DIM), lambda b, j: (b, 0, 0, 0)),
                pl.BlockSpec((1, R_V), lambda b, j: (0, 0))],
      out_specs=[pl.BlockSpec((None, tc, R_V), lambda b, j: (b, j, 0)),
                 pl.BlockSpec((None, R_HEADS, R_KEY_DIM, R_VAL_DIM), lambda b, j: (b, 0, 0, 0))],
      out_shape=[jax.ShapeDtypeStruct((batch, t, R_V), F32),
                 jax.ShapeDtypeStruct((batch, R_HEADS, R_KEY_DIM, R_VAL_DIM), F32)],
      scratch_shapes=[pltpu.VMEM((R_HEADS, R_KEY_DIM, R_VAL_DIM), F32)],
      compiler_params=_params(("parallel", "arbitrary")),
      name="retention",
  )(qkv, qkv, qkv, cos, sin, dmat, qdec, kdec, cdec, s0, gn)


def _retention_tables(c_true, c_pad):
  lg = jnp.log1p(-jnp.exp2(-5.0 - jnp.arange(R_HEADS, dtype=F32)))
  i = jnp.arange(c_pad, dtype=F32)
  live = i < c_true
  diff = i[:, None] - i[None, :]
  causal = (diff >= 0) & live[:, None] & live[None, :]
  dmat = jnp.where(causal[None], jnp.exp(jnp.where(causal, diff, 0.0)[None] * lg[:, None, None]), 0.0)
  qdec = jnp.where(live[None, :], jnp.exp((i[None, :] + 1.0) * lg[:, None]), 0.0)
  kdec = jnp.where(live[None, :], jnp.exp((c_true - 1.0 - i)[None, :] * lg[:, None]), 0.0)
  cdec = jnp.exp(c_true * lg)
  qdec = jnp.broadcast_to(qdec[:, :, None], (R_HEADS, c_pad, R_VAL_DIM))
  kdec = jnp.broadcast_to(kdec[:, :, None], (R_HEADS, c_pad, R_KEY_DIM))
  cdec = jnp.broadcast_to(cdec[:, None, None], (R_HEADS, R_KEY_DIM, R_VAL_DIM))
  return dmat, qdec, kdec, cdec


def _rope_tables(pos):
  half = R_KEY_DIM // 2
  inv = 1.0 / (ROPE_BASE ** jnp.linspace(0.0, 1.0, half, dtype=F32))
  ang = pos.astype(F32)[:, None] * inv[None, :]
  cos = jnp.cos(ang)
  sin = jnp.sin(ang)
  return jnp.concatenate([cos, cos], axis=-1), jnp.concatenate([-sin, sin], axis=-1)


def _output_kernel(x_ref, oa_ref, or_ref, wza_ref, wzr_ref, wga_ref, wgb_ref, wpa_ref, wpb_ref,
                   wo_ref, lng_ref, lnb_ref, y_ref):
  x = x_ref[...]
  xb = x.astype(BF16)
  za = _dot(xb, wza_ref[...])
  ya = _dot((jax.nn.silu(za) * oa_ref[...]).astype(BF16), wpa_ref[...])
  zr = _dot(xb, wzr_ref[...])
  yb = _dot((jax.nn.silu(zr) * or_ref[...]).astype(BF16), wpb_ref[...])
  ga = _dot(xb, wga_ref[...])
  gb = _dot(xb, wgb_ref[...])
  mix = jax.nn.sigmoid(ga) * ya + jax.nn.sigmoid(gb) * yb
  out = _dot(mix.astype(BF16), wo_ref[...])
  zz = DEEPNORM_ALPHA * x + out
  zm = jnp.mean(zz, axis=-1, keepdims=True)
  zv = jnp.mean(jnp.square(zz - zm), axis=-1, keepdims=True)
  y_ref[...] = (zz - zm) * lax.rsqrt(zv + EPS) * lng_ref[...] + lnb_ref[...]


def _output(x, oa, orr, w16, wpa, wpb, wo, lng, lnb, tm):
  m = x.shape[0]
  row = lambda w: pl.BlockSpec((tm, w), lambda i: (i, 0))
  full = lambda a: pl.BlockSpec(a.shape, lambda i: (0, 0), pipeline_mode=pl.Buffered(1))
  wcol = lambda off, width: pl.BlockSpec((D_MODEL, width), lambda i: (0, off // width),
                                         pipeline_mode=pl.Buffered(1))
  return pl.pallas_call(
      _output_kernel,
      grid=(m // tm,),
      in_specs=[row(D_MODEL), row(A_OUT), row(R_V),
                wcol(_OFF[3], A_OUT), wcol(_OFF[7], R_V), wcol(_OFF[8], D_MODEL),
                wcol(_OFF[9], D_MODEL),
                full(wpa), full(wpb), full(wo), full(lng), full(lnb)],
      out_specs=row(D_MODEL),
      out_shape=jax.ShapeDtypeStruct((m, D_MODEL), F32),
      compiler_params=_params(("parallel",)),
      name="gates_output",
  )(x, oa, orr, w16, w16, w16, w16, wpa, wpb, wo, lng, lnb)


def _alibi_slopes():
  n = len(A_GROUPS) * A_HEADS
  return jnp.exp2(-8.0 * jnp.arange(1, n + 1, dtype=F32) / n).reshape(len(A_GROUPS), A_HEADS)


def kernel(x_prompt, x_sample, cache_kv_w128, cache_kv_w512, cache_kv_w2048, state_ret,
           w_in, w_pa, w_pb, w_o, gn_g, ln_g, ln_b):
  bp, tp, _ = x_prompt.shape
  bs, ts, _ = x_sample.shape
  slopes = _alibi_slopes()
  w16 = w_in.astype(BF16)
  n_ret = 2 * R_QK + R_V
  wpa = w_pa.astype(BF16)
  wpb = w_pb.astype(BF16)
  wo = w_o.astype(BF16)
  gn = gn_g.reshape(1, R_V)
  lng = ln_g.reshape(1, D_MODEL)
  lnb = ln_b.reshape(1, D_MODEL)

  xp2 = x_prompt.reshape(bp * tp, D_MODEL)
  qkvs, kv_p = [], []
  for g, (w, _) in enumerate(A_GROUPS):
    qkv, tail = _proj_group(xp2, w16, g, bp, tp)
    qkvs.append(qkv)
    kv_p.append(tail.reshape(bp, min(w, tp), 2, A_HEADS, A_HEAD_DIM))
  oa_p = _attn_prompt(qkvs, slopes, bp, tp).reshape(bp * tp, A_OUT)

  ret_p = _matmul(xp2, w16, _OFF[4], n_ret, F32, 1024, 1024)
  cos_p, sin_p = _rope_tables(jnp.arange(tp, dtype=jnp.int32))
  or_p, s_p = _retention(
      ret_p.reshape(bp, tp, n_ret), cos_p, sin_p, _retention_tables(R_CHUNK, R_CHUNK),
      jnp.zeros((bp, R_HEADS, R_KEY_DIM, R_VAL_DIM), F32), gn, 512, R_CHUNK)
  y_p = _output(xp2, oa_p, or_p.reshape(bp * tp, R_V), w16, wpa, wpb, wo, lng, lnb, 512)

  xs2 = x_sample.reshape(bs * ts, D_MODEL)
  hs = _matmul(xs2, w16, 0, _OFF[7], F32, bs * ts, A_GW)
  caches = (cache_kv_w128, cache_kv_w512, cache_kv_w2048)
  q_s, news = [], []
  for g in range(3):
    col = lambda part: hs[:, _OFF[part] + g * A_GW:_OFF[part] + (g + 1) * A_GW]
    qg = col(0).reshape(bs, ts * A_HEADS, A_HEAD_DIM)
    q_s.append(jnp.pad(qg, ((0, 0), (0, 128 - ts * A_HEADS), (0, 0))))
    news.append(jnp.concatenate([col(1), col(2)], axis=1).reshape(bs, ts, 8, 128))
  q_s = jnp.stack(q_s, axis=1)
  caches8 = [c.reshape(bs, c.shape[1], 8, 128) for c in caches]
  oa_s, u0, u1, u2 = _attn_sample(q_s, caches8, news, slopes, ts)
  oa_s = oa_s[:, :ts * A_HEADS].reshape(bs * ts, A_OUT)
  kv_s = [u.reshape(bs, u.shape[1], 2, A_HEADS, A_HEAD_DIM) for u in (u0, u1, u2)]

  cpad = 16
  ret_s = jnp.pad(hs[:, _OFF[4]:].reshape(bs, ts, n_ret), ((0, 0), (0, cpad - ts), (0, 0)))
  cos_s, sin_s = _rope_tables(PAST_LEN + jnp.arange(ts, dtype=jnp.int32))
  padtab = lambda a: jnp.pad(a, ((0, cpad - ts), (0, 0)))
  or_s, s_s = _retention(
      ret_s, padtab(cos_s), padtab(sin_s),
      _retention_tables(ts, cpad), state_ret.astype(F32), gn, cpad, cpad)
  or_s = or_s[:, :ts].reshape(bs * ts, R_V)
  y_s = _output(xs2, oa_s, or_s, w16, wpa, wpb, wo, lng, lnb, 256)

  return (y_p.reshape(bp, tp, D_MODEL), y_s.reshape(bs, ts, D_MODEL),
          kv_p[0], kv_p[1], kv_p[2], s_p,
          kv_s[0], kv_s[1], kv_s[2], s_s)
```

```python
import functools

import numpy as np
import jax
import jax.numpy as jnp
from jax import lax
from jax.experimental import pallas as pl
from jax.experimental.pallas import tpu as pltpu

D_MODEL = 1024
PAST_LEN = 16384
A_GROUPS = ((128, 1), (512, 4), (2048, 16))
A_HEADS = 4
A_HEAD_DIM = 128
A_GW = A_HEADS * A_HEAD_DIM
A_QKV = len(A_GROUPS) * A_GW
A_OUT = A_GW
N_DIST = 128
R_HEADS = 4
R_KEY_DIM = 128
R_VAL_DIM = 256
R_QK = R_HEADS * R_KEY_DIM
R_V = R_HEADS * R_VAL_DIM
R_CHUNK = 128
ROPE_BASE = 10000.0
EPS = 1e-5
NEG_INF = -1e30
DEEPNORM_ALPHA = 2.0 ** 0.25
_SPLIT = (A_QKV, A_QKV, A_QKV, A_OUT, R_QK, R_QK, R_V, R_V, D_MODEL, D_MODEL)
_OFF = tuple(int(v) for v in np.concatenate([[0], np.cumsum(_SPLIT)]))

BF16 = jnp.bfloat16
F32 = jnp.float32
VMEM_LIMIT = 56 * 1024 * 1024

_NT = (((1,), (1,)), ((), ()))
_TN = (((0,), (0,)), ((), ()))


def _dot(a, b):
  return jnp.dot(a, b, preferred_element_type=F32)


def _dot_nt(a, b):
  return lax.dot_general(a, b, _NT, preferred_element_type=F32)


def _dot_tn(a, b):
  return lax.dot_general(a, b, _TN, preferred_element_type=F32)


def _params(sem):
  return pltpu.CompilerParams(dimension_semantics=sem, vmem_limit_bytes=VMEM_LIMIT)


def _mm_kernel(x_ref, w_ref, o_ref):
  o_ref[...] = _dot(x_ref[...].astype(BF16), w_ref[...]).astype(o_ref.dtype)


def _matmul(x, w, col0, ncols, out_dtype, tm, tn):
  m, k = x.shape
  c0 = col0 // tn
  return pl.pallas_call(
      _mm_kernel,
      grid=(m // tm, ncols // tn),
      in_specs=[pl.BlockSpec((tm, k), lambda i, j: (i, 0)),
                pl.BlockSpec((k, tn), lambda i, j: (0, c0 + j))],
      out_specs=pl.BlockSpec((tm, tn), lambda i, j: (i, j)),
      out_shape=jax.ShapeDtypeStruct((m, ncols), out_dtype),
      compiler_params=_params(("parallel", "arbitrary")),
      name="proj_matmul",
  )(x, w)


PROJ_TM = 512


def _proj_group_kernel(x_ref, wq_ref, wk_ref, wv_ref, o_ref, tail_ref, *scratch,
                       d, first_tail, tail_rows):
  i = pl.program_id(1)
  tm = x_ref.shape[0]
  n = tm // d
  x = x_ref[...]
  xb = x.astype(BF16)
  if d == 1:
    perm = xb
  else:
    xs_ref, xp_ref = scratch
    for c in range(D_MODEL // 128):
      xs_ref[c] = x[:, c * 128:(c + 1) * 128]
    for r in range(d):
      for c in range(D_MODEL // 128):
        xp_ref[r * n:(r + 1) * n, c * 128:(c + 1) * 128] = (
            xs_ref[c, pl.ds(r, n, stride=d), :].astype(BF16))
    perm = xp_ref[...]
  nat = []
  for part, w_ref in enumerate((wq_ref, wk_ref, wv_ref)):
    res = _dot(perm, w_ref[...])
    nat.append(res)
    for r in range(d):
      for h in range(A_HEADS):
        o_ref[r, part * A_HEADS + h] = res[r * n:(r + 1) * n, h * 128:(h + 1) * 128].astype(BF16)

  @pl.when(i >= first_tail)
  def _():
    if d == 1:
      kv = nat[1:]
    else:
      kv = [_dot(xb, wk_ref[...]), _dot(xb, wv_ref[...])]
    for c in range(2 * A_HEADS):
      src = kv[c // A_HEADS][tm - tail_rows:, (c % A_HEADS) * 128:(c % A_HEADS + 1) * 128]
      tail_ref[pl.ds(c, tail_rows, stride=2 * A_HEADS), :] = src


def _proj_group(x2, w16, g, batch, seq):
  w, d = A_GROUPS[g]
  w = min(w, seq)
  tm = PROJ_TM
  nblk = seq // tm
  tail_rows = min(tm, w)
  first_tail = nblk - max(w // tm, 1)
  ls = seq // d
  wspec = lambda part: pl.BlockSpec((D_MODEL, A_GW), lambda b, i: (0, 3 * part + g))
  scratch = []
  if d > 1:
    scratch = [pltpu.VMEM((D_MODEL // 128, tm, 128), F32), pltpu.VMEM((tm, D_MODEL), BF16)]
  return pl.pallas_call(
      functools.partial(_proj_group_kernel, d=d, first_tail=first_tail, tail_rows=tail_rows),
      grid=(batch, nblk),
      in_specs=[pl.BlockSpec((tm, D_MODEL), lambda b, i: (b * nblk + i, 0)),
                wspec(0), wspec(1), wspec(2)],
      out_specs=[pl.BlockSpec((None, d, 3 * A_HEADS, tm // d, 128), lambda b, i: (b, 0, 0, i, 0)),
                 pl.BlockSpec((None, tail_rows * 8, 128),
                              lambda b, i: (b, jnp.maximum(i - first_tail, 0), 0))],
      out_shape=[jax.ShapeDtypeStruct((batch, d, 3 * A_HEADS, ls, 128), BF16),
                 jax.ShapeDtypeStruct((batch, w * 8, 128), F32)],
      scratch_shapes=scratch,
      compiler_params=_params(("parallel", "arbitrary")),
      name="proj_group",
  )(x2, w16, w16, w16)


ATT_BQ = 256


def _attn_prompt_kernel(slopes_ref, *refs, seq):
  qkv = refs[:9]
  o_ref = refs[9]
  oacc = refs[10:13]
  lacc = refs[13:16]
  bias_ref, s_buf, p_buf, l_buf = refs[16:20]
  h = pl.program_id(0)
  scale = A_HEAD_DIM ** -0.5

  @pl.when(pl.program_id(1) == 0)
  def _():
    for g, (_, d) in enumerate(A_GROUPS):
      ls = seq // d
      bq = min(ATT_BQ, ls)
      klen = min(bq + N_DIST, ls)
      slope = slopes_ref[g, h]
      qi = lax.broadcasted_iota(jnp.int32, (bq, klen), 0)
      kj = lax.broadcasted_iota(jnp.int32, (bq, klen), 1)
      for which, off in enumerate((0, N_DIST)):
        delta = qi - kj + off
        valid = (delta >= 0) & (delta <= N_DIST)
        dist = (delta * d).astype(F32)
        bias_ref[2 * g + which, :bq, :klen] = jnp.where(valid, -slope * dist, NEG_INF)

  for g, (_, d) in enumerate(A_GROUPS):
    q_ref, k_ref, v_ref = qkv[3 * g:3 * g + 3]
    ls = seq // d
    bq = min(ATT_BQ, ls)
    nqb = ls // bq
    klen = min(bq + N_DIST, ls)

    def coords(idx, nqb=nqb, bq=bq):
      r = idx // nqb
      qb = idx % nqb
      q0 = pl.multiple_of(qb * bq, bq)
      k0 = pl.multiple_of(jnp.maximum(qb * bq - N_DIST, 0), N_DIST)
      return r, qb, q0, k0

    def put_rows(ref, idx, val, d=d, bq=bq):
      r, _, q0, _ = coords(idx)
      if d == 1:
        ref[pl.ds(q0, bq), :] = val
      else:
        ref[pl.ds(r + q0 * d, bq, stride=d), :] = val

    def scores(idx, slot, bq=bq, klen=klen, q_ref=q_ref, k_ref=k_ref):
      r, _, q0, k0 = coords(idx)
      s_buf[slot, :bq, :klen] = _dot_nt(q_ref[r, pl.ds(q0, bq), :], k_ref[r, pl.ds(k0, klen), :])

    def softmax(idx, slot, g=g, bq=bq, klen=klen):
      _, qb, _, _ = coords(idx)
      s = s_buf[slot, :bq, :klen] * scale + bias_ref[2 * g + jnp.minimum(qb, 1), :bq, :klen]
      m = jnp.max(s, axis=-1, keepdims=True)
      p = jnp.exp(s - m)
      l = jnp.sum(p, axis=-1, keepdims=True)
      p_buf[slot, :bq, :klen] = p.astype(BF16)
      l_buf[slot, :bq, :] = jnp.broadcast_to(l, (bq, 128))
      put_rows(lacc[g], idx, jnp.broadcast_to(m + jnp.log(l), (bq, 128)))

    def values(idx, slot, g=g, bq=bq, klen=klen, v_ref=v_ref):
      r, _, _, k0 = coords(idx)
      o = _dot(p_buf[slot, :bq, :klen], v_ref[r, pl.ds(k0, klen), :]) / l_buf[slot, :bq, :]
      put_rows(oacc[g], idx, o)

    nblk = d * nqb
    scores(0, 0)
    scores(1, 1)
    softmax(0, 0)

    def steady(j, carry):
      i = 2 + 2 * j
      values(i - 2, 0)
      softmax(i - 1, 1)
      scores(i, 0)
      values(i - 1, 1)
      softmax(i, 0)
      scores(i + 1, 1)
      return carry

    lax.fori_loop(0, (nblk - 2) // 2, steady, 0)
    values(nblk - 2, 0)
    softmax(nblk - 1, 1)
    values(nblk - 1, 1)

  mrows = 512

  def merge(i, carry):
    r0 = pl.multiple_of(i * mrows, mrows)
    l0 = lacc[0][pl.ds(r0, mrows), :]
    l1 = lacc[1][pl.ds(r0, mrows), :]
    l2 = lacc[2][pl.ds(r0, mrows), :]
    mx = jnp.maximum(jnp.maximum(l0, l1), l2)
    e0 = jnp.exp(l0 - mx)
    e1 = jnp.exp(l1 - mx)
    e2 = jnp.exp(l2 - mx)
    den = e0 + e1 + e2
    acc = (e0 * oacc[0][pl.ds(r0, mrows), :] + e1 * oacc[1][pl.ds(r0, mrows), :]
           + e2 * oacc[2][pl.ds(r0, mrows), :])
    o_ref[pl.ds(r0, mrows), :] = acc / den
    return carry

  lax.fori_loop(0, seq // mrows, merge, 0)


def _attn_prompt(qkvs, slopes, batch, seq):
  in_specs = [pl.BlockSpec(memory_space=pltpu.SMEM)]
  args = [slopes]
  for g, (_, d) in enumerate(A_GROUPS):
    ls = seq // d
    for part in range(3):
      in_specs.append(pl.BlockSpec(
          (None, d, None, ls, 128),
          lambda h, b, part=part: (b, 0, part * A_HEADS + h, 0, 0)))
      args.append(qkvs[g])
  scratch = ([pltpu.VMEM((seq, 128), F32) for _ in range(6)]
             + [pltpu.VMEM((6, ATT_BQ, ATT_BQ + N_DIST), F32),
                pltpu.VMEM((2, ATT_BQ, ATT_BQ + N_DIST), F32),
                pltpu.VMEM((2, ATT_BQ, ATT_BQ + N_DIST), BF16),
                pltpu.VMEM((2, ATT_BQ, 128), F32)])
  return pl.pallas_call(
      functools.partial(_attn_prompt_kernel, seq=seq),
      grid=(A_HEADS, batch),
      in_specs=in_specs,
      out_specs=pl.BlockSpec((None, seq, 128), lambda h, b: (b, 0, h)),
      out_shape=jax.ShapeDtypeStruct((batch, seq, A_OUT), F32),
      scratch_shapes=scratch,
      compiler_params=_params(("parallel", "arbitrary")),
      name="attn_prompt",
  )(*args)


def _attn_sample_kernel(slopes_ref, q_ref, c0_ref, c1_ref, c2_ref, n0_ref, n1_ref, n2_ref,
                        o_ref, u0_ref, u1_ref, u2_ref, sems, *, tq):
  b = pl.program_id(0)
  cache_refs = (c0_ref, c1_ref, c2_ref)
  new_refs = (n0_ref, n1_ref, n2_ref)
  out_refs = (u0_ref, u1_ref, u2_ref)

  copies = []
  for g, (w, _) in enumerate(A_GROUPS):
    body = pltpu.make_async_copy(cache_refs[g].at[0, pl.ds(tq, w - tq)],
                                 out_refs[g].at[b, pl.ds(0, w - tq)], sems.at[2 * g])
    tail = pltpu.make_async_copy(new_refs[g].at[0], out_refs[g].at[b, pl.ds(w - tq, tq)],
                                 sems.at[2 * g + 1])
    body.start()
    tail.start()
    copies += [body, tail]

  scale = A_HEAD_DIM ** -0.5
  nrow = N_DIST * 8 + tq * 8
  ri = lax.broadcasted_iota(jnp.int32, (nrow, 128), 0)
  cj = lax.broadcasted_iota(jnp.int32, (nrow, 128), 1)
  slot = ri % 8
  key = ri // 8
  tcol = cj // A_HEADS
  hcol = cj % A_HEADS
  col_ok = cj < tq * A_HEADS
  lse_g = []
  ot_g = []
  for g, (w, d) in enumerate(A_GROUPS):
    qg = q_ref[g]
    xn = new_refs[g][0].reshape(tq * 8, 128)
    nres = min(d, tq)
    xs = []
    s = jnp.zeros((nrow, 128), F32)
    col1 = lax.broadcasted_iota(jnp.int32, (128, 1), 0) // A_HEADS
    for rho in range(nres):
      if d == 1:
        xc = cache_refs[g][0]
      else:
        xc = cache_refs[g][0, pl.ds(rho, N_DIST, stride=d), :, :]
      xc = xc.reshape(N_DIST * 8, 128)
      x = jnp.concatenate([xc, xn], axis=0).astype(BF16)
      xs.append(x)
      qm = jnp.where(col1 % d == rho, qg, 0.0).astype(BF16)
      s = s + _dot_nt(x, qm)
    qt = tcol // d
    nidx = key - N_DIST
    in_cache = key < N_DIST
    valid_c = in_cache & (key >= qt)
    dist_c = d * (N_DIST + qt - key)
    valid_n = (~in_cache) & (nidx <= tcol) & ((tcol - nidx) % d == 0)
    dist_n = tcol - nidx
    valid = (valid_c | valid_n) & (slot == hcol) & col_ok
    dist = jnp.where(in_cache, dist_c, dist_n).astype(F32)
    slope = jnp.zeros((nrow, 128), F32)
    for hh in range(A_HEADS):
      slope = jnp.where(hcol == hh, slopes_ref[g, hh], slope)
    s = jnp.where(valid, s * scale - slope * dist, NEG_INF)
    m = jnp.max(s, axis=0, keepdims=True)
    p = jnp.where(valid, jnp.exp(s - m), 0.0)
    l = jnp.sum(p, axis=0, keepdims=True)
    l = jnp.where(l > 0.0, l, 1.0)
    pn = p / l
    pv = pltpu.roll(pn, 4, 0)
    ot = jnp.zeros((128, 128), F32)
    col_row = lax.broadcasted_iota(jnp.int32, (1, 128), 1) // A_HEADS
    for rho in range(nres):
      pm = jnp.where(col_row % d == rho, pv, 0.0).astype(BF16)
      ot = ot + _dot_tn(xs[rho], pm)
    ot_g.append(ot)
    lse_g.append(m + jnp.log(l))
  mx = jnp.maximum(jnp.maximum(lse_g[0], lse_g[1]), lse_g[2])
  es = [jnp.exp(v - mx) for v in lse_g]
  den = es[0] + es[1] + es[2]
  ot = (es[0] * ot_g[0] + es[1] * ot_g[1] + es[2] * ot_g[2]) / den
  o_ref[...] = ot.T

  for c in copies:
    c.wait()


def _attn_sample(q, caches, news, slopes, tq):
  batch = q.shape[0]
  in_specs = [pl.BlockSpec(memory_space=pltpu.SMEM),
              pl.BlockSpec((None, 3, 128, 128), lambda b: (b, 0, 0, 0))]
  args = [slopes, q]
  for g, (w, d) in enumerate(A_GROUPS):
    in_specs.append(pl.BlockSpec((1, w, 8, 128), lambda b: (b, 0, 0, 0)))
    args.append(caches[g])
  for g in range(3):
    in_specs.append(pl.BlockSpec((1, tq, 8, 128), lambda b: (b, 0, 0, 0)))
    args.append(news[g])
  out_shape = [jax.ShapeDtypeStruct((batch, 128, 128), F32)]
  out_specs = [pl.BlockSpec((None, 128, 128), lambda b: (b, 0, 0))]
  for g, (w, _) in enumerate(A_GROUPS):
    out_shape.append(jax.ShapeDtypeStruct((batch, w, 8, 128), F32))
    out_specs.append(pl.BlockSpec(memory_space=pl.ANY))
  return pl.pallas_call(
      functools.partial(_attn_sample_kernel, tq=tq),
      grid=(batch,),
      in_specs=in_specs,
      out_specs=out_specs,
      out_shape=out_shape,
      scratch_shapes=[pltpu.SemaphoreType.DMA((6,))],
      compiler_params=_params(("arbitrary",)),
      name="attn_sample",
  )(*args)


def _retention_kernel(x_ref, wqk_ref, wv_ref, cos_ref, sin_ref, dmat_ref, qdec_ref, kdec_ref,
                      cdec_ref, s0_ref, gn_ref, o_ref, sout_ref, state, qk_ref, v_ref, *, chunk):
  j = pl.program_id(1)

  @pl.when(j == 0)
  def _():
    state[...] = s0_ref[...]

  xb = x_ref[...].astype(BF16)
  qk_ref[...] = _dot(xb, wqk_ref[...])
  v_ref[...] = _dot(xb, wv_ref[...]).astype(BF16)

  nchunk = x_ref.shape[0] // chunk
  kscale = R_KEY_DIM ** -0.5
  for ci in range(nchunk):
    rows = pl.ds(ci * chunk, chunk)
    cosv = cos_ref[rows, :]
    sinv = sin_ref[rows, :]
    for h in range(R_HEADS):
      qh = qk_ref[rows, h * 128:(h + 1) * 128]
      kh = qk_ref[rows, R_QK + h * 128:R_QK + (h + 1) * 128]
      qrot = qh * cosv + pltpu.roll(qh, 64, 1) * sinv
      krot = (kh * cosv + pltpu.roll(kh, 64, 1) * sinv) * kscale
      qb = qrot.astype(BF16)
      vh = v_ref[rows, h * 256:(h + 1) * 256]
      att = _dot_nt(qb, krot.astype(BF16)) * dmat_ref[h]
      sh = state[h]
      o = _dot(att.astype(BF16), vh) + _dot(qb, sh.astype(BF16)) * qdec_ref[h]
      state[h] = cdec_ref[h] * sh + _dot_tn((krot * kdec_ref[h]).astype(BF16), vh)
      mu = jnp.mean(o, axis=-1, keepdims=True)
      var = jnp.mean(jnp.square(o - mu), axis=-1, keepdims=True)
      on = (o - mu) * lax.rsqrt(var + EPS) * gn_ref[:, h * 256:(h + 1) * 256]
      o_ref[rows, h * 256:(h + 1) * 256] = on

  @pl.when(j == pl.num_programs(1) - 1)
  def _():
    sout_ref[...] = state[...]


def _retention(x, w16, cos, sin, tables, s0, gn, rows_per_step, chunk):
  batch, t, _ = x.shape
  dmat, qdec, kdec, cdec = tables
  tc = rows_per_step
  const3 = lambda b, j: (0, 0, 0)
  wslab = lambda off: pl.BlockSpec((D_MODEL, R_V), lambda b, j: (0, off // R_V),
                                   pipeline_mode=pl.Buffered(1))
  return pl.pallas_call(
      functools.partial(_retention_kernel, chunk=chunk),
      grid=(batch, t // tc),
      in_specs=[pl.BlockSpec((None, tc, D_MODEL), lambda b, j: (b, j, 0)),
                wslab(_OFF[4]), wslab(_OFF[6]),
                pl.BlockSpec((tc, 128), lambda b, j: (j, 0)),
                pl.BlockSpec((tc, 128), lambda b, j: (j, 0)),
                pl.BlockSpec(dmat.shape, const3),
                pl.BlockSpec(qdec.shape, const3),
                pl.BlockSpec(kdec.shape, const3),
                pl.BlockSpec(cdec.shape, const3),
                pl.BlockSpec((None, R_HEADS, R_KEY_DIM, R_VAL_DIM), lambda b, j: (b, 0, 0, 0)),
                pl.BlockSpec((1, R_V), lambda b, j: (0, 0))],
      out_specs=[pl.BlockSpec((None, tc, R_V), lambda b, j: (b, j, 0)),
                 pl.BlockSpec((None, R_HEADS, R_KEY_DIM, R_VAL_DIM), lambda b, j: (b, 0, 0, 0))],
      out_shape=[jax.ShapeDtypeStruct((batch, t, R_V), F32),
                 jax.ShapeDtypeStruct((batch, R_HEADS, R_KEY_DIM, R_VAL_DIM), F32)],
      scratch_shapes=[pltpu.VMEM((R_HEADS, R_KEY_DIM, R_VAL_DIM), F32),
                      pltpu.VMEM((tc, 2 * R_QK), F32), pltpu.VMEM((tc, R_V), BF16)],
      compiler_params=_params(("parallel", "arbitrary")),
      name="retention",
  )(x, w16, w16, cos, sin, dmat, qdec, kdec, cdec, s0, gn)


def _retention_tables(c_true, c_pad):
  lg = jnp.log1p(-jnp.exp2(-5.0 - jnp.arange(R_HEADS, dtype=F32)))
  i = jnp.arange(c_pad, dtype=F32)
  live = i < c_true
  diff = i[:, None] - i[None, :]
  causal = (diff >= 0) & live[:, None] & live[None, :]
  dmat = jnp.where(causal[None], jnp.exp(jnp.where(causal, diff, 0.0)[None] * lg[:, None, None]), 0.0)
  qdec = jnp.where(live[None, :], jnp.exp((i[None, :] + 1.0) * lg[:, None]), 0.0)
  kdec = jnp.where(live[None, :], jnp.exp((c_true - 1.0 - i)[None, :] * lg[:, None]), 0.0)
  cdec = jnp.exp(c_true * lg)
  qdec = jnp.broadcast_to(qdec[:, :, None], (R_HEADS, c_pad, R_VAL_DIM))
  kdec = jnp.broadcast_to(kdec[:, :, None], (R_HEADS, c_pad, R_KEY_DIM))
  cdec = jnp.broadcast_to(cdec[:, None, None], (R_HEADS, R_KEY_DIM, R_VAL_DIM))
  return dmat, qdec, kdec, cdec


def _rope_tables(pos):
  half = R_KEY_DIM // 2
  inv = 1.0 / (ROPE_BASE ** jnp.linspace(0.0, 1.0, half, dtype=F32))
  ang = pos.astype(F32)[:, None] * inv[None, :]
  cos = jnp.cos(ang)
  sin = jnp.sin(ang)
  return jnp.concatenate([cos, cos], axis=-1), jnp.concatenate([-sin, sin], axis=-1)


def _output_kernel(x_ref, oa_ref, or_ref, wza_ref, wzr_ref, wga_ref, wgb_ref, wpa_ref, wpb_ref,
                   wo_ref, lng_ref, lnb_ref, y_ref):
  x = x_ref[...]
  xb = x.astype(BF16)
  za = _dot(xb, wza_ref[...])
  ya = _dot((jax.nn.silu(za) * oa_ref[...]).astype(BF16), wpa_ref[...])
  zr = _dot(xb, wzr_ref[...])
  yb = _dot((jax.nn.silu(zr) * or_ref[...]).astype(BF16), wpb_ref[...])
  ga = _dot(xb, wga_ref[...])
  gb = _dot(xb, wgb_ref[...])
  mix = jax.nn.sigmoid(ga) * ya + jax.nn.sigmoid(gb) * yb
  out = _dot(mix.astype(BF16), wo_ref[...])
  zz = DEEPNORM_ALPHA * x + out
  zm = jnp.mean(zz, axis=-1, keepdims=True)
  zv = jnp.mean(jnp.square(zz - zm), axis=-1, keepdims=True)
  y_ref[...] = (zz - zm) * lax.rsqrt(zv + EPS) * lng_ref[...] + lnb_ref[...]


def _output(x, oa, orr, w16, wpa, wpb, wo, lng, lnb, tm):
  m = x.shape[0]
  row = lambda w: pl.BlockSpec((tm, w), lambda i: (i, 0))
  full = lambda a: pl.BlockSpec(a.shape, lambda i: (0, 0), pipeline_mode=pl.Buffered(1))
  wcol = lambda off, width: pl.BlockSpec((D_MODEL, width), lambda i: (0, off // width),
                                         pipeline_mode=pl.Buffered(1))
  return pl.pallas_call(
      _output_kernel,
      grid=(m // tm,),
      in_specs=[row(D_MODEL), row(A_OUT), row(R_V),
                wcol(_OFF[3], A_OUT), wcol(_OFF[7], R_V), wcol(_OFF[8], D_MODEL),
                wcol(_OFF[9], D_MODEL),
                full(wpa), full(wpb), full(wo), full(lng), full(lnb)],
      out_specs=row(D_MODEL),
      out_shape=jax.ShapeDtypeStruct((m, D_MODEL), F32),
      compiler_params=_params(("parallel",)),
      name="gates_output",
  )(x, oa, orr, w16, w16, w16, w16, wpa, wpb, wo, lng, lnb)


def _alibi_slopes():
  n = len(A_GROUPS) * A_HEADS
  return jnp.exp2(-8.0 * jnp.arange(1, n + 1, dtype=F32) / n).reshape(len(A_GROUPS), A_HEADS)


def kernel(x_prompt, x_sample, cache_kv_w128, cache_kv_w512, cache_kv_w2048, state_ret,
           w_in, w_pa, w_pb, w_o, gn_g, ln_g, ln_b):
  bp, tp, _ = x_prompt.shape
  bs, ts, _ = x_sample.shape
  slopes = _alibi_slopes()
  w16 = w_in.astype(BF16)
  wpa = w_pa.astype(BF16)
  wpb = w_pb.astype(BF16)
  wo = w_o.astype(BF16)
  gn = gn_g.reshape(1, R_V)
  lng = ln_g.reshape(1, D_MODEL)
  lnb = ln_b.reshape(1, D_MODEL)

  xp2 = x_prompt.reshape(bp * tp, D_MODEL)
  qkvs, kv_p = [], []
  for g, (w, _) in enumerate(A_GROUPS):
    qkv, tail = _proj_group(xp2, w16, g, bp, tp)
    qkvs.append(qkv)
    kv_p.append(tail.reshape(bp, min(w, tp), 2, A_HEADS, A_HEAD_DIM))
  oa_p = _attn_prompt(qkvs, slopes, bp, tp).reshape(bp * tp, A_OUT)

  cos_p, sin_p = _rope_tables(jnp.arange(tp, dtype=jnp.int32))
  or_p, s_p = _retention(
      x_prompt, w16, cos_p, sin_p, _retention_tables(R_CHUNK, R_CHUNK),
      jnp.zeros((bp, R_HEADS, R_KEY_DIM, R_VAL_DIM), F32), gn, 512, R_CHUNK)
  y_p = _output(xp2, oa_p, or_p.reshape(bp * tp, R_V), w16, wpa, wpb, wo, lng, lnb, 512)

  xs2 = x_sample.reshape(bs * ts, D_MODEL)
  hs = _matmul(xs2, w16, 0, _OFF[3], F32, bs * ts, A_GW)
  caches = (cache_kv_w128, cache_kv_w512, cache_kv_w2048)
  q_s, news = [], []
  for g in range(3):
    col = lambda part: hs[:, _OFF[part] + g * A_GW:_OFF[part] + (g + 1) * A_GW]
    qg = col(0).reshape(bs, ts * A_HEADS, A_HEAD_DIM)
    q_s.append(jnp.pad(qg, ((0, 0), (0, 128 - ts * A_HEADS), (0, 0))))
    news.append(jnp.concatenate([col(1), col(2)], axis=1).reshape(bs, ts, 8, 128))
  q_s = jnp.stack(q_s, axis=1)
  caches8 = [c.reshape(bs, c.shape[1], 8, 128) for c in caches]
  oa_s, u0, u1, u2 = _attn_sample(q_s, caches8, news, slopes, ts)
  oa_s = oa_s[:, :ts * A_HEADS].reshape(bs * ts, A_OUT)
  kv_s = [u.reshape(bs, u.shape[1], 2, A_HEADS, A_HEAD_DIM) for u in (u0, u1, u2)]

  cpad = 16
  xs_pad = jnp.pad(x_sample, ((0, 0), (0, cpad - ts), (0, 0)))
  cos_s, sin_s = _rope_tables(PAST_LEN + jnp.arange(ts, dtype=jnp.int32))
  padtab = lambda a: jnp.pad(a, ((0, cpad - ts), (0, 0)))
  or_s, s_s = _retention(
      xs_pad, w16, padtab(cos_s), padtab(sin_s),
      _retention_tables(ts, cpad), state_ret.astype(F32), gn, cpad, cpad)
  or_s = or_s[:, :ts].reshape(bs * ts, R_V)
  y_s = _output(xs2, oa_s, or_s, w16, wpa, wpb, wo, lng, lnb, 256)

  return (y_p.reshape(bp, tp, D_MODEL), y_s.reshape(bs, ts, D_MODEL),
          kv_p[0], kv_p[1], kv_p[2], s_p,
          kv_s[0], kv_s[1], kv_s[2], s_s)
```

```python
import functools

import numpy as np
import jax
import jax.numpy as jnp
from jax import lax
from jax.experimental import pallas as pl
from jax.experimental.pallas import tpu as pltpu

D_MODEL = 1024
PAST_LEN = 16384
A_GROUPS = ((128, 1), (512, 4), (2048, 16))
A_HEADS = 4
A_HEAD_DIM = 128
A_GW = A_HEADS * A_HEAD_DIM
A_QKV = len(A_GROUPS) * A_GW
A_OUT = A_GW
N_DIST = 128
R_HEADS = 4
R_KEY_DIM = 128
R_VAL_DIM = 256
R_QK = R_HEADS * R_KEY_DIM
R_V = R_HEADS * R_VAL_DIM
R_CHUNK = 128
ROPE_BASE = 10000.0
EPS = 1e-5
NEG_INF = -1e30
DEEPNORM_ALPHA = 2.0 ** 0.25
_SPLIT = (A_QKV, A_QKV, A_QKV, A_OUT, R_QK, R_QK, R_V, R_V, D_MODEL, D_MODEL)
_OFF = tuple(int(v) for v in np.concatenate([[0], np.cumsum(_SPLIT)]))

BF16 = jnp.bfloat16
F32 = jnp.float32
VMEM_LIMIT = 56 * 1024 * 1024

_NT = (((1,), (1,)), ((), ()))
_TN = (((0,), (0,)), ((), ()))


def _dot(a, b):
  return jnp.dot(a, b, preferred_element_type=F32)


def _dot_nt(a, b):
  return lax.dot_general(a, b, _NT, preferred_element_type=F32)


def _dot_tn(a, b):
  return lax.dot_general(a, b, _TN, preferred_element_type=F32)


def _params(sem):
  return pltpu.CompilerParams(dimension_semantics=sem, vmem_limit_bytes=VMEM_LIMIT)


def _mm_kernel(x_ref, w_ref, o_ref):
  o_ref[...] = _dot(x_ref[...].astype(BF16), w_ref[...]).astype(o_ref.dtype)


def _matmul(x, w, col0, ncols, out_dtype, tm, tn):
  m, k = x.shape
  c0 = col0 // tn
  return pl.pallas_call(
      _mm_kernel,
      grid=(m // tm, ncols // tn),
      in_specs=[pl.BlockSpec((tm, k), lambda i, j: (i, 0)),
                pl.BlockSpec((k, tn), lambda i, j: (0, c0 + j))],
      out_specs=pl.BlockSpec((tm, tn), lambda i, j: (i, j)),
      out_shape=jax.ShapeDtypeStruct((m, ncols), out_dtype),
      compiler_params=_params(("parallel", "arbitrary")),
      name="proj_matmul",
  )(x, w)


PROJ_TM = 512


def _proj_group_kernel(x_ref, wq_ref, wk_ref, wv_ref, o_ref, tail_ref, *scratch,
                       d, first_tail, tail_rows):
  i = pl.program_id(1)
  tm = x_ref.shape[0]
  n = tm // d
  x = x_ref[...]
  xb = x.astype(BF16)
  if d == 1:
    perm = xb
  else:
    xs_ref, xp_ref = scratch
    for c in range(D_MODEL // 128):
      xs_ref[c] = x[:, c * 128:(c + 1) * 128]
    for r in range(d):
      for c in range(D_MODEL // 128):
        xp_ref[r * n:(r + 1) * n, c * 128:(c + 1) * 128] = (
            xs_ref[c, pl.ds(r, n, stride=d), :].astype(BF16))
    perm = xp_ref[...]
  nat = []
  for part, w_ref in enumerate((wq_ref, wk_ref, wv_ref)):
    res = _dot(perm, w_ref[...])
    nat.append(res)
    for r in range(d):
      for h in range(A_HEADS):
        o_ref[r, part * A_HEADS + h] = res[r * n:(r + 1) * n, h * 128:(h + 1) * 128].astype(BF16)

  @pl.when(i >= first_tail)
  def _():
    if d == 1:
      kv = nat[1:]
    else:
      kv = [_dot(xb, wk_ref[...]), _dot(xb, wv_ref[...])]
    for c in range(2 * A_HEADS):
      src = kv[c // A_HEADS][tm - tail_rows:, (c % A_HEADS) * 128:(c % A_HEADS + 1) * 128]
      tail_ref[pl.ds(c, tail_rows, stride=2 * A_HEADS), :] = src


def _proj_group(x2, w16, g, batch, seq):
  w, d = A_GROUPS[g]
  w = min(w, seq)
  tm = PROJ_TM
  nblk = seq // tm
  tail_rows = min(tm, w)
  first_tail = nblk - max(w // tm, 1)
  ls = seq // d
  wspec = lambda part: pl.BlockSpec((D_MODEL, A_GW), lambda b, i: (0, 3 * part + g))
  scratch = []
  if d > 1:
    scratch = [pltpu.VMEM((D_MODEL // 128, tm, 128), F32), pltpu.VMEM((tm, D_MODEL), BF16)]
  return pl.pallas_call(
      functools.partial(_proj_group_kernel, d=d, first_tail=first_tail, tail_rows=tail_rows),
      grid=(batch, nblk),
      in_specs=[pl.BlockSpec((tm, D_MODEL), lambda b, i: (b * nblk + i, 0)),
                wspec(0), wspec(1), wspec(2)],
      out_specs=[pl.BlockSpec((None, d, 3 * A_HEADS, tm // d, 128), lambda b, i: (b, 0, 0, i, 0)),
                 pl.BlockSpec((None, tail_rows * 8, 128),
                              lambda b, i: (b, jnp.maximum(i - first_tail, 0), 0))],
      out_shape=[jax.ShapeDtypeStruct((batch, d, 3 * A_HEADS, ls, 128), BF16),
                 jax.ShapeDtypeStruct((batch, w * 8, 128), F32)],
      scratch_shapes=scratch,
      compiler_params=_params(("parallel", "arbitrary")),
      name="proj_group",
  )(x2, w16, w16, w16)


ATT_BQ = 256


def _attn_prompt_kernel(slopes_ref, *refs, seq):
  qkv = refs[:9]
  o_ref = refs[9]
  oacc = refs[10:13]
  lacc = refs[13:16]
  bias_ref, s_buf, p_buf, l_buf = refs[16:20]
  h = pl.program_id(0)
  scale = A_HEAD_DIM ** -0.5

  @pl.when(pl.program_id(1) == 0)
  def _():
    for g, (_, d) in enumerate(A_GROUPS):
      ls = seq // d
      bq = min(ATT_BQ, ls)
      klen = min(bq + N_DIST, ls)
      slope = slopes_ref[g, h]
      qi = lax.broadcasted_iota(jnp.int32, (bq, klen), 0)
      kj = lax.broadcasted_iota(jnp.int32, (bq, klen), 1)
      for which, off in enumerate((0, N_DIST)):
        delta = qi - kj + off
        valid = (delta >= 0) & (delta <= N_DIST)
        dist = (delta * d).astype(F32)
        bias_ref[2 * g + which, :bq, :klen] = jnp.where(valid, -slope * dist, NEG_INF)

  for g, (_, d) in enumerate(A_GROUPS):
    q_ref, k_ref, v_ref = qkv[3 * g:3 * g + 3]
    ls = seq // d
    bq = min(ATT_BQ, ls)
    nqb = ls // bq
    klen = min(bq + N_DIST, ls)

    def coords(idx, nqb=nqb, bq=bq):
      r = idx // nqb
      qb = idx % nqb
      q0 = pl.multiple_of(qb * bq, bq)
      k0 = pl.multiple_of(jnp.maximum(qb * bq - N_DIST, 0), N_DIST)
      return r, qb, q0, k0

    def put_rows(ref, idx, val, d=d, bq=bq):
      r, _, q0, _ = coords(idx)
      if d == 1:
        ref[pl.ds(q0, bq), :] = val
      else:
        ref[pl.ds(r + q0 * d, bq, stride=d), :] = val

    def scores(idx, slot, bq=bq, klen=klen, q_ref=q_ref, k_ref=k_ref):
      r, _, q0, k0 = coords(idx)
      s_buf[slot, :bq, :klen] = _dot_nt(q_ref[r, pl.ds(q0, bq), :], k_ref[r, pl.ds(k0, klen), :])

    def softmax(idx, slot, g=g, bq=bq, klen=klen):
      _, qb, _, _ = coords(idx)
      s = s_buf[slot, :bq, :klen] * scale + bias_ref[2 * g + jnp.minimum(qb, 1), :bq, :klen]
      m = jnp.max(s, axis=-1, keepdims=True)
      p = jnp.exp(s - m)
      l = jnp.sum(p, axis=-1, keepdims=True)
      p_buf[slot, :bq, :klen] = p.astype(BF16)
      l_buf[slot, :bq, :] = jnp.broadcast_to(l, (bq, 128))
      put_rows(lacc[g], idx, jnp.broadcast_to(m + jnp.log(l), (bq, 128)))

    def values(idx, slot, g=g, bq=bq, klen=klen, v_ref=v_ref):
      r, _, _, k0 = coords(idx)
      o = _dot(p_buf[slot, :bq, :klen], v_ref[r, pl.ds(k0, klen), :]) / l_buf[slot, :bq, :]
      put_rows(oacc[g], idx, o)

    nblk = d * nqb
    scores(0, 0)
    scores(1, 1)
    softmax(0, 0)

    def steady(j, carry):
      i = 2 + 2 * j
      values(i - 2, 0)
      softmax(i - 1, 1)
      scores(i, 0)
      values(i - 1, 1)
      softmax(i, 0)
      scores(i + 1, 1)
      return carry

    lax.fori_loop(0, (nblk - 2) // 2, steady, 0)
    values(nblk - 2, 0)
    softmax(nblk - 1, 1)
    values(nblk - 1, 1)

  mrows = 512

  def merge(i, carry):
    r0 = pl.multiple_of(i * mrows, mrows)
    l0 = lacc[0][pl.ds(r0, mrows), :]
    l1 = lacc[1][pl.ds(r0, mrows), :]
    l2 = lacc[2][pl.ds(r0, mrows), :]
    mx = jnp.maximum(jnp.maximum(l0, l1), l2)
    e0 = jnp.exp(l0 - mx)
    e1 = jnp.exp(l1 - mx)
    e2 = jnp.exp(l2 - mx)
    den = e0 + e1 + e2
    acc = (e0 * oacc[0][pl.ds(r0, mrows), :] + e1 * oacc[1][pl.ds(r0, mrows), :]
           + e2 * oacc[2][pl.ds(r0, mrows), :])
    o_ref[pl.ds(r0, mrows), :] = acc / den
    return carry

  lax.fori_loop(0, seq // mrows, merge, 0)


def _attn_prompt(qkvs, slopes, batch, seq):
  in_specs = [pl.BlockSpec(memory_space=pltpu.SMEM)]
  args = [slopes]
  for g, (_, d) in enumerate(A_GROUPS):
    ls = seq // d
    for part in range(3):
      in_specs.append(pl.BlockSpec(
          (None, d, None, ls, 128),
          lambda h, b, part=part: (b, 0, part * A_HEADS + h, 0, 0)))
      args.append(qkvs[g])
  scratch = ([pltpu.VMEM((seq, 128), F32) for _ in range(6)]
             + [pltpu.VMEM((6, ATT_BQ, ATT_BQ + N_DIST), F32),
                pltpu.VMEM((2, ATT_BQ, ATT_BQ + N_DIST), F32),
                pltpu.VMEM((2, ATT_BQ, ATT_BQ + N_DIST), BF16),
                pltpu.VMEM((2, ATT_BQ, 128), F32)])
  return pl.pallas_call(
      functools.partial(_attn_prompt_kernel, seq=seq),
      grid=(A_HEADS, batch),
      in_specs=in_specs,
      out_specs=pl.BlockSpec((None, seq, 128), lambda h, b: (b, 0, h)),
      out_shape=jax.ShapeDtypeStruct((batch, seq, A_OUT), F32),
      scratch_shapes=scratch,
      compiler_params=_params(("parallel", "arbitrary")),
      name="attn_prompt",
  )(*args)


ROLL_ELSEWHERE = 2


def _attn_sample_kernel(slopes_ref, q_ref, c0_ref, c1_ref, c2_ref, n0_ref, n1_ref, n2_ref,
                        o_ref, u0_ref, u1_ref, sems, *, tq):
  b = pl.program_id(0)
  cache_refs = (c0_ref, c1_ref, c2_ref)
  new_refs = (n0_ref, n1_ref, n2_ref)
  out_refs = (u0_ref, u1_ref)

  copies = []
  for g, (w, _) in enumerate(A_GROUPS):
    if g == ROLL_ELSEWHERE:
      continue
    body = pltpu.make_async_copy(cache_refs[g].at[0, pl.ds(tq, w - tq)],
                                 out_refs[g].at[b, pl.ds(0, w - tq)], sems.at[2 * g])
    tail = pltpu.make_async_copy(new_refs[g].at[0], out_refs[g].at[b, pl.ds(w - tq, tq)],
                                 sems.at[2 * g + 1])
    body.start()
    tail.start()
    copies += [body, tail]

  scale = A_HEAD_DIM ** -0.5
  nrow = N_DIST * 8 + tq * 8
  ri = lax.broadcasted_iota(jnp.int32, (nrow, 128), 0)
  cj = lax.broadcasted_iota(jnp.int32, (nrow, 128), 1)
  slot = ri % 8
  key = ri // 8
  tcol = cj // A_HEADS
  hcol = cj % A_HEADS
  col_ok = cj < tq * A_HEADS
  lse_g = []
  ot_g = []
  for g, (w, d) in enumerate(A_GROUPS):
    qg = q_ref[g]
    xn = new_refs[g][0].reshape(tq * 8, 128)
    nres = min(d, tq)
    xs = []
    s = jnp.zeros((nrow, 128), F32)
    col1 = lax.broadcasted_iota(jnp.int32, (128, 1), 0) // A_HEADS
    for rho in range(nres):
      if g == ROLL_ELSEWHERE:
        xc = cache_refs[g][0, :, rho, :, :]
      elif d == 1:
        xc = cache_refs[g][0]
      else:
        xc = cache_refs[g][0, pl.ds(rho, N_DIST, stride=d), :, :]
      xc = xc.reshape(N_DIST * 8, 128)
      x = jnp.concatenate([xc, xn], axis=0).astype(BF16)
      xs.append(x)
      qm = jnp.where(col1 % d == rho, qg, 0.0).astype(BF16)
      s = s + _dot_nt(x, qm)
    qt = tcol // d
    nidx = key - N_DIST
    in_cache = key < N_DIST
    valid_c = in_cache & (key >= qt)
    dist_c = d * (N_DIST + qt - key)
    valid_n = (~in_cache) & (nidx <= tcol) & ((tcol - nidx) % d == 0)
    dist_n = tcol - nidx
    valid = (valid_c | valid_n) & (slot == hcol) & col_ok
    dist = jnp.where(in_cache, dist_c, dist_n).astype(F32)
    slope = jnp.zeros((nrow, 128), F32)
    for hh in range(A_HEADS):
      slope = jnp.where(hcol == hh, slopes_ref[g, hh], slope)
    s = jnp.where(valid, s * scale - slope * dist, NEG_INF)
    m = jnp.max(s, axis=0, keepdims=True)
    p = jnp.where(valid, jnp.exp(s - m), 0.0)
    l = jnp.sum(p, axis=0, keepdims=True)
    l = jnp.where(l > 0.0, l, 1.0)
    pn = p / l
    pv = pltpu.roll(pn, 4, 0)
    ot = jnp.zeros((128, 128), F32)
    col_row = lax.broadcasted_iota(jnp.int32, (1, 128), 1) // A_HEADS
    for rho in range(nres):
      pm = jnp.where(col_row % d == rho, pv, 0.0).astype(BF16)
      ot = ot + _dot_tn(xs[rho], pm)
    ot_g.append(ot)
    lse_g.append(m + jnp.log(l))
  mx = jnp.maximum(jnp.maximum(lse_g[0], lse_g[1]), lse_g[2])
  es = [jnp.exp(v - mx) for v in lse_g]
  den = es[0] + es[1] + es[2]
  ot = (es[0] * ot_g[0] + es[1] * ot_g[1] + es[2] * ot_g[2]) / den
  o_ref[...] = ot.T

  for c in copies:
    c.wait()


def _attn_sample(q, caches, news, slopes, tq):
  batch = q.shape[0]
  in_specs = [pl.BlockSpec(memory_space=pltpu.SMEM),
              pl.BlockSpec((None, 3, 128, 128), lambda b: (b, 0, 0, 0))]
  args = [slopes, q]
  for g, (w, d) in enumerate(A_GROUPS):
    if g == ROLL_ELSEWHERE:
      in_specs.append(pl.BlockSpec((1, N_DIST, min(d, tq), 8, 128), lambda b: (b, 0, 0, 0, 0)))
      args.append(caches[g].reshape(batch, N_DIST, d, 8, 128))
    else:
      in_specs.append(pl.BlockSpec((1, w, 8, 128), lambda b: (b, 0, 0, 0)))
      args.append(caches[g])
  for g in range(3):
    in_specs.append(pl.BlockSpec((1, tq, 8, 128), lambda b: (b, 0, 0, 0)))
    args.append(news[g])
  out_shape = [jax.ShapeDtypeStruct((batch, 128, 128), F32)]
  out_specs = [pl.BlockSpec((None, 128, 128), lambda b: (b, 0, 0))]
  for g, (w, _) in enumerate(A_GROUPS):
    if g != ROLL_ELSEWHERE:
      out_shape.append(jax.ShapeDtypeStruct((batch, w, 8, 128), F32))
      out_specs.append(pl.BlockSpec(memory_space=pl.ANY))
  return pl.pallas_call(
      functools.partial(_attn_sample_kernel, tq=tq),
      grid=(batch,),
      in_specs=in_specs,
      out_specs=out_specs,
      out_shape=out_shape,
      scratch_shapes=[pltpu.SemaphoreType.DMA((2 * len(A_GROUPS),))],
      compiler_params=_params(("arbitrary",)),
      name="attn_sample",
  )(*args)


def _retention_kernel(x_ref, wqk_ref, wv_ref, cos_ref, sin_ref, dmat_ref, qdec_ref, kdec_ref,
                      cdec_ref, s0_ref, gn_ref, o_ref, sout_ref, state, qk_ref, v_ref, *, chunk):
  j = pl.program_id(1)

  @pl.when(j == 0)
  def _():
    state[...] = s0_ref[...]

  xb = x_ref[...].astype(BF16)
  qk_ref[...] = _dot(xb, wqk_ref[...])
  v_ref[...] = _dot(xb, wv_ref[...]).astype(BF16)

  nchunk = x_ref.shape[0] // chunk
  kscale = R_KEY_DIM ** -0.5
  for ci in range(nchunk):
    rows = pl.ds(ci * chunk, chunk)
    cosv = cos_ref[rows, :]
    sinv = sin_ref[rows, :]
    for h in range(R_HEADS):
      qh = qk_ref[rows, h * 128:(h + 1) * 128]
      kh = qk_ref[rows, R_QK + h * 128:R_QK + (h + 1) * 128]
      qrot = qh * cosv + pltpu.roll(qh, 64, 1) * sinv
      krot = (kh * cosv + pltpu.roll(kh, 64, 1) * sinv) * kscale
      qb = qrot.astype(BF16)
      vh = v_ref[rows, h * 256:(h + 1) * 256]
      att = _dot_nt(qb, krot.astype(BF16)) * dmat_ref[h]
      sh = state[h]
      o = _dot(att.astype(BF16), vh) + _dot(qb, sh.astype(BF16)) * qdec_ref[h]
      state[h] = cdec_ref[h] * sh + _dot_tn((krot * kdec_ref[h]).astype(BF16), vh)
      mu = jnp.mean(o, axis=-1, keepdims=True)
      var = jnp.mean(jnp.square(o - mu), axis=-1, keepdims=True)
      on = (o - mu) * lax.rsqrt(var + EPS) * gn_ref[:, h * 256:(h + 1) * 256]
      o_ref[rows, h * 256:(h + 1) * 256] = on

  @pl.when(j == pl.num_programs(1) - 1)
  def _():
    sout_ref[...] = state[...]


def _retention(x, w16, cos, sin, tables, s0, gn, rows_per_step, chunk):
  batch, t, _ = x.shape
  dmat, qdec, kdec, cdec = tables
  tc = rows_per_step
  const3 = lambda b, j: (0, 0, 0)
  wslab = lambda off: pl.BlockSpec((D_MODEL, R_V), lambda b, j: (0, off // R_V),
                                   pipeline_mode=pl.Buffered(1))
  return pl.pallas_call(
      functools.partial(_retention_kernel, chunk=chunk),
      grid=(batch, t // tc),
      in_specs=[pl.BlockSpec((None, tc, D_MODEL), lambda b, j: (b, j, 0)),
                wslab(_OFF[4]), wslab(_OFF[6]),
                pl.BlockSpec((tc, 128), lambda b, j: (j, 0)),
                pl.BlockSpec((tc, 128), lambda b, j: (j, 0)),
                pl.BlockSpec(dmat.shape, const3),
                pl.BlockSpec(qdec.shape, const3),
                pl.BlockSpec(kdec.shape, const3),
                pl.BlockSpec(cdec.shape, const3),
                pl.BlockSpec((None, R_HEADS, R_KEY_DIM, R_VAL_DIM), lambda b, j: (b, 0, 0, 0)),
                pl.BlockSpec((1, R_V), lambda b, j: (0, 0))],
      out_specs=[pl.BlockSpec((None, tc, R_V), lambda b, j: (b, j, 0)),
                 pl.BlockSpec((None, R_HEADS, R_KEY_DIM, R_VAL_DIM), lambda b, j: (b, 0, 0, 0))],
      out_shape=[jax.ShapeDtypeStruct((batch, t, R_V), F32),
                 jax.ShapeDtypeStruct((batch, R_HEADS, R_KEY_DIM, R_VAL_DIM), F32)],
      scratch_shapes=[pltpu.VMEM((R_HEADS, R_KEY_DIM, R_VAL_DIM), F32),
                      pltpu.VMEM((tc, 2 * R_QK), F32), pltpu.VMEM((tc, R_V), BF16)],
      compiler_params=_params(("parallel", "arbitrary")),
      name="retention",
  )(x, w16, w16, cos, sin, dmat, qdec, kdec, cdec, s0, gn)


def _retention_tables(c_true, c_pad):
  lg = jnp.log1p(-jnp.exp2(-5.0 - jnp.arange(R_HEADS, dtype=F32)))
  i = jnp.arange(c_pad, dtype=F32)
  live = i < c_true
  diff = i[:, None] - i[None, :]
  causal = (diff >= 0) & live[:, None] & live[None, :]
  dmat = jnp.where(causal[None], jnp.exp(jnp.where(causal, diff, 0.0)[None] * lg[:, None, None]), 0.0)
  qdec = jnp.where(live[None, :], jnp.exp((i[None, :] + 1.0) * lg[:, None]), 0.0)
  kdec = jnp.where(live[None, :], jnp.exp((c_true - 1.0 - i)[None, :] * lg[:, None]), 0.0)
  cdec = jnp.exp(c_true * lg)
  qdec = jnp.broadcast_to(qdec[:, :, None], (R_HEADS, c_pad, R_VAL_DIM))
  kdec = jnp.broadcast_to(kdec[:, :, None], (R_HEADS, c_pad, R_KEY_DIM))
  cdec = jnp.broadcast_to(cdec[:, None, None], (R_HEADS, R_KEY_DIM, R_VAL_DIM))
  return dmat, qdec, kdec, cdec


def _rope_tables(pos):
  half = R_KEY_DIM // 2
  inv = 1.0 / (ROPE_BASE ** jnp.linspace(0.0, 1.0, half, dtype=F32))
  ang = pos.astype(F32)[:, None] * inv[None, :]
  cos = jnp.cos(ang)
  sin = jnp.sin(ang)
  return jnp.concatenate([cos, cos], axis=-1), jnp.concatenate([-sin, sin], axis=-1)


def _output_kernel(x_ref, oa_ref, or_ref, wza_ref, wzr_ref, wga_ref, wgb_ref, wpa_ref, wpb_ref,
                   wo_ref, lng_ref, lnb_ref, y_ref):
  x = x_ref[...]
  xb = x.astype(BF16)
  za = _dot(xb, wza_ref[...])
  ya = _dot((jax.nn.silu(za) * oa_ref[...]).astype(BF16), wpa_ref[...])
  zr = _dot(xb, wzr_ref[...])
  yb = _dot((jax.nn.silu(zr) * or_ref[...]).astype(BF16), wpb_ref[...])
  ga = _dot(xb, wga_ref[...])
  gb = _dot(xb, wgb_ref[...])
  mix = jax.nn.sigmoid(ga) * ya + jax.nn.sigmoid(gb) * yb
  out = _dot(mix.astype(BF16), wo_ref[...])
  zz = DEEPNORM_ALPHA * x + out
  zm = jnp.mean(zz, axis=-1, keepdims=True)
  zv = jnp.mean(jnp.square(zz - zm), axis=-1, keepdims=True)
  y_ref[...] = (zz - zm) * lax.rsqrt(zv + EPS) * lng_ref[...] + lnb_ref[...]


def _output_roll_kernel(*refs, tq):
  ins, (cache_hbm, new_ref, y_ref, roll_hbm, stage, sem_in, sem_out, sem_new) = refs[:12], refs[12:]
  i = pl.program_id(0)
  n = pl.num_programs(0)
  w = stage.shape[1]
  slot = i % 2

  def fetch(step, s):
    return pltpu.make_async_copy(cache_hbm.at[step], stage.at[s], sem_in.at[s])

  def write_back(step, s):
    return pltpu.make_async_copy(stage.at[s, pl.ds(tq, w - tq)],
                                 roll_hbm.at[step, pl.ds(0, w - tq)], sem_out.at[s])

  new_rows = pltpu.make_async_copy(new_ref.at[0], roll_hbm.at[i, pl.ds(w - tq, tq)], sem_new.at[0])

  @pl.when(i == 0)
  def _():
    fetch(0, 0).start()

  fetch(i, slot).wait()
  write_back(i, slot).start()
  new_rows.start()

  @pl.when(i >= 1)
  def _():
    write_back(i - 1, 1 - slot).wait()

  @pl.when(i + 1 < n)
  def _():
    fetch(i + 1, 1 - slot).start()

  _output_kernel(*ins, y_ref)
  new_rows.wait()

  @pl.when(i == n - 1)
  def _():
    write_back(i, slot).wait()


def _output(x, oa, orr, w16, wpa, wpb, wo, lng, lnb, tm, roll=None):
  m = x.shape[0]
  row = lambda w: pl.BlockSpec((tm, w), lambda i: (i, 0))
  full = lambda a: pl.BlockSpec(a.shape, lambda i: (0, 0), pipeline_mode=pl.Buffered(1))
  wcol = lambda off, width: pl.BlockSpec((D_MODEL, width), lambda i: (0, off // width),
                                         pipeline_mode=pl.Buffered(1))
  in_specs = [row(D_MODEL), row(A_OUT), row(R_V),
              wcol(_OFF[3], A_OUT), wcol(_OFF[7], R_V), wcol(_OFF[8], D_MODEL),
              wcol(_OFF[9], D_MODEL),
              full(wpa), full(wpb), full(wo), full(lng), full(lnb)]
  args = [x, oa, orr, w16, w16, w16, w16, wpa, wpb, wo, lng, lnb]
  y_shape = jax.ShapeDtypeStruct((m, D_MODEL), F32)
  if roll is None:
    return pl.pallas_call(
        _output_kernel,
        grid=(m // tm,),
        in_specs=in_specs,
        out_specs=row(D_MODEL),
        out_shape=y_shape,
        compiler_params=_params(("parallel",)),
        name="gates_output",
    )(*args)
  cache, new = roll
  assert cache.shape[0] == m // tm, "one window buffer per grid step"
  tq = new.shape[1]
  return pl.pallas_call(
      functools.partial(_output_roll_kernel, tq=tq),
      grid=(m // tm,),
      in_specs=in_specs + [pl.BlockSpec(memory_space=pl.ANY),
                           pl.BlockSpec((1,) + new.shape[1:], lambda i: (i, 0, 0, 0))],
      out_specs=[row(D_MODEL), pl.BlockSpec(memory_space=pl.ANY)],
      out_shape=[y_shape, jax.ShapeDtypeStruct(cache.shape, cache.dtype)],
      scratch_shapes=[pltpu.VMEM((2,) + cache.shape[1:], cache.dtype),
                      pltpu.SemaphoreType.DMA((2,)), pltpu.SemaphoreType.DMA((2,)),
                      pltpu.SemaphoreType.DMA((1,))],
      compiler_params=_params(("arbitrary",)),
      name="gates_output_roll",
  )(*args, cache, new)


def _alibi_slopes():
  n = len(A_GROUPS) * A_HEADS
  return jnp.exp2(-8.0 * jnp.arange(1, n + 1, dtype=F32) / n).reshape(len(A_GROUPS), A_HEADS)


def kernel(x_prompt, x_sample, cache_kv_w128, cache_kv_w512, cache_kv_w2048, state_ret,
           w_in, w_pa, w_pb, w_o, gn_g, ln_g, ln_b):
  bp, tp, _ = x_prompt.shape
  bs, ts, _ = x_sample.shape
  slopes = _alibi_slopes()
  w16 = w_in.astype(BF16)
  wpa = w_pa.astype(BF16)
  wpb = w_pb.astype(BF16)
  wo = w_o.astype(BF16)
  gn = gn_g.reshape(1, R_V)
  lng = ln_g.reshape(1, D_MODEL)
  lnb = ln_b.reshape(1, D_MODEL)

  xp2 = x_prompt.reshape(bp * tp, D_MODEL)
  qkvs, kv_p = [], []
  for g, (w, _) in enumerate(A_GROUPS):
    qkv, tail = _proj_group(xp2, w16, g, bp, tp)
    qkvs.append(qkv)
    kv_p.append(tail.reshape(bp, min(w, tp), 2, A_HEADS, A_HEAD_DIM))
  oa_p = _attn_prompt(qkvs, slopes, bp, tp).reshape(bp * tp, A_OUT)

  cos_p, sin_p = _rope_tables(jnp.arange(tp, dtype=jnp.int32))
  or_p, s_p = _retention(
      x_prompt, w16, cos_p, sin_p, _retention_tables(R_CHUNK, R_CHUNK),
      jnp.zeros((bp, R_HEADS, R_KEY_DIM, R_VAL_DIM), F32), gn, 512, R_CHUNK)

  xs2 = x_sample.reshape(bs * ts, D_MODEL)
  hs = _matmul(xs2, w16, 0, _OFF[3], F32, bs * ts, A_GW)
  caches = (cache_kv_w128, cache_kv_w512, cache_kv_w2048)
  q_s, news = [], []
  for g in range(3):
    col = lambda part: hs[:, _OFF[part] + g * A_GW:_OFF[part] + (g + 1) * A_GW]
    qg = col(0).reshape(bs, ts * A_HEADS, A_HEAD_DIM)
    q_s.append(jnp.pad(qg, ((0, 0), (0, 128 - ts * A_HEADS), (0, 0))))
    news.append(jnp.concatenate([col(1), col(2)], axis=1).reshape(bs, ts, 8, 128))
  q_s = jnp.stack(q_s, axis=1)
  caches8 = [c.reshape(bs, c.shape[1], 8, 128) for c in caches]
  oa_s, u0, u1 = _attn_sample(q_s, caches8, news, slopes, ts)
  oa_s = oa_s[:, :ts * A_HEADS].reshape(bs * ts, A_OUT)

  y_p, u2 = _output(xp2, oa_p, or_p.reshape(bp * tp, R_V), w16, wpa, wpb, wo, lng, lnb,
                    bp * tp // bs, roll=(caches8[ROLL_ELSEWHERE], news[ROLL_ELSEWHERE]))
  kv_s = [u.reshape(bs, u.shape[1], 2, A_HEADS, A_HEAD_DIM) for u in (u0, u1, u2)]

  cpad = 16
  xs_pad = jnp.pad(x_sample, ((0, 0), (0, cpad - ts), (0, 0)))
  cos_s, sin_s = _rope_tables(PAST_LEN + jnp.arange(ts, dtype=jnp.int32))
  padtab = lambda a: jnp.pad(a, ((0, cpad - ts), (0, 0)))
  or_s, s_s = _retention(
      xs_pad, w16, padtab(cos_s), padtab(sin_s),
      _retention_tables(ts, cpad), state_ret.astype(F32), gn, cpad, cpad)
  or_s = or_s[:, :ts].reshape(bs * ts, R_V)
  y_s = _output(xs2, oa_s, or_s, w16, wpa, wpb, wo, lng, lnb, 256)

  return (y_p.reshape(bp, tp, D_MODEL), y_s.reshape(bs, ts, D_MODEL),
          kv_p[0], kv_p[1], kv_p[2], s_p,
          kv_s[0], kv_s[1], kv_s[2], s_s)
```

```python
import functools

import numpy as np
import jax
import jax.numpy as jnp
from jax import lax
from jax.experimental import pallas as pl
from jax.experimental.pallas import tpu as pltpu

D_MODEL = 1024
PAST_LEN = 16384
A_GROUPS = ((128, 1), (512, 4), (2048, 16))
A_HEADS = 4
A_HEAD_DIM = 128
A_GW = A_HEADS * A_HEAD_DIM
A_QKV = len(A_GROUPS) * A_GW
A_OUT = A_GW
N_DIST = 128
R_HEADS = 4
R_KEY_DIM = 128
R_VAL_DIM = 256
R_QK = R_HEADS * R_KEY_DIM
R_V = R_HEADS * R_VAL_DIM
R_CHUNK = 128
ROPE_BASE = 10000.0
EPS = 1e-5
NEG_INF = -1e30
DEEPNORM_ALPHA = 2.0 ** 0.25
_SPLIT = (A_QKV, A_QKV, A_QKV, A_OUT, R_QK, R_QK, R_V, R_V, D_MODEL, D_MODEL)
_OFF = tuple(int(v) for v in np.concatenate([[0], np.cumsum(_SPLIT)]))

BF16 = jnp.bfloat16
F32 = jnp.float32
VMEM_LIMIT = 56 * 1024 * 1024

_NT = (((1,), (1,)), ((), ()))
_TN = (((0,), (0,)), ((), ()))


def _dot(a, b):
  return jnp.dot(a, b, preferred_element_type=F32)


def _dot_nt(a, b):
  return lax.dot_general(a, b, _NT, preferred_element_type=F32)


def _dot_tn(a, b):
  return lax.dot_general(a, b, _TN, preferred_element_type=F32)


def _params(sem):
  return pltpu.CompilerParams(dimension_semantics=sem, vmem_limit_bytes=VMEM_LIMIT)


def _mm_kernel(x_ref, w_ref, o_ref):
  o_ref[...] = _dot(x_ref[...].astype(BF16), w_ref[...]).astype(o_ref.dtype)


def _matmul(x, w, col0, ncols, out_dtype, tm, tn):
  m, k = x.shape
  c0 = col0 // tn
  return pl.pallas_call(
      _mm_kernel,
      grid=(m // tm, ncols // tn),
      in_specs=[pl.BlockSpec((tm, k), lambda i, j: (i, 0)),
                pl.BlockSpec((k, tn), lambda i, j: (0, c0 + j))],
      out_specs=pl.BlockSpec((tm, tn), lambda i, j: (i, j)),
      out_shape=jax.ShapeDtypeStruct((m, ncols), out_dtype),
      compiler_params=_params(("parallel", "arbitrary")),
      name="proj_matmul",
  )(x, w)


PROJ_TM = 512


def _proj_group_kernel(x_ref, wq_ref, wk_ref, wv_ref, o_ref, tail_ref, *scratch,
                       d, first_tail, tail_rows):
  i = pl.program_id(1)
  tm = x_ref.shape[0]
  n = tm // d
  x = x_ref[...]
  xb = x.astype(BF16)
  if d == 1:
    perm = xb
  else:
    xs_ref, xp_ref = scratch
    for c in range(D_MODEL // 128):
      xs_ref[c] = x[:, c * 128:(c + 1) * 128]
    for r in range(d):
      for c in range(D_MODEL // 128):
        xp_ref[r * n:(r + 1) * n, c * 128:(c + 1) * 128] = (
            xs_ref[c, pl.ds(r, n, stride=d), :].astype(BF16))
    perm = xp_ref[...]
  nat = []
  for part, w_ref in enumerate((wq_ref, wk_ref, wv_ref)):
    res = _dot(perm, w_ref[...])
    nat.append(res)
    for r in range(d):
      for h in range(A_HEADS):
        o_ref[r, part * A_HEADS + h] = res[r * n:(r + 1) * n, h * 128:(h + 1) * 128].astype(BF16)

  @pl.when(i >= first_tail)
  def _():
    if d == 1:
      kv = nat[1:]
    else:
      kv = [_dot(xb, wk_ref[...]), _dot(xb, wv_ref[...])]
    for c in range(2 * A_HEADS):
      src = kv[c // A_HEADS][tm - tail_rows:, (c % A_HEADS) * 128:(c % A_HEADS + 1) * 128]
      tail_ref[pl.ds(c, tail_rows, stride=2 * A_HEADS), :] = src


def _proj_group(x2, w16, g, batch, seq):
  w, d = A_GROUPS[g]
  w = min(w, seq)
  tm = PROJ_TM
  nblk = seq // tm
  tail_rows = min(tm, w)
  first_tail = nblk - max(w // tm, 1)
  ls = seq // d
  wspec = lambda part: pl.BlockSpec((D_MODEL, A_GW), lambda b, i: (0, 3 * part + g))
  scratch = []
  if d > 1:
    scratch = [pltpu.VMEM((D_MODEL // 128, tm, 128), F32), pltpu.VMEM((tm, D_MODEL), BF16)]
  return pl.pallas_call(
      functools.partial(_proj_group_kernel, d=d, first_tail=first_tail, tail_rows=tail_rows),
      grid=(batch, nblk),
      in_specs=[pl.BlockSpec((tm, D_MODEL), lambda b, i: (b * nblk + i, 0)),
                wspec(0), wspec(1), wspec(2)],
      out_specs=[pl.BlockSpec((None, d, 3 * A_HEADS, tm // d, 128), lambda b, i: (b, 0, 0, i, 0)),
                 pl.BlockSpec((None, tail_rows * 8, 128),
                              lambda b, i: (b, jnp.maximum(i - first_tail, 0), 0))],
      out_shape=[jax.ShapeDtypeStruct((batch, d, 3 * A_HEADS, ls, 128), BF16),
                 jax.ShapeDtypeStruct((batch, w * 8, 128), F32)],
      scratch_shapes=scratch,
      compiler_params=_params(("parallel", "arbitrary")),
      name="proj_group",
  )(x2, w16, w16, w16)


ATT_BQ = 256
LOG2E = 1.4426950408889634


def _attn_prompt_kernel(slopes_ref, *refs, seq):
  qkv = refs[:9]
  o_ref = refs[9]
  oacc = refs[10:13]
  lacc = refs[13:16]
  bias_ref, s_buf, p_buf, l_buf = refs[16:20]
  h = pl.program_id(0)
  scale = A_HEAD_DIM ** -0.5

  @pl.when(pl.program_id(1) == 0)
  def _():
    for g, (_, d) in enumerate(A_GROUPS):
      ls = seq // d
      bq = min(ATT_BQ, ls)
      klen = min(bq + N_DIST, ls)
      slope = slopes_ref[g, h]
      qi = lax.broadcasted_iota(jnp.int32, (bq, klen), 0)
      kj = lax.broadcasted_iota(jnp.int32, (bq, klen), 1)
      for which, off in enumerate((0, N_DIST)):
        delta = qi - kj + off
        valid = (delta >= 0) & (delta <= N_DIST)
        dist = (delta * d).astype(F32)
        bias_ref[2 * g + which, :bq, :klen] = jnp.where(valid, -(slope * LOG2E) * dist, NEG_INF)

  for g, (_, d) in enumerate(A_GROUPS):
    q_ref, k_ref, v_ref = qkv[3 * g:3 * g + 3]
    ls = seq // d
    bq = min(ATT_BQ, ls)
    nqb = ls // bq
    klen = min(bq + N_DIST, ls)

    def coords(idx, nqb=nqb, bq=bq):
      r = idx // nqb
      qb = idx % nqb
      q0 = pl.multiple_of(qb * bq, bq)
      k0 = pl.multiple_of(jnp.maximum(qb * bq - N_DIST, 0), N_DIST)
      return r, qb, q0, k0

    def put_rows(ref, idx, val, d=d, bq=bq):
      r, _, q0, _ = coords(idx)
      if d == 1:
        ref[pl.ds(q0, bq), :] = val
      else:
        ref[pl.ds(r + q0 * d, bq, stride=d), :] = val

    def scores(idx, slot, bq=bq, klen=klen, q_ref=q_ref, k_ref=k_ref):
      r, _, q0, k0 = coords(idx)
      s_buf[slot, :bq, :klen] = _dot_nt(q_ref[r, pl.ds(q0, bq), :], k_ref[r, pl.ds(k0, klen), :])

    def softmax(idx, slot, g=g, bq=bq, klen=klen):
      _, qb, _, _ = coords(idx)
      s = (s_buf[slot, :bq, :klen] * (scale * LOG2E)
           + bias_ref[2 * g + jnp.minimum(qb, 1), :bq, :klen])
      m = jnp.max(s, axis=-1, keepdims=True)
      p = jnp.exp2(s - m)
      l = jnp.sum(p, axis=-1, keepdims=True)
      p_buf[slot, :bq, :klen] = p.astype(BF16)
      l_buf[slot, :bq, :] = jnp.broadcast_to(l, (bq, 128))
      put_rows(lacc[g], idx, jnp.broadcast_to(m + jnp.log2(l), (bq, 128)))

    def values(idx, slot, g=g, bq=bq, klen=klen, v_ref=v_ref):
      r, _, _, k0 = coords(idx)
      o = _dot(p_buf[slot, :bq, :klen], v_ref[r, pl.ds(k0, klen), :]) / l_buf[slot, :bq, :]
      put_rows(oacc[g], idx, o)

    nblk = d * nqb
    scores(0, 0)
    scores(1, 1)
    softmax(0, 0)
    softmax(1, 1)
    scores(2, 2)
    scores(3, 3)

    def steady(j, carry):
      i = 4 + 4 * j
      values(i - 4, 0)
      values(i - 3, 1)
      softmax(i - 2, 2)
      softmax(i - 1, 3)
      scores(i, 0)
      scores(i + 1, 1)
      values(i - 2, 2)
      values(i - 1, 3)
      softmax(i, 0)
      softmax(i + 1, 1)
      scores(i + 2, 2)
      scores(i + 3, 3)
      return carry

    lax.fori_loop(0, (nblk - 4) // 4, steady, 0)
    values(nblk - 4, 0)
    values(nblk - 3, 1)
    softmax(nblk - 2, 2)
    softmax(nblk - 1, 3)
    values(nblk - 2, 2)
    values(nblk - 1, 3)

  mrows = 512

  def merge(i, carry):
    r0 = pl.multiple_of(i * mrows, mrows)
    l0 = lacc[0][pl.ds(r0, mrows), :]
    l1 = lacc[1][pl.ds(r0, mrows), :]
    l2 = lacc[2][pl.ds(r0, mrows), :]
    mx = jnp.maximum(jnp.maximum(l0, l1), l2)
    e0 = jnp.exp2(l0 - mx)
    e1 = jnp.exp2(l1 - mx)
    e2 = jnp.exp2(l2 - mx)
    den = e0 + e1 + e2
    acc = (e0 * oacc[0][pl.ds(r0, mrows), :] + e1 * oacc[1][pl.ds(r0, mrows), :]
           + e2 * oacc[2][pl.ds(r0, mrows), :])
    o_ref[pl.ds(r0, mrows), :] = acc / den
    return carry

  lax.fori_loop(0, seq // mrows, merge, 0)


def _attn_prompt(qkvs, slopes, batch, seq):
  in_specs = [pl.BlockSpec(memory_space=pltpu.SMEM)]
  args = [slopes]
  for g, (_, d) in enumerate(A_GROUPS):
    ls = seq // d
    for part in range(3):
      in_specs.append(pl.BlockSpec(
          (None, d, None, ls, 128),
          lambda h, b, part=part: (b, 0, part * A_HEADS + h, 0, 0)))
      args.append(qkvs[g])
  scratch = ([pltpu.VMEM((seq, 128), F32) for _ in range(6)]
             + [pltpu.VMEM((6, ATT_BQ, ATT_BQ + N_DIST), F32),
                pltpu.VMEM((4, ATT_BQ, ATT_BQ + N_DIST), F32),
                pltpu.VMEM((4, ATT_BQ, ATT_BQ + N_DIST), BF16),
                pltpu.VMEM((4, ATT_BQ, 128), F32)])
  return pl.pallas_call(
      functools.partial(_attn_prompt_kernel, seq=seq),
      grid=(A_HEADS, batch),
      in_specs=in_specs,
      out_specs=pl.BlockSpec((None, seq, 128), lambda h, b: (b, 0, h)),
      out_shape=jax.ShapeDtypeStruct((batch, seq, A_OUT), F32),
      scratch_shapes=scratch,
      compiler_params=_params(("parallel", "arbitrary")),
      name="attn_prompt",
  )(*args)


def _attn_sample_kernel(slopes_ref, q_ref, c0_ref, c1_ref, c2_ref, n0_ref, n1_ref, n2_ref,
                        o_ref, u0_ref, u1_ref, u2_ref, sems, *, tq):
  b = pl.program_id(0)
  cache_refs = (c0_ref, c1_ref, c2_ref)
  new_refs = (n0_ref, n1_ref, n2_ref)
  out_refs = (u0_ref, u1_ref, u2_ref)

  copies = []
  for g, (w, _) in enumerate(A_GROUPS):
    body = pltpu.make_async_copy(cache_refs[g].at[0, pl.ds(tq, w - tq)],
                                 out_refs[g].at[b, pl.ds(0, w - tq)], sems.at[2 * g])
    tail = pltpu.make_async_copy(new_refs[g].at[0], out_refs[g].at[b, pl.ds(w - tq, tq)],
                                 sems.at[2 * g + 1])
    body.start()
    tail.start()
    copies += [body, tail]

  scale = A_HEAD_DIM ** -0.5
  nrow = N_DIST * 8 + tq * 8
  ri = lax.broadcasted_iota(jnp.int32, (nrow, 128), 0)
  cj = lax.broadcasted_iota(jnp.int32, (nrow, 128), 1)
  slot = ri % 8
  key = ri // 8
  tcol = cj // A_HEADS
  hcol = cj % A_HEADS
  col_ok = cj < tq * A_HEADS
  lse_g = []
  ot_g = []
  for g, (w, d) in enumerate(A_GROUPS):
    qg = q_ref[g]
    xn = new_refs[g][0].reshape(tq * 8, 128)
    nres = min(d, tq)
    xs = []
    s = jnp.zeros((nrow, 128), F32)
    col1 = lax.broadcasted_iota(jnp.int32, (128, 1), 0) // A_HEADS
    for rho in range(nres):
      if d == 1:
        xc = cache_refs[g][0]
      else:
        xc = cache_refs[g][0, pl.ds(rho, N_DIST, stride=d), :, :]
      xc = xc.reshape(N_DIST * 8, 128)
      x = jnp.concatenate([xc, xn], axis=0).astype(BF16)
      xs.append(x)
      qm = jnp.where(col1 % d == rho, qg, 0.0).astype(BF16)
      s = s + _dot_nt(x, qm)
    qt = tcol // d
    nidx = key - N_DIST
    in_cache = key < N_DIST
    valid_c = in_cache & (key >= qt)
    dist_c = d * (N_DIST + qt - key)
    valid_n = (~in_cache) & (nidx <= tcol) & ((tcol - nidx) % d == 0)
    dist_n = tcol - nidx
    valid = (valid_c | valid_n) & (slot == hcol) & col_ok
    dist = jnp.where(in_cache, dist_c, dist_n).astype(F32)
    slope = jnp.zeros((nrow, 128), F32)
    for hh in range(A_HEADS):
      slope = jnp.where(hcol == hh, slopes_ref[g, hh], slope)
    s = jnp.where(valid, s * scale - slope * dist, NEG_INF)
    m = jnp.max(s, axis=0, keepdims=True)
    p = jnp.where(valid, jnp.exp(s - m), 0.0)
    l = jnp.sum(p, axis=0, keepdims=True)
    l = jnp.where(l > 0.0, l, 1.0)
    pn = p / l
    pv = pltpu.roll(pn, 4, 0)
    ot = jnp.zeros((128, 128), F32)
    col_row = lax.broadcasted_iota(jnp.int32, (1, 128), 1) // A_HEADS
    for rho in range(nres):
      pm = jnp.where(col_row % d == rho, pv, 0.0).astype(BF16)
      ot = ot + _dot_tn(xs[rho], pm)
    ot_g.append(ot)
    lse_g.append(m + jnp.log(l))
  mx = jnp.maximum(jnp.maximum(lse_g[0], lse_g[1]), lse_g[2])
  es = [jnp.exp(v - mx) for v in lse_g]
  den = es[0] + es[1] + es[2]
  ot = (es[0] * ot_g[0] + es[1] * ot_g[1] + es[2] * ot_g[2]) / den
  o_ref[...] = ot.T

  for c in copies:
    c.wait()


def _attn_sample(q, caches, news, slopes, tq):
  batch = q.shape[0]
  in_specs = [pl.BlockSpec(memory_space=pltpu.SMEM),
              pl.BlockSpec((None, 3, 128, 128), lambda b: (b, 0, 0, 0))]
  args = [slopes, q]
  for g, (w, d) in enumerate(A_GROUPS):
    in_specs.append(pl.BlockSpec((1, w, 8, 128), lambda b: (b, 0, 0, 0)))
    args.append(caches[g])
  for g in range(3):
    in_specs.append(pl.BlockSpec((1, tq, 8, 128), lambda b: (b, 0, 0, 0)))
    args.append(news[g])
  out_shape = [jax.ShapeDtypeStruct((batch, 128, 128), F32)]
  out_specs = [pl.BlockSpec((None, 128, 128), lambda b: (b, 0, 0))]
  for g, (w, _) in enumerate(A_GROUPS):
    out_shape.append(jax.ShapeDtypeStruct((batch, w, 8, 128), F32))
    out_specs.append(pl.BlockSpec(memory_space=pl.ANY))
  return pl.pallas_call(
      functools.partial(_attn_sample_kernel, tq=tq),
      grid=(batch,),
      in_specs=in_specs,
      out_specs=out_specs,
      out_shape=out_shape,
      scratch_shapes=[pltpu.SemaphoreType.DMA((2 * len(A_GROUPS),))],
      compiler_params=_params(("arbitrary",)),
      name="attn_sample",
  )(*args)


def _retention_kernel(x_ref, wqk_ref, wv_ref, cos_ref, sin_ref, dmat_ref, qdec_ref, kdec_ref,
                      cdec_ref, s0_ref, gn_ref, o_ref, sout_ref, state, qk_ref, v_ref, *, chunk):
  j = pl.program_id(1)

  @pl.when(j == 0)
  def _():
    state[...] = s0_ref[...]

  xb = x_ref[...].astype(BF16)
  qk_ref[...] = _dot(xb, wqk_ref[...])
  v_ref[...] = _dot(xb, wv_ref[...]).astype(BF16)

  nchunk = x_ref.shape[0] // chunk
  kscale = R_KEY_DIM ** -0.5
  for ci in range(nchunk):
    rows = pl.ds(ci * chunk, chunk)
    cosv = cos_ref[rows, :]
    sinv = sin_ref[rows, :]
    for h in range(R_HEADS):
      qh = qk_ref[rows, h * 128:(h + 1) * 128]
      kh = qk_ref[rows, R_QK + h * 128:R_QK + (h + 1) * 128]
      qrot = qh * cosv + pltpu.roll(qh, 64, 1) * sinv
      krot = (kh * cosv + pltpu.roll(kh, 64, 1) * sinv) * kscale
      qb = qrot.astype(BF16)
      vh = v_ref[rows, h * 256:(h + 1) * 256]
      att = _dot_nt(qb, krot.astype(BF16)) * dmat_ref[h]
      sh = state[h]
      o = _dot(att.astype(BF16), vh) + _dot(qb, sh.astype(BF16)) * qdec_ref[h]
      state[h] = cdec_ref[h] * sh + _dot_tn((krot * kdec_ref[h]).astype(BF16), vh)
      mu = jnp.mean(o, axis=-1, keepdims=True)
      var = jnp.mean(jnp.square(o - mu), axis=-1, keepdims=True)
      on = (o - mu) * lax.rsqrt(var + EPS) * gn_ref[:, h * 256:(h + 1) * 256]
      o_ref[rows, h * 256:(h + 1) * 256] = on

  @pl.when(j == pl.num_programs(1) - 1)
  def _():
    sout_ref[...] = state[...]


def _retention(x, w16, cos, sin, tables, s0, gn, rows_per_step, chunk):
  batch, t, _ = x.shape
  dmat, qdec, kdec, cdec = tables
  tc = rows_per_step
  const3 = lambda b, j: (0, 0, 0)
  wslab = lambda off: pl.BlockSpec((D_MODEL, R_V), lambda b, j: (0, off // R_V),
                                   pipeline_mode=pl.Buffered(1))
  return pl.pallas_call(
      functools.partial(_retention_kernel, chunk=chunk),
      grid=(batch, t // tc),
      in_specs=[pl.BlockSpec((None, tc, D_MODEL), lambda b, j: (b, j, 0)),
                wslab(_OFF[4]), wslab(_OFF[6]),
                pl.BlockSpec((tc, 128), lambda b, j: (j, 0)),
                pl.BlockSpec((tc, 128), lambda b, j: (j, 0)),
                pl.BlockSpec(dmat.shape, const3),
                pl.BlockSpec(qdec.shape, const3),
                pl.BlockSpec(kdec.shape, const3),
                pl.BlockSpec(cdec.shape, const3),
                pl.BlockSpec((None, R_HEADS, R_KEY_DIM, R_VAL_DIM), lambda b, j: (b, 0, 0, 0)),
                pl.BlockSpec((1, R_V), lambda b, j: (0, 0))],
      out_specs=[pl.BlockSpec((None, tc, R_V), lambda b, j: (b, j, 0)),
                 pl.BlockSpec((None, R_HEADS, R_KEY_DIM, R_VAL_DIM), lambda b, j: (b, 0, 0, 0))],
      out_shape=[jax.ShapeDtypeStruct((batch, t, R_V), F32),
                 jax.ShapeDtypeStruct((batch, R_HEADS, R_KEY_DIM, R_VAL_DIM), F32)],
      scratch_shapes=[pltpu.VMEM((R_HEADS, R_KEY_DIM, R_VAL_DIM), F32),
                      pltpu.VMEM((tc, 2 * R_QK), F32), pltpu.VMEM((tc, R_V), BF16)],
      compiler_params=_params(("parallel", "arbitrary")),
      name="retention",
  )(x, w16, w16, cos, sin, dmat, qdec, kdec, cdec, s0, gn)


def _retention_tables(c_true, c_pad):
  lg = jnp.log1p(-jnp.exp2(-5.0 - jnp.arange(R_HEADS, dtype=F32)))
  i = jnp.arange(c_pad, dtype=F32)
  live = i < c_true
  diff = i[:, None] - i[None, :]
  causal = (diff >= 0) & live[:, None] & live[None, :]
  dmat = jnp.where(causal[None], jnp.exp(jnp.where(causal, diff, 0.0)[None] * lg[:, None, None]), 0.0)
  qdec = jnp.where(live[None, :], jnp.exp((i[None, :] + 1.0) * lg[:, None]), 0.0)
  kdec = jnp.where(live[None, :], jnp.exp((c_true - 1.0 - i)[None, :] * lg[:, None]), 0.0)
  cdec = jnp.exp(c_true * lg)
  qdec = jnp.broadcast_to(qdec[:, :, None], (R_HEADS, c_pad, R_VAL_DIM))
  kdec = jnp.broadcast_to(kdec[:, :, None], (R_HEADS, c_pad, R_KEY_DIM))
  cdec = jnp.broadcast_to(cdec[:, None, None], (R_HEADS, R_KEY_DIM, R_VAL_DIM))
  return dmat, qdec, kdec, cdec


def _rope_tables(pos):
  half = R_KEY_DIM // 2
  inv = 1.0 / (ROPE_BASE ** jnp.linspace(0.0, 1.0, half, dtype=F32))
  ang = pos.astype(F32)[:, None] * inv[None, :]
  cos = jnp.cos(ang)
  sin = jnp.sin(ang)
  return jnp.concatenate([cos, cos], axis=-1), jnp.concatenate([-sin, sin], axis=-1)


def _output_kernel(x_ref, oa_ref, or_ref, wza_ref, wzr_ref, wga_ref, wgb_ref, wpa_ref, wpb_ref,
                   wo_ref, lng_ref, lnb_ref, y_ref):
  x = x_ref[...]
  xb = x.astype(BF16)
  za = _dot(xb, wza_ref[...])
  ya = _dot((jax.nn.silu(za) * oa_ref[...]).astype(BF16), wpa_ref[...])
  zr = _dot(xb, wzr_ref[...])
  yb = _dot((jax.nn.silu(zr) * or_ref[...]).astype(BF16), wpb_ref[...])
  ga = _dot(xb, wga_ref[...])
  gb = _dot(xb, wgb_ref[...])
  mix = jax.nn.sigmoid(ga) * ya + jax.nn.sigmoid(gb) * yb
  out = _dot(mix.astype(BF16), wo_ref[...])
  zz = DEEPNORM_ALPHA * x + out
  zm = jnp.mean(zz, axis=-1, keepdims=True)
  zv = jnp.mean(jnp.square(zz - zm), axis=-1, keepdims=True)
  y_ref[...] = (zz - zm) * lax.rsqrt(zv + EPS) * lng_ref[...] + lnb_ref[...]


def _output(x, oa, orr, w16, wpa, wpb, wo, lng, lnb, tm):
  m = x.shape[0]
  row = lambda w: pl.BlockSpec((tm, w), lambda i: (i, 0))
  full = lambda a: pl.BlockSpec(a.shape, lambda i: (0, 0), pipeline_mode=pl.Buffered(1))
  wcol = lambda off, width: pl.BlockSpec((D_MODEL, width), lambda i: (0, off // width),
                                         pipeline_mode=pl.Buffered(1))
  in_specs = [row(D_MODEL), row(A_OUT), row(R_V),
              wcol(_OFF[3], A_OUT), wcol(_OFF[7], R_V), wcol(_OFF[8], D_MODEL),
              wcol(_OFF[9], D_MODEL),
              full(wpa), full(wpb), full(wo), full(lng), full(lnb)]
  args = [x, oa, orr, w16, w16, w16, w16, wpa, wpb, wo, lng, lnb]
  return pl.pallas_call(
      _output_kernel,
      grid=(m // tm,),
      in_specs=in_specs,
      out_specs=row(D_MODEL),
      out_shape=jax.ShapeDtypeStruct((m, D_MODEL), F32),
      compiler_params=_params(("parallel",)),
      name="gates_output",
  )(*args)


def _alibi_slopes():
  n = len(A_GROUPS) * A_HEADS
  return jnp.exp2(-8.0 * jnp.arange(1, n + 1, dtype=F32) / n).reshape(len(A_GROUPS), A_HEADS)


def kernel(x_prompt, x_sample, cache_kv_w128, cache_kv_w512, cache_kv_w2048, state_ret,
           w_in, w_pa, w_pb, w_o, gn_g, ln_g, ln_b):
  bp, tp, _ = x_prompt.shape
  bs, ts, _ = x_sample.shape
  slopes = _alibi_slopes()
  w16 = w_in.astype(BF16)
  wpa = w_pa.astype(BF16)
  wpb = w_pb.astype(BF16)
  wo = w_o.astype(BF16)
  gn = gn_g.reshape(1, R_V)
  lng = ln_g.reshape(1, D_MODEL)
  lnb = ln_b.reshape(1, D_MODEL)

  xp2 = x_prompt.reshape(bp * tp, D_MODEL)
  qkvs, kv_p = [], []
  for g, (w, _) in enumerate(A_GROUPS):
    qkv, tail = _proj_group(xp2, w16, g, bp, tp)
    qkvs.append(qkv)
    kv_p.append(tail.reshape(bp, min(w, tp), 2, A_HEADS, A_HEAD_DIM))
  oa_p = _attn_prompt(qkvs, slopes, bp, tp).reshape(bp * tp, A_OUT)

  cos_p, sin_p = _rope_tables(jnp.arange(tp, dtype=jnp.int32))
  or_p, s_p = _retention(
      x_prompt, w16, cos_p, sin_p, _retention_tables(R_CHUNK, R_CHUNK),
      jnp.zeros((bp, R_HEADS, R_KEY_DIM, R_VAL_DIM), F32), gn, 512, R_CHUNK)
  y_p = _output(xp2, oa_p, or_p.reshape(bp * tp, R_V), w16, wpa, wpb, wo, lng, lnb, 512)
  xs2 = x_sample.reshape(bs * ts, D_MODEL)
  hs = _matmul(xs2, w16, 0, _OFF[3], F32, bs * ts, A_GW)
  caches = (cache_kv_w128, cache_kv_w512, cache_kv_w2048)
  q_s, news = [], []
  for g in range(3):
    col = lambda part: hs[:, _OFF[part] + g * A_GW:_OFF[part] + (g + 1) * A_GW]
    qg = col(0).reshape(bs, ts * A_HEADS, A_HEAD_DIM)
    q_s.append(jnp.pad(qg, ((0, 0), (0, 128 - ts * A_HEADS), (0, 0))))
    news.append(jnp.concatenate([col(1), col(2)], axis=1).reshape(bs, ts, 8, 128))
  q_s = jnp.stack(q_s, axis=1)
  caches8 = [c.reshape(bs, c.shape[1], 8, 128) for c in caches]
  oa_s, u0, u1, u2 = _attn_sample(q_s, caches8, news, slopes, ts)
  oa_s = oa_s[:, :ts * A_HEADS].reshape(bs * ts, A_OUT)
  kv_s = [u.reshape(bs, u.shape[1], 2, A_HEADS, A_HEAD_DIM) for u in (u0, u1, u2)]

  cpad = 16
  xs_pad = jnp.pad(x_sample, ((0, 0), (0, cpad - ts), (0, 0)))
  cos_s, sin_s = _rope_tables(PAST_LEN + jnp.arange(ts, dtype=jnp.int32))
  padtab = lambda a: jnp.pad(a, ((0, cpad - ts), (0, 0)))
  or_s, s_s = _retention(
      xs_pad, w16, padtab(cos_s), padtab(sin_s),
      _retention_tables(ts, cpad), state_ret.astype(F32), gn, cpad, cpad)
  or_s = or_s[:, :ts].reshape(bs * ts, R_V)
  y_s = _output(xs2, oa_s, or_s, w16, wpa, wpb, wo, lng, lnb, 256)

  return (y_p.reshape(bp, tp, D_MODEL), y_s.reshape(bs, ts, D_MODEL),
          kv_p[0], kv_p[1], kv_p[2], s_p,
          kv_s[0], kv_s[1], kv_s[2], s_s)
```

```python
import functools

import numpy as np
import jax
import jax.numpy as jnp
from jax import lax
from jax.experimental import pallas as pl
from jax.experimental.pallas import tpu as pltpu

D_MODEL = 1024
PAST_LEN = 16384
A_GROUPS = ((128, 1), (512, 4), (2048, 16))
A_HEADS = 4
A_HEAD_DIM = 128
A_GW = A_HEADS * A_HEAD_DIM
A_QKV = len(A_GROUPS) * A_GW
A_OUT = A_GW
N_DIST = 128
R_HEADS = 4
R_KEY_DIM = 128
R_VAL_DIM = 256
R_QK = R_HEADS * R_KEY_DIM
R_V = R_HEADS * R_VAL_DIM
R_CHUNK = 256
ROPE_BASE = 10000.0
EPS = 1e-5
NEG_INF = -1e30
DEEPNORM_ALPHA = 2.0 ** 0.25
_SPLIT = (A_QKV, A_QKV, A_QKV, A_OUT, R_QK, R_QK, R_V, R_V, D_MODEL, D_MODEL)
_OFF = tuple(int(v) for v in np.concatenate([[0], np.cumsum(_SPLIT)]))

BF16 = jnp.bfloat16
F32 = jnp.float32
VMEM_LIMIT = 56 * 1024 * 1024

_NT = (((1,), (1,)), ((), ()))
_TN = (((0,), (0,)), ((), ()))


def _dot(a, b):
  return jnp.dot(a, b, preferred_element_type=F32)


def _dot_nt(a, b):
  return lax.dot_general(a, b, _NT, preferred_element_type=F32)


def _dot_tn(a, b):
  return lax.dot_general(a, b, _TN, preferred_element_type=F32)


def _params(sem):
  return pltpu.CompilerParams(dimension_semantics=sem, vmem_limit_bytes=VMEM_LIMIT)


def _mm_kernel(x_ref, w_ref, o_ref):
  o_ref[...] = _dot(x_ref[...].astype(BF16), w_ref[...]).astype(o_ref.dtype)


def _matmul(x, w, col0, ncols, out_dtype, tm, tn):
  m, k = x.shape
  c0 = col0 // tn
  return pl.pallas_call(
      _mm_kernel,
      grid=(m // tm, ncols // tn),
      in_specs=[pl.BlockSpec((tm, k), lambda i, j: (i, 0)),
                pl.BlockSpec((k, tn), lambda i, j: (0, c0 + j))],
      out_specs=pl.BlockSpec((tm, tn), lambda i, j: (i, j)),
      out_shape=jax.ShapeDtypeStruct((m, ncols), out_dtype),
      compiler_params=_params(("parallel", "arbitrary")),
      name="proj_matmul",
  )(x, w)


PROJ_TM = 1024


def _proj_group_kernel(x_ref, wq_ref, wk_ref, wv_ref, o_ref, tail_ref, *scratch,
                       d, first_tail, tail_rows):
  i = pl.program_id(1)
  tm = x_ref.shape[0]
  n = tm // d
  x = x_ref[...]
  xb = x.astype(BF16)
  if d == 1:
    perm = xb
  else:
    xs_ref, xp_ref = scratch
    for c in range(D_MODEL // 128):
      xs_ref[c] = x[:, c * 128:(c + 1) * 128]
    for r in range(d):
      for c in range(D_MODEL // 128):
        xp_ref[r * n:(r + 1) * n, c * 128:(c + 1) * 128] = (
            xs_ref[c, pl.ds(r, n, stride=d), :].astype(BF16))
    perm = xp_ref[...]
  nat = []
  for part, w_ref in enumerate((wq_ref, wk_ref, wv_ref)):
    res = _dot(perm, w_ref[...])
    nat.append(res)
    for r in range(d):
      for h in range(A_HEADS):
        o_ref[r, part * A_HEADS + h] = res[r * n:(r + 1) * n, h * 128:(h + 1) * 128].astype(BF16)

  @pl.when(i >= first_tail)
  def _():
    if d == 1:
      kv = nat[1:]
    else:
      kv = [_dot(xb, wk_ref[...]), _dot(xb, wv_ref[...])]
    for c in range(2 * A_HEADS):
      src = kv[c // A_HEADS][tm - tail_rows:, (c % A_HEADS) * 128:(c % A_HEADS + 1) * 128]
      tail_ref[pl.ds(c, tail_rows, stride=2 * A_HEADS), :] = src


def _proj_group(x2, w16, g, batch, seq):
  w, d = A_GROUPS[g]
  w = min(w, seq)
  tm = PROJ_TM
  nblk = seq // tm
  tail_rows = min(tm, w)
  first_tail = nblk - max(w // tm, 1)
  ls = seq // d
  wspec = lambda part: pl.BlockSpec((D_MODEL, A_GW), lambda b, i: (0, 3 * part + g))
  scratch = []
  if d > 1:
    scratch = [pltpu.VMEM((D_MODEL // 128, tm, 128), F32), pltpu.VMEM((tm, D_MODEL), BF16)]
  return pl.pallas_call(
      functools.partial(_proj_group_kernel, d=d, first_tail=first_tail, tail_rows=tail_rows),
      grid=(batch, nblk),
      in_specs=[pl.BlockSpec((tm, D_MODEL), lambda b, i: (b * nblk + i, 0)),
                wspec(0), wspec(1), wspec(2)],
      out_specs=[pl.BlockSpec((None, d, 3 * A_HEADS, tm // d, 128), lambda b, i: (b, 0, 0, i, 0)),
                 pl.BlockSpec((None, tail_rows * 8, 128),
                              lambda b, i: (b, jnp.maximum(i - first_tail, 0), 0))],
      out_shape=[jax.ShapeDtypeStruct((batch, d, 3 * A_HEADS, ls, 128), BF16),
                 jax.ShapeDtypeStruct((batch, w * 8, 128), F32)],
      scratch_shapes=scratch,
      compiler_params=_params(("parallel", "arbitrary")),
      name="proj_group",
  )(x2, w16, w16, w16)


ATT_BQ = 256
LOG2E = 1.4426950408889634


def _attn_prompt_kernel(slopes_ref, *refs, seq):
  qkv = refs[:9]
  o_ref = refs[9]
  oacc = refs[10:13]
  lacc = refs[13:16]
  bias_ref, s_buf, p_buf, l_buf = refs[16:20]
  h = pl.program_id(0)
  scale = A_HEAD_DIM ** -0.5

  @pl.when(pl.program_id(1) == 0)
  def _():
    for g, (_, d) in enumerate(A_GROUPS):
      ls = seq // d
      bq = min(ATT_BQ, ls)
      klen = min(bq + N_DIST, ls)
      slope = slopes_ref[g, h]
      qi = lax.broadcasted_iota(jnp.int32, (bq, klen), 0)
      kj = lax.broadcasted_iota(jnp.int32, (bq, klen), 1)
      for which, off in enumerate((0, N_DIST)):
        delta = qi - kj + off
        valid = (delta >= 0) & (delta <= N_DIST)
        dist = (delta * d).astype(F32)
        bias_ref[2 * g + which, :bq, :klen] = jnp.where(valid, -(slope * LOG2E) * dist, NEG_INF)

  for g, (_, d) in enumerate(A_GROUPS):
    q_ref, k_ref, v_ref = qkv[3 * g:3 * g + 3]
    ls = seq // d
    bq = min(ATT_BQ, ls)
    nqb = ls // bq
    klen = min(bq + N_DIST, ls)

    def coords(idx, nqb=nqb, bq=bq):
      r = idx // nqb
      qb = idx % nqb
      q0 = pl.multiple_of(qb * bq, bq)
      k0 = pl.multiple_of(jnp.maximum(qb * bq - N_DIST, 0), N_DIST)
      return r, qb, q0, k0

    def put_rows(ref, idx, val, d=d, bq=bq):
      r, _, q0, _ = coords(idx)
      if d == 1:
        ref[pl.ds(q0, bq), :] = val
      else:
        ref[pl.ds(r + q0 * d, bq, stride=d), :] = val

    def scores(idx, slot, bq=bq, klen=klen, q_ref=q_ref, k_ref=k_ref):
      r, _, q0, k0 = coords(idx)
      s_buf[slot, :bq, :klen] = _dot_nt(q_ref[r, pl.ds(q0, bq), :], k_ref[r, pl.ds(k0, klen), :])

    def softmax(idx, slot, g=g, bq=bq, klen=klen):
      _, qb, _, _ = coords(idx)
      s = (s_buf[slot, :bq, :klen] * (scale * LOG2E)
           + bias_ref[2 * g + jnp.minimum(qb, 1), :bq, :klen])
      m = jnp.max(s, axis=-1, keepdims=True)
      p = jnp.exp2(s - m)
      l = jnp.sum(p, axis=-1, keepdims=True)
      p_buf[slot, :bq, :klen] = p.astype(BF16)
      l_buf[slot, :bq, :] = jnp.broadcast_to(l, (bq, 128))
      put_rows(lacc[g], idx, jnp.broadcast_to(m + jnp.log2(l), (bq, 128)))

    def values(idx, slot, g=g, bq=bq, klen=klen, v_ref=v_ref):
      r, _, _, k0 = coords(idx)
      o = _dot(p_buf[slot, :bq, :klen], v_ref[r, pl.ds(k0, klen), :]) / l_buf[slot, :bq, :]
      put_rows(oacc[g], idx, o)

    nblk = d * nqb
    scores(0, 0)
    scores(1, 1)
    softmax(0, 0)
    softmax(1, 1)
    scores(2, 2)
    scores(3, 3)

    def steady(j, carry):
      i = 4 + 4 * j
      values(i - 4, 0)
      values(i - 3, 1)
      softmax(i - 2, 2)
      softmax(i - 1, 3)
      scores(i, 0)
      scores(i + 1, 1)
      values(i - 2, 2)
      values(i - 1, 3)
      softmax(i, 0)
      softmax(i + 1, 1)
      scores(i + 2, 2)
      scores(i + 3, 3)
      return carry

    lax.fori_loop(0, (nblk - 4) // 4, steady, 0)
    values(nblk - 4, 0)
    values(nblk - 3, 1)
    softmax(nblk - 2, 2)
    softmax(nblk - 1, 3)
    values(nblk - 2, 2)
    values(nblk - 1, 3)

  mrows = 512

  def merge(i, carry):
    r0 = pl.multiple_of(i * mrows, mrows)
    l0 = lacc[0][pl.ds(r0, mrows), :]
    l1 = lacc[1][pl.ds(r0, mrows), :]
    l2 = lacc[2][pl.ds(r0, mrows), :]
    mx = jnp.maximum(jnp.maximum(l0, l1), l2)
    e0 = jnp.exp2(l0 - mx)
    e1 = jnp.exp2(l1 - mx)
    e2 = jnp.exp2(l2 - mx)
    den = e0 + e1 + e2
    acc = (e0 * oacc[0][pl.ds(r0, mrows), :] + e1 * oacc[1][pl.ds(r0, mrows), :]
           + e2 * oacc[2][pl.ds(r0, mrows), :])
    o_ref[pl.ds(r0, mrows), :] = acc / den
    return carry

  lax.fori_loop(0, seq // mrows, merge, 0)


def _attn_prompt(qkvs, slopes, batch, seq):
  in_specs = [pl.BlockSpec(memory_space=pltpu.SMEM)]
  args = [slopes]
  for g, (_, d) in enumerate(A_GROUPS):
    ls = seq // d
    for part in range(3):
      in_specs.append(pl.BlockSpec(
          (None, d, None, ls, 128),
          lambda h, b, part=part: (b, 0, part * A_HEADS + h, 0, 0)))
      args.append(qkvs[g])
  scratch = ([pltpu.VMEM((seq, 128), F32) for _ in range(6)]
             + [pltpu.VMEM((6, ATT_BQ, ATT_BQ + N_DIST), F32),
                pltpu.VMEM((4, ATT_BQ, ATT_BQ + N_DIST), F32),
                pltpu.VMEM((4, ATT_BQ, ATT_BQ + N_DIST), BF16),
                pltpu.VMEM((4, ATT_BQ, 128), F32)])
  return pl.pallas_call(
      functools.partial(_attn_prompt_kernel, seq=seq),
      grid=(A_HEADS, batch),
      in_specs=in_specs,
      out_specs=pl.BlockSpec((None, seq, 128), lambda h, b: (b, 0, h)),
      out_shape=jax.ShapeDtypeStruct((batch, seq, A_OUT), F32),
      scratch_shapes=scratch,
      compiler_params=_params(("parallel", "arbitrary")),
      name="attn_prompt",
  )(*args)


def _attn_sample_kernel(slopes_ref, q_ref, c0_ref, c1_ref, c2_ref, n0_ref, n1_ref, n2_ref,
                        o_ref, u0_ref, u1_ref, u2_ref, sems, *, tq):
  b = pl.program_id(0)
  cache_refs = (c0_ref, c1_ref, c2_ref)
  new_refs = (n0_ref, n1_ref, n2_ref)
  out_refs = (u0_ref, u1_ref, u2_ref)

  copies = []
  for g, (w, _) in enumerate(A_GROUPS):
    body = pltpu.make_async_copy(cache_refs[g].at[0, pl.ds(tq, w - tq)],
                                 out_refs[g].at[b, pl.ds(0, w - tq)], sems.at[2 * g])
    tail = pltpu.make_async_copy(new_refs[g].at[0], out_refs[g].at[b, pl.ds(w - tq, tq)],
                                 sems.at[2 * g + 1])
    body.start()
    tail.start()
    copies += [body, tail]

  scale = A_HEAD_DIM ** -0.5
  nrow = N_DIST * 8 + tq * 8
  ri = lax.broadcasted_iota(jnp.int32, (nrow, 128), 0)
  cj = lax.broadcasted_iota(jnp.int32, (nrow, 128), 1)
  slot = ri % 8
  key = ri // 8
  tcol = cj // A_HEADS
  hcol = cj % A_HEADS
  col_ok = cj < tq * A_HEADS
  lse_g = []
  ot_g = []
  for g, (w, d) in enumerate(A_GROUPS):
    qg = q_ref[g]
    xn = new_refs[g][0].reshape(tq * 8, 128)
    nres = min(d, tq)
    xs = []
    s = jnp.zeros((nrow, 128), F32)
    col1 = lax.broadcasted_iota(jnp.int32, (128, 1), 0) // A_HEADS
    for rho in range(nres):
      if d == 1:
        xc = cache_refs[g][0]
      else:
        xc = cache_refs[g][0, pl.ds(rho, N_DIST, stride=d), :, :]
      xc = xc.reshape(N_DIST * 8, 128)
      x = jnp.concatenate([xc, xn], axis=0).astype(BF16)
      xs.append(x)
      qm = jnp.where(col1 % d == rho, qg, 0.0).astype(BF16)
      s = s + _dot_nt(x, qm)
    qt = tcol // d
    nidx = key - N_DIST
    in_cache = key < N_DIST
    valid_c = in_cache & (key >= qt)
    dist_c = d * (N_DIST + qt - key)
    valid_n = (~in_cache) & (nidx <= tcol) & ((tcol - nidx) % d == 0)
    dist_n = tcol - nidx
    valid = (valid_c | valid_n) & (slot == hcol) & col_ok
    dist = jnp.where(in_cache, dist_c, dist_n).astype(F32)
    slope = jnp.zeros((nrow, 128), F32)
    for hh in range(A_HEADS):
      slope = jnp.where(hcol == hh, slopes_ref[g, hh], slope)
    s = jnp.where(valid, s * scale - slope * dist, NEG_INF)
    m = jnp.max(s, axis=0, keepdims=True)
    p = jnp.where(valid, jnp.exp(s - m), 0.0)
    l = jnp.sum(p, axis=0, keepdims=True)
    l = jnp.where(l > 0.0, l, 1.0)
    pn = p / l
    pv = pltpu.roll(pn, 4, 0)
    ot = jnp.zeros((128, 128), F32)
    col_row = lax.broadcasted_iota(jnp.int32, (1, 128), 1) // A_HEADS
    for rho in range(nres):
      pm = jnp.where(col_row % d == rho, pv, 0.0).astype(BF16)
      ot = ot + _dot_tn(xs[rho], pm)
    ot_g.append(ot)
    lse_g.append(m + jnp.log(l))
  mx = jnp.maximum(jnp.maximum(lse_g[0], lse_g[1]), lse_g[2])
  es = [jnp.exp(v - mx) for v in lse_g]
  den = es[0] + es[1] + es[2]
  ot = (es[0] * ot_g[0] + es[1] * ot_g[1] + es[2] * ot_g[2]) / den
  o_ref[...] = ot.T

  for c in copies:
    c.wait()


def _attn_sample(q, caches, news, slopes, tq):
  batch = q.shape[0]
  in_specs = [pl.BlockSpec(memory_space=pltpu.SMEM),
              pl.BlockSpec((None, 3, 128, 128), lambda b: (b, 0, 0, 0))]
  args = [slopes, q]
  for g, (w, d) in enumerate(A_GROUPS):
    in_specs.append(pl.BlockSpec((1, w, 8, 128), lambda b: (b, 0, 0, 0)))
    args.append(caches[g])
  for g in range(3):
    in_specs.append(pl.BlockSpec((1, tq, 8, 128), lambda b: (b, 0, 0, 0)))
    args.append(news[g])
  out_shape = [jax.ShapeDtypeStruct((batch, 128, 128), F32)]
  out_specs = [pl.BlockSpec((None, 128, 128), lambda b: (b, 0, 0))]
  for g, (w, _) in enumerate(A_GROUPS):
    out_shape.append(jax.ShapeDtypeStruct((batch, w, 8, 128), F32))
    out_specs.append(pl.BlockSpec(memory_space=pl.ANY))
  return pl.pallas_call(
      functools.partial(_attn_sample_kernel, tq=tq),
      grid=(batch,),
      in_specs=in_specs,
      out_specs=out_specs,
      out_shape=out_shape,
      scratch_shapes=[pltpu.SemaphoreType.DMA((2 * len(A_GROUPS),))],
      compiler_params=_params(("arbitrary",)),
      name="attn_sample",
  )(*args)


RET_SUB = 1024


def _retention_kernel(x_ref, wqk_ref, wv_ref, cos_ref, sin_ref, dmat_ref, qdec_ref, kdec_ref,
                      cdec_ref, s0_ref, gn_ref, o_ref, sout_ref, state, qk_ref, v_ref, *, chunk):
  j = pl.program_id(1)

  @pl.when(j == 0)
  def _():
    state[...] = s0_ref[...]

  tc = x_ref.shape[0]
  sub = min(RET_SUB, tc)
  kscale = R_KEY_DIM ** -0.5
  for ci in range(tc // chunk):
    rows = pl.ds(ci * chunk, chunk)
    if (ci * chunk) % sub == 0:
      sub_rows = pl.ds(ci * chunk, sub)
      xb = x_ref[sub_rows, :].astype(BF16)
      qk_ref[sub_rows, :] = _dot(xb, wqk_ref[...])
      v_ref[sub_rows, :] = _dot(xb, wv_ref[...]).astype(BF16)
    cosv = cos_ref[rows, :]
    sinv = sin_ref[rows, :]
    for h in range(R_HEADS):
      qh = qk_ref[rows, h * 128:(h + 1) * 128]
      kh = qk_ref[rows, R_QK + h * 128:R_QK + (h + 1) * 128]
      qrot = qh * cosv + pltpu.roll(qh, 64, 1) * sinv
      krot = (kh * cosv + pltpu.roll(kh, 64, 1) * sinv) * kscale
      qb = qrot.astype(BF16)
      vh = v_ref[rows, h * 256:(h + 1) * 256]
      att = _dot_nt(qb, krot.astype(BF16)) * dmat_ref[h]
      sh = state[h]
      o = _dot(att.astype(BF16), vh) + _dot(qb, sh.astype(BF16)) * qdec_ref[h]
      state[h] = cdec_ref[h] * sh + _dot_tn((krot * kdec_ref[h]).astype(BF16), vh)
      mu = jnp.mean(o, axis=-1, keepdims=True)
      var = jnp.mean(jnp.square(o - mu), axis=-1, keepdims=True)
      on = (o - mu) * lax.rsqrt(var + EPS) * gn_ref[:, h * 256:(h + 1) * 256]
      o_ref[rows, h * 256:(h + 1) * 256] = on

  @pl.when(j == pl.num_programs(1) - 1)
  def _():
    sout_ref[...] = state[...]


def _retention(x, w16, cos, sin, tables, s0, gn, rows_per_step, chunk):
  batch, t, _ = x.shape
  dmat, qdec, kdec, cdec = tables
  tc = rows_per_step
  const3 = lambda b, j: (0, 0, 0)
  wslab = lambda off: pl.BlockSpec((D_MODEL, R_V), lambda b, j: (0, off // R_V),
                                   pipeline_mode=pl.Buffered(1))
  return pl.pallas_call(
      functools.partial(_retention_kernel, chunk=chunk),
      grid=(batch, t // tc),
      in_specs=[pl.BlockSpec((None, tc, D_MODEL), lambda b, j: (b, j, 0)),
                wslab(_OFF[4]), wslab(_OFF[6]),
                pl.BlockSpec((tc, 128), lambda b, j: (j, 0)),
                pl.BlockSpec((tc, 128), lambda b, j: (j, 0)),
                pl.BlockSpec(dmat.shape, const3),
                pl.BlockSpec(qdec.shape, const3),
                pl.BlockSpec(kdec.shape, const3),
                pl.BlockSpec(cdec.shape, const3),
                pl.BlockSpec((None, R_HEADS, R_KEY_DIM, R_VAL_DIM), lambda b, j: (b, 0, 0, 0)),
                pl.BlockSpec((1, R_V), lambda b, j: (0, 0))],
      out_specs=[pl.BlockSpec((None, tc, R_V), lambda b, j: (b, j, 0)),
                 pl.BlockSpec((None, R_HEADS, R_KEY_DIM, R_VAL_DIM), lambda b, j: (b, 0, 0, 0))],
      out_shape=[jax.ShapeDtypeStruct((batch, t, R_V), F32),
                 jax.ShapeDtypeStruct((batch, R_HEADS, R_KEY_DIM, R_VAL_DIM), F32)],
      scratch_shapes=[pltpu.VMEM((R_HEADS, R_KEY_DIM, R_VAL_DIM), F32),
                      pltpu.VMEM((tc, 2 * R_QK), F32), pltpu.VMEM((tc, R_V), BF16)],
      compiler_params=_params(("parallel", "arbitrary")),
      name="retention",
  )(x, w16, w16, cos, sin, dmat, qdec, kdec, cdec, s0, gn)


def _retention_tables(c_true, c_pad):
  lg = jnp.log1p(-jnp.exp2(-5.0 - jnp.arange(R_HEADS, dtype=F32)))
  i = jnp.arange(c_pad, dtype=F32)
  live = i < c_true
  diff = i[:, None] - i[None, :]
  causal = (diff >= 0) & live[:, None] & live[None, :]
  dmat = jnp.where(causal[None], jnp.exp(jnp.where(causal, diff, 0.0)[None] * lg[:, None, None]), 0.0)
  qdec = jnp.where(live[None, :], jnp.exp((i[None, :] + 1.0) * lg[:, None]), 0.0)
  kdec = jnp.where(live[None, :], jnp.exp((c_true - 1.0 - i)[None, :] * lg[:, None]), 0.0)
  cdec = jnp.exp(c_true * lg)
  qdec = jnp.broadcast_to(qdec[:, :, None], (R_HEADS, c_pad, R_VAL_DIM))
  kdec = jnp.broadcast_to(kdec[:, :, None], (R_HEADS, c_pad, R_KEY_DIM))
  cdec = jnp.broadcast_to(cdec[:, None, None], (R_HEADS, R_KEY_DIM, R_VAL_DIM))
  return dmat, qdec, kdec, cdec


def _rope_tables(pos):
  half = R_KEY_DIM // 2
  inv = 1.0 / (ROPE_BASE ** jnp.linspace(0.0, 1.0, half, dtype=F32))
  ang = pos.astype(F32)[:, None] * inv[None, :]
  cos = jnp.cos(ang)
  sin = jnp.sin(ang)
  return jnp.concatenate([cos, cos], axis=-1), jnp.concatenate([-sin, sin], axis=-1)


def _output_kernel(x_ref, oa_ref, or_ref, wza_ref, wzr_ref, wga_ref, wgb_ref, wpa_ref, wpb_ref,
                   wo_ref, lng_ref, lnb_ref, y_ref):
  tm = x_ref.shape[0]
  nsub = 2 if tm % 32 == 0 else 1
  sub = tm // nsub
  for si in range(nsub):
    rows = pl.ds(si * sub, sub)
    x = x_ref[rows, :]
    xb = x.astype(BF16)
    za = _dot(xb, wza_ref[...])
    ya = _dot((jax.nn.silu(za) * oa_ref[rows, :]).astype(BF16), wpa_ref[...])
    zr = _dot(xb, wzr_ref[...])
    yb = _dot((jax.nn.silu(zr) * or_ref[rows, :]).astype(BF16), wpb_ref[...])
    ga = _dot(xb, wga_ref[...])
    gb = _dot(xb, wgb_ref[...])
    mix = jax.nn.sigmoid(ga) * ya + jax.nn.sigmoid(gb) * yb
    out = _dot(mix.astype(BF16), wo_ref[...])
    zz = DEEPNORM_ALPHA * x + out
    zm = jnp.mean(zz, axis=-1, keepdims=True)
    zv = jnp.mean(jnp.square(zz - zm), axis=-1, keepdims=True)
    y_ref[rows, :] = (zz - zm) * lax.rsqrt(zv + EPS) * lng_ref[...] + lnb_ref[...]


def _output(x, oa, orr, w16, wpa, wpb, wo, lng, lnb, tm):
  m = x.shape[0]
  row = lambda w: pl.BlockSpec((tm, w), lambda i: (i, 0))
  full = lambda a: pl.BlockSpec(a.shape, lambda i: (0, 0), pipeline_mode=pl.Buffered(1))
  wcol = lambda off, width: pl.BlockSpec((D_MODEL, width), lambda i: (0, off // width),
                                         pipeline_mode=pl.Buffered(1))
  in_specs = [row(D_MODEL), row(A_OUT), row(R_V),
              wcol(_OFF[3], A_OUT), wcol(_OFF[7], R_V), wcol(_OFF[8], D_MODEL),
              wcol(_OFF[9], D_MODEL),
              full(wpa), full(wpb), full(wo), full(lng), full(lnb)]
  args = [x, oa, orr, w16, w16, w16, w16, wpa, wpb, wo, lng, lnb]
  return pl.pallas_call(
      _output_kernel,
      grid=(m // tm,),
      in_specs=in_specs,
      out_specs=row(D_MODEL),
      out_shape=jax.ShapeDtypeStruct((m, D_MODEL), F32),
      compiler_params=_params(("parallel",)),
      name="gates_output",
  )(*args)


def _alibi_slopes():
  n = len(A_GROUPS) * A_HEADS
  return jnp.exp2(-8.0 * jnp.arange(1, n + 1, dtype=F32) / n).reshape(len(A_GROUPS), A_HEADS)


def kernel(x_prompt, x_sample, cache_kv_w128, cache_kv_w512, cache_kv_w2048, state_ret,
           w_in, w_pa, w_pb, w_o, gn_g, ln_g, ln_b):
  bp, tp, _ = x_prompt.shape
  bs, ts, _ = x_sample.shape
  slopes = _alibi_slopes()
  w16 = w_in.astype(BF16)
  wpa = w_pa.astype(BF16)
  wpb = w_pb.astype(BF16)
  wo = w_o.astype(BF16)
  gn = gn_g.reshape(1, R_V)
  lng = ln_g.reshape(1, D_MODEL)
  lnb = ln_b.reshape(1, D_MODEL)

  xp2 = x_prompt.reshape(bp * tp, D_MODEL)
  qkvs, kv_p = [], []
  for g, (w, _) in enumerate(A_GROUPS):
    qkv, tail = _proj_group(xp2, w16, g, bp, tp)
    qkvs.append(qkv)
    kv_p.append(tail.reshape(bp, min(w, tp), 2, A_HEADS, A_HEAD_DIM))
  oa_p = _attn_prompt(qkvs, slopes, bp, tp).reshape(bp * tp, A_OUT)

  cos_p, sin_p = _rope_tables(jnp.arange(tp, dtype=jnp.int32))
  or_p, s_p = _retention(
      x_prompt, w16, cos_p, sin_p, _retention_tables(R_CHUNK, R_CHUNK),
      jnp.zeros((bp, R_HEADS, R_KEY_DIM, R_VAL_DIM), F32), gn, 1024, R_CHUNK)
  y_p = _output(xp2, oa_p, or_p.reshape(bp * tp, R_V), w16, wpa, wpb, wo, lng, lnb, 512)
  xs2 = x_sample.reshape(bs * ts, D_MODEL)
  hs = _matmul(xs2, w16, 0, _OFF[3], F32, bs * ts, A_GW)
  caches = (cache_kv_w128, cache_kv_w512, cache_kv_w2048)
  q_s, news = [], []
  for g in range(3):
    col = lambda part: hs[:, _OFF[part] + g * A_GW:_OFF[part] + (g + 1) * A_GW]
    qg = col(0).reshape(bs, ts * A_HEADS, A_HEAD_DIM)
    q_s.append(jnp.pad(qg, ((0, 0), (0, 128 - ts * A_HEADS), (0, 0))))
    news.append(jnp.concatenate([col(1), col(2)], axis=1).reshape(bs, ts, 8, 128))
  q_s = jnp.stack(q_s, axis=1)
  caches8 = [c.reshape(bs, c.shape[1], 8, 128) for c in caches]
  oa_s, u0, u1, u2 = _attn_sample(q_s, caches8, news, slopes, ts)
  oa_s = oa_s[:, :ts * A_HEADS].reshape(bs * ts, A_OUT)
  kv_s = [u.reshape(bs, u.shape[1], 2, A_HEADS, A_HEAD_DIM) for u in (u0, u1, u2)]

  cpad = 16
  xs_pad = jnp.pad(x_sample, ((0, 0), (0, cpad - ts), (0, 0)))
  cos_s, sin_s = _rope_tables(PAST_LEN + jnp.arange(ts, dtype=jnp.int32))
  padtab = lambda a: jnp.pad(a, ((0, cpad - ts), (0, 0)))
  or_s, s_s = _retention(
      xs_pad, w16, padtab(cos_s), padtab(sin_s),
      _retention_tables(ts, cpad), state_ret.astype(F32), gn, cpad, cpad)
  or_s = or_s[:, :ts].reshape(bs * ts, R_V)
  y_s = _output(xs2, oa_s, or_s, w16, wpa, wpb, wo, lng, lnb, 256)

  return (y_p.reshape(bp, tp, D_MODEL), y_s.reshape(bs, ts, D_MODEL),
          kv_p[0], kv_p[1], kv_p[2], s_p,
          kv_s[0], kv_s[1], kv_s[2], s_s)
```

```python
import functools

import numpy as np
import jax
import jax.numpy as jnp
from jax import lax
from jax.experimental import pallas as pl
from jax.experimental.pallas import tpu as pltpu

D_MODEL = 1024
PAST_LEN = 16384
A_GROUPS = ((128, 1), (512, 4), (2048, 16))
A_HEADS = 4
A_HEAD_DIM = 128
A_GW = A_HEADS * A_HEAD_DIM
A_QKV = len(A_GROUPS) * A_GW
A_OUT = A_GW
N_DIST = 128
R_HEADS = 4
R_KEY_DIM = 128
R_VAL_DIM = 256
R_QK = R_HEADS * R_KEY_DIM
R_V = R_HEADS * R_VAL_DIM
R_CHUNK = 256
ROPE_BASE = 10000.0
EPS = 1e-5
NEG_INF = -1e30
DEEPNORM_ALPHA = 2.0 ** 0.25
_SPLIT = (A_QKV, A_QKV, A_QKV, A_OUT, R_QK, R_QK, R_V, R_V, D_MODEL, D_MODEL)
_OFF = tuple(int(v) for v in np.concatenate([[0], np.cumsum(_SPLIT)]))

BF16 = jnp.bfloat16
F32 = jnp.float32
VMEM_LIMIT = 56 * 1024 * 1024

_NT = (((1,), (1,)), ((), ()))
_TN = (((0,), (0,)), ((), ()))


def _dot(a, b):
  return jnp.dot(a, b, preferred_element_type=F32)


def _dot_nt(a, b):
  return lax.dot_general(a, b, _NT, preferred_element_type=F32)


def _dot_tn(a, b):
  return lax.dot_general(a, b, _TN, preferred_element_type=F32)


def _params(sem):
  return pltpu.CompilerParams(dimension_semantics=sem, vmem_limit_bytes=VMEM_LIMIT)


def _mm_kernel(x_ref, w_ref, o_ref):
  o_ref[...] = _dot(x_ref[...].astype(BF16), w_ref[...]).astype(o_ref.dtype)


def _matmul(x, w, col0, ncols, out_dtype, tm, tn):
  m, k = x.shape
  c0 = col0 // tn
  return pl.pallas_call(
      _mm_kernel,
      grid=(m // tm, ncols // tn),
      in_specs=[pl.BlockSpec((tm, k), lambda i, j: (i, 0)),
                pl.BlockSpec((k, tn), lambda i, j: (0, c0 + j))],
      out_specs=pl.BlockSpec((tm, tn), lambda i, j: (i, j)),
      out_shape=jax.ShapeDtypeStruct((m, ncols), out_dtype),
      compiler_params=_params(("parallel", "arbitrary")),
      name="proj_matmul",
  )(x, w)


PROJ_TM = 1024


def _proj_group_kernel(x_ref, wq_ref, wk_ref, wv_ref, o_ref, tail_ref, *scratch,
                       d, first_tail, tail_rows):
  i = pl.program_id(1)
  tm = x_ref.shape[0]
  n = tm // d
  x = x_ref[...]
  xb = x.astype(BF16)
  nat = []
  if d > 1:
    xs_ref, xp_ref = scratch
    for c in range(D_MODEL // 128):
      xs_ref[c] = x[:, c * 128:(c + 1) * 128]
  nsplit = 2 if d > 1 else 1
  rper = d // nsplit
  w_refs = (wq_ref, wk_ref, wv_ref)

  def deinterleave(sb):
    for r in range(sb * rper, (sb + 1) * rper):
      for c in range(D_MODEL // 128):
        xp_ref[r * n:(r + 1) * n, c * 128:(c + 1) * 128] = (
            xs_ref[c, pl.ds(r, n, stride=d), :].astype(BF16))

  def project(sb, part):
    perm = xb if d == 1 else xp_ref[sb * rper * n:(sb + 1) * rper * n, :]
    res = _dot(perm, w_refs[part][...])
    nat.append(res)
    for r in range(sb * rper, (sb + 1) * rper):
      lo = (r - sb * rper) * n
      for h in range(A_HEADS):
        o_ref[r, part * A_HEADS + h] = res[lo:lo + n, h * 128:(h + 1) * 128].astype(BF16)

  if d == 1:
    for part in range(3):
      project(0, part)
  else:
    deinterleave(0)
    project(0, 0)
    deinterleave(1)
    project(0, 1)
    project(0, 2)
    for part in range(3):
      project(1, part)

  @pl.when(i >= first_tail)
  def _():
    if d == 1:
      kv = nat[1:]
    else:
      kv = [_dot(xb, wk_ref[...]), _dot(xb, wv_ref[...])]
    for c in range(2 * A_HEADS):
      src = kv[c // A_HEADS][tm - tail_rows:, (c % A_HEADS) * 128:(c % A_HEADS + 1) * 128]
      tail_ref[pl.ds(c, tail_rows, stride=2 * A_HEADS), :] = src


def _proj_group(x2, w16, g, batch, seq):
  w, d = A_GROUPS[g]
  w = min(w, seq)
  tm = PROJ_TM
  nblk = seq // tm
  tail_rows = min(tm, w)
  first_tail = nblk - max(w // tm, 1)
  ls = seq // d
  wspec = lambda part: pl.BlockSpec((D_MODEL, A_GW), lambda b, i: (0, 3 * part + g))
  scratch = []
  if d > 1:
    scratch = [pltpu.VMEM((D_MODEL // 128, tm, 128), F32), pltpu.VMEM((tm, D_MODEL), BF16)]
  return pl.pallas_call(
      functools.partial(_proj_group_kernel, d=d, first_tail=first_tail, tail_rows=tail_rows),
      grid=(batch, nblk),
      in_specs=[pl.BlockSpec((tm, D_MODEL), lambda b, i: (b * nblk + i, 0)),
                wspec(0), wspec(1), wspec(2)],
      out_specs=[pl.BlockSpec((None, d, 3 * A_HEADS, tm // d, 128), lambda b, i: (b, 0, 0, i, 0)),
                 pl.BlockSpec((None, tail_rows * 8, 128),
                              lambda b, i: (b, jnp.maximum(i - first_tail, 0), 0))],
      out_shape=[jax.ShapeDtypeStruct((batch, d, 3 * A_HEADS, ls, 128), BF16),
                 jax.ShapeDtypeStruct((batch, w * 8, 128), F32)],
      scratch_shapes=scratch,
      compiler_params=_params(("parallel", "arbitrary")),
      name="proj_group",
  )(x2, w16, w16, w16)


ATT_BQ = 128
ATT_SET = 4
LOG2E = 1.4426950408889634


def _attn_prompt_kernel(slopes_ref, *refs, seq):
  qkv = refs[:9]
  o_ref = refs[9]
  oacc = refs[10:13]
  lacc = refs[13:16]
  bias_ref, s_buf, p_buf, l_buf = refs[16:20]
  h = pl.program_id(0)
  scale = A_HEAD_DIM ** -0.5

  @pl.when(pl.program_id(1) == 0)
  def _():
    for g, (_, d) in enumerate(A_GROUPS):
      ls = seq // d
      bq = min(ATT_BQ, ls)
      klen = min(bq + N_DIST, ls)
      slope = slopes_ref[g, h]
      qi = lax.broadcasted_iota(jnp.int32, (bq, klen), 0)
      kj = lax.broadcasted_iota(jnp.int32, (bq, klen), 1)
      for which, off in enumerate((0, N_DIST)):
        delta = qi - kj + off
        valid = (delta >= 0) & (delta <= N_DIST)
        dist = (delta * d).astype(F32)
        bias_ref[2 * g + which, :bq, :klen] = jnp.where(valid, -(slope * LOG2E) * dist, NEG_INF)

  for g, (_, d) in enumerate(A_GROUPS):
    q_ref, k_ref, v_ref = qkv[3 * g:3 * g + 3]
    ls = seq // d
    bq = min(ATT_BQ, ls)
    nqb = ls // bq
    klen = min(bq + N_DIST, ls)

    def coords(idx, nqb=nqb, bq=bq):
      r = idx // nqb
      qb = idx % nqb
      q0 = pl.multiple_of(qb * bq, bq)
      k0 = pl.multiple_of(jnp.maximum(qb * bq - N_DIST, 0), N_DIST)
      return r, qb, q0, k0

    def put_rows(ref, idx, val, d=d, bq=bq):
      r, _, q0, _ = coords(idx)
      if d == 1:
        ref[pl.ds(q0, bq), :] = val
      else:
        ref[pl.ds(r + q0 * d, bq, stride=d), :] = val

    def scores(idx, slot, bq=bq, klen=klen, q_ref=q_ref, k_ref=k_ref):
      r, _, q0, k0 = coords(idx)
      s_buf[slot, :bq, :klen] = _dot_nt(q_ref[r, pl.ds(q0, bq), :], k_ref[r, pl.ds(k0, klen), :])

    def softmax(idx, slot, g=g, bq=bq, klen=klen):
      _, qb, _, _ = coords(idx)
      s = (s_buf[slot, :bq, :klen] * (scale * LOG2E)
           + bias_ref[2 * g + jnp.minimum(qb, 1), :bq, :klen])
      m = jnp.max(s, axis=-1, keepdims=True)
      p = jnp.exp2(s - m)
      l = jnp.sum(p, axis=-1, keepdims=True)
      p_buf[slot, :bq, :klen] = p.astype(BF16)
      l_buf[slot, :bq, :] = jnp.broadcast_to(l, (bq, 128))
      put_rows(lacc[g], idx, jnp.broadcast_to(m + jnp.log2(l), (bq, 128)))

    def values(idx, slot, g=g, bq=bq, klen=klen, v_ref=v_ref):
      r, _, _, k0 = coords(idx)
      o = _dot(p_buf[slot, :bq, :klen], v_ref[r, pl.ds(k0, klen), :]) / l_buf[slot, :bq, :]
      put_rows(oacc[g], idx, o)

    nblk = d * nqb
    ns = ATT_SET

    def run(stage, first, half):
      for t in range(ns):
        stage(first + t, half * ns + t)

    run(scores, 0, 0)
    run(softmax, 0, 0)
    run(scores, ns, 1)

    def steady(j, carry):
      i = 2 * ns * (j + 1)
      run(values, i - 2 * ns, 0)
      run(softmax, i - ns, 1)
      run(scores, i, 0)
      run(values, i - ns, 1)
      run(softmax, i, 0)
      run(scores, i + ns, 1)
      return carry

    lax.fori_loop(0, (nblk - 2 * ns) // (2 * ns), steady, 0)
    run(values, nblk - 2 * ns, 0)
    run(softmax, nblk - ns, 1)
    run(values, nblk - ns, 1)

  mrows = 512

  def merge(i, carry):
    r0 = pl.multiple_of(i * mrows, mrows)
    l0 = lacc[0][pl.ds(r0, mrows), :]
    l1 = lacc[1][pl.ds(r0, mrows), :]
    l2 = lacc[2][pl.ds(r0, mrows), :]
    mx = jnp.maximum(jnp.maximum(l0, l1), l2)
    e0 = jnp.exp2(l0 - mx)
    e1 = jnp.exp2(l1 - mx)
    e2 = jnp.exp2(l2 - mx)
    den = e0 + e1 + e2
    acc = (e0 * oacc[0][pl.ds(r0, mrows), :] + e1 * oacc[1][pl.ds(r0, mrows), :]
           + e2 * oacc[2][pl.ds(r0, mrows), :])
    o_ref[pl.ds(r0, mrows), :] = acc / den
    return carry

  lax.fori_loop(0, seq // mrows, merge, 0)


def _attn_prompt(qkvs, slopes, batch, seq):
  in_specs = [pl.BlockSpec(memory_space=pltpu.SMEM)]
  args = [slopes]
  for g, (_, d) in enumerate(A_GROUPS):
    ls = seq // d
    for part in range(3):
      in_specs.append(pl.BlockSpec(
          (None, d, None, ls, 128),
          lambda h, b, part=part: (b, 0, part * A_HEADS + h, 0, 0)))
      args.append(qkvs[g])
  scratch = ([pltpu.VMEM((seq, 128), F32) for _ in range(6)]
             + [pltpu.VMEM((6, ATT_BQ, ATT_BQ + N_DIST), F32),
                pltpu.VMEM((2 * ATT_SET, ATT_BQ, ATT_BQ + N_DIST), F32),
                pltpu.VMEM((2 * ATT_SET, ATT_BQ, ATT_BQ + N_DIST), BF16),
                pltpu.VMEM((2 * ATT_SET, ATT_BQ, 128), F32)])
  return pl.pallas_call(
      functools.partial(_attn_prompt_kernel, seq=seq),
      grid=(A_HEADS, batch),
      in_specs=in_specs,
      out_specs=pl.BlockSpec((None, seq, 128), lambda h, b: (b, 0, h)),
      out_shape=jax.ShapeDtypeStruct((batch, seq, A_OUT), F32),
      scratch_shapes=scratch,
      compiler_params=_params(("parallel", "arbitrary")),
      name="attn_prompt",
  )(*args)


def _attn_sample_kernel(slopes_ref, q_ref, c0_ref, c1_ref, c2_ref, n0_ref, n1_ref, n2_ref,
                        o_ref, u0_ref, u1_ref, u2_ref, sems, *, tq):
  b = pl.program_id(0)
  cache_refs = (c0_ref, c1_ref, c2_ref)
  new_refs = (n0_ref, n1_ref, n2_ref)
  out_refs = (u0_ref, u1_ref, u2_ref)

  copies = []
  for g, (w, _) in enumerate(A_GROUPS):
    body = pltpu.make_async_copy(cache_refs[g].at[0, pl.ds(tq, w - tq)],
                                 out_refs[g].at[b, pl.ds(0, w - tq)], sems.at[2 * g])
    tail = pltpu.make_async_copy(new_refs[g].at[0], out_refs[g].at[b, pl.ds(w - tq, tq)],
                                 sems.at[2 * g + 1])
    body.start()
    tail.start()
    copies += [body, tail]

  scale = A_HEAD_DIM ** -0.5
  nrow = N_DIST * 8 + tq * 8
  ri = lax.broadcasted_iota(jnp.int32, (nrow, 128), 0)
  cj = lax.broadcasted_iota(jnp.int32, (nrow, 128), 1)
  slot = ri % 8
  key = ri // 8
  tcol = cj // A_HEADS
  hcol = cj % A_HEADS
  col_ok = cj < tq * A_HEADS
  lse_g = []
  ot_g = []
  for g, (w, d) in enumerate(A_GROUPS):
    qg = q_ref[g]
    xn = new_refs[g][0].reshape(tq * 8, 128)
    nres = min(d, tq)
    xs = []
    s = jnp.zeros((nrow, 128), F32)
    col1 = lax.broadcasted_iota(jnp.int32, (128, 1), 0) // A_HEADS
    for rho in range(nres):
      if d == 1:
        xc = cache_refs[g][0]
      else:
        xc = cache_refs[g][0, pl.ds(rho, N_DIST, stride=d), :, :]
      xc = xc.reshape(N_DIST * 8, 128)
      x = jnp.concatenate([xc, xn], axis=0).astype(BF16)
      xs.append(x)
      qm = jnp.where(col1 % d == rho, qg, 0.0).astype(BF16)
      s = s + _dot_nt(x, qm)
    qt = tcol // d
    nidx = key - N_DIST
    in_cache = key < N_DIST
    valid_c = in_cache & (key >= qt)
    dist_c = d * (N_DIST + qt - key)
    valid_n = (~in_cache) & (nidx <= tcol) & ((tcol - nidx) % d == 0)
    dist_n = tcol - nidx
    valid = (valid_c | valid_n) & (slot == hcol) & col_ok
    dist = jnp.where(in_cache, dist_c, dist_n).astype(F32)
    slope = jnp.zeros((nrow, 128), F32)
    for hh in range(A_HEADS):
      slope = jnp.where(hcol == hh, slopes_ref[g, hh], slope)
    s = jnp.where(valid, s * scale - slope * dist, NEG_INF)
    m = jnp.max(s, axis=0, keepdims=True)
    p = jnp.where(valid, jnp.exp(s - m), 0.0)
    l = jnp.sum(p, axis=0, keepdims=True)
    l = jnp.where(l > 0.0, l, 1.0)
    pn = p / l
    pv = pltpu.roll(pn, 4, 0)
    ot = jnp.zeros((128, 128), F32)
    col_row = lax.broadcasted_iota(jnp.int32, (1, 128), 1) // A_HEADS
    for rho in range(nres):
      pm = jnp.where(col_row % d == rho, pv, 0.0).astype(BF16)
      ot = ot + _dot_tn(xs[rho], pm)
    ot_g.append(ot)
    lse_g.append(m + jnp.log(l))
  mx = jnp.maximum(jnp.maximum(lse_g[0], lse_g[1]), lse_g[2])
  es = [jnp.exp(v - mx) for v in lse_g]
  den = es[0] + es[1] + es[2]
  ot = (es[0] * ot_g[0] + es[1] * ot_g[1] + es[2] * ot_g[2]) / den
  o_ref[...] = ot.T

  for c in copies:
    c.wait()


def _attn_sample(q, caches, news, slopes, tq):
  batch = q.shape[0]
  in_specs = [pl.BlockSpec(memory_space=pltpu.SMEM),
              pl.BlockSpec((None, 3, 128, 128), lambda b: (b, 0, 0, 0))]
  args = [slopes, q]
  for g, (w, d) in enumerate(A_GROUPS):
    in_specs.append(pl.BlockSpec((1, w, 8, 128), lambda b: (b, 0, 0, 0)))
    args.append(caches[g])
  for g in range(3):
    in_specs.append(pl.BlockSpec((1, tq, 8, 128), lambda b: (b, 0, 0, 0)))
    args.append(news[g])
  out_shape = [jax.ShapeDtypeStruct((batch, 128, 128), F32)]
  out_specs = [pl.BlockSpec((None, 128, 128), lambda b: (b, 0, 0))]
  for g, (w, _) in enumerate(A_GROUPS):
    out_shape.append(jax.ShapeDtypeStruct((batch, w, 8, 128), F32))
    out_specs.append(pl.BlockSpec(memory_space=pl.ANY))
  return pl.pallas_call(
      functools.partial(_attn_sample_kernel, tq=tq),
      grid=(batch,),
      in_specs=in_specs,
      out_specs=out_specs,
      out_shape=out_shape,
      scratch_shapes=[pltpu.SemaphoreType.DMA((2 * len(A_GROUPS),))],
      compiler_params=_params(("arbitrary",)),
      name="attn_sample",
  )(*args)


RET_SUB = 1024


def _retention_kernel(x_ref, wqk_ref, wv_ref, cos_ref, sin_ref, dmat_ref, qdec_ref, kdec_ref,
                      cdec_ref, s0_ref, gn_ref, o_ref, sout_ref, state, qk_ref, v_ref, *, chunk):
  j = pl.program_id(1)

  @pl.when(j == 0)
  def _():
    state[...] = s0_ref[...]

  tc = x_ref.shape[0]
  sub = min(RET_SUB, tc)
  kscale = R_KEY_DIM ** -0.5
  for ci in range(tc // chunk):
    rows = pl.ds(ci * chunk, chunk)
    if (ci * chunk) % sub == 0:
      sub_rows = pl.ds(ci * chunk, sub)
      xb = x_ref[sub_rows, :].astype(BF16)
      qk_ref[sub_rows, :] = _dot(xb, wqk_ref[...])
      v_ref[sub_rows, :] = _dot(xb, wv_ref[...]).astype(BF16)
    cosv = cos_ref[rows, :]
    sinv = sin_ref[rows, :]
    for h in range(R_HEADS):
      qh = qk_ref[rows, h * 128:(h + 1) * 128]
      kh = qk_ref[rows, R_QK + h * 128:R_QK + (h + 1) * 128]
      qrot = qh * cosv + pltpu.roll(qh, 64, 1) * sinv
      krot = (kh * cosv + pltpu.roll(kh, 64, 1) * sinv) * kscale
      qb = qrot.astype(BF16)
      vh = v_ref[rows, h * 256:(h + 1) * 256]
      att = _dot_nt(qb, krot.astype(BF16)) * dmat_ref[h]
      sh = state[h]
      o = _dot(att.astype(BF16), vh) + _dot(qb, sh.astype(BF16)) * qdec_ref[h]
      state[h] = cdec_ref[h] * sh + _dot_tn((krot * kdec_ref[h]).astype(BF16), vh)
      mu = jnp.mean(o, axis=-1, keepdims=True)
      var = jnp.mean(jnp.square(o - mu), axis=-1, keepdims=True)
      on = (o - mu) * lax.rsqrt(var + EPS) * gn_ref[:, h * 256:(h + 1) * 256]
      o_ref[rows, h * 256:(h + 1) * 256] = on

  @pl.when(j == pl.num_programs(1) - 1)
  def _():
    sout_ref[...] = state[...]


def _retention(x, w16, cos, sin, tables, s0, gn, rows_per_step, chunk):
  batch, t, _ = x.shape
  dmat, qdec, kdec, cdec = tables
  tc = rows_per_step
  const3 = lambda b, j: (0, 0, 0)
  wslab = lambda off: pl.BlockSpec((D_MODEL, R_V), lambda b, j: (0, off // R_V),
                                   pipeline_mode=pl.Buffered(1))
  return pl.pallas_call(
      functools.partial(_retention_kernel, chunk=chunk),
      grid=(batch, t // tc),
      in_specs=[pl.BlockSpec((None, tc, D_MODEL), lambda b, j: (b, j, 0)),
                wslab(_OFF[4]), wslab(_OFF[6]),
                pl.BlockSpec((tc, 128), lambda b, j: (j, 0)),
                pl.BlockSpec((tc, 128), lambda b, j: (j, 0)),
                pl.BlockSpec(dmat.shape, const3),
                pl.BlockSpec(qdec.shape, const3),
                pl.BlockSpec(kdec.shape, const3),
                pl.BlockSpec(cdec.shape, const3),
                pl.BlockSpec((None, R_HEADS, R_KEY_DIM, R_VAL_DIM), lambda b, j: (b, 0, 0, 0)),
                pl.BlockSpec((1, R_V), lambda b, j: (0, 0))],
      out_specs=[pl.BlockSpec((None, tc, R_V), lambda b, j: (b, j, 0)),
                 pl.BlockSpec((None, R_HEADS, R_KEY_DIM, R_VAL_DIM), lambda b, j: (b, 0, 0, 0))],
      out_shape=[jax.ShapeDtypeStruct((batch, t, R_V), F32),
                 jax.ShapeDtypeStruct((batch, R_HEADS, R_KEY_DIM, R_VAL_DIM), F32)],
      scratch_shapes=[pltpu.VMEM((R_HEADS, R_KEY_DIM, R_VAL_DIM), F32),
                      pltpu.VMEM((tc, 2 * R_QK), F32), pltpu.VMEM((tc, R_V), BF16)],
      compiler_params=_params(("parallel", "arbitrary")),
      name="retention",
  )(x, w16, w16, cos, sin, dmat, qdec, kdec, cdec, s0, gn)


def _retention_tables(c_true, c_pad):
  lg = jnp.log1p(-jnp.exp2(-5.0 - jnp.arange(R_HEADS, dtype=F32)))
  i = jnp.arange(c_pad, dtype=F32)
  live = i < c_true
  diff = i[:, None] - i[None, :]
  causal = (diff >= 0) & live[:, None] & live[None, :]
  dmat = jnp.where(causal[None], jnp.exp(jnp.where(causal, diff, 0.0)[None] * lg[:, None, None]), 0.0)
  qdec = jnp.where(live[None, :], jnp.exp((i[None, :] + 1.0) * lg[:, None]), 0.0)
  kdec = jnp.where(live[None, :], jnp.exp((c_true - 1.0 - i)[None, :] * lg[:, None]), 0.0)
  cdec = jnp.exp(c_true * lg)
  qdec = jnp.broadcast_to(qdec[:, :, None], (R_HEADS, c_pad, R_VAL_DIM))
  kdec = jnp.broadcast_to(kdec[:, :, None], (R_HEADS, c_pad, R_KEY_DIM))
  cdec = jnp.broadcast_to(cdec[:, None, None], (R_HEADS, R_KEY_DIM, R_VAL_DIM))
  return dmat, qdec, kdec, cdec


def _rope_tables(pos):
  half = R_KEY_DIM // 2
  inv = 1.0 / (ROPE_BASE ** jnp.linspace(0.0, 1.0, half, dtype=F32))
  ang = pos.astype(F32)[:, None] * inv[None, :]
  cos = jnp.cos(ang)
  sin = jnp.sin(ang)
  return jnp.concatenate([cos, cos], axis=-1), jnp.concatenate([-sin, sin], axis=-1)


def _output_kernel(x_ref, oa_ref, or_ref, wza_ref, wzr_ref, wga_ref, wgb_ref, wpa_ref, wpb_ref,
                   wo_ref, lng_ref, lnb_ref, y_ref):
  tm = x_ref.shape[0]
  nsub = 2 if tm % 32 == 0 else 1
  sub = tm // nsub
  for si in range(nsub):
    rows = pl.ds(si * sub, sub)
    x = x_ref[rows, :]
    xb = x.astype(BF16)
    za = _dot(xb, wza_ref[...])
    ya = _dot((jax.nn.silu(za) * oa_ref[rows, :]).astype(BF16), wpa_ref[...])
    zr = _dot(xb, wzr_ref[...])
    yb = _dot((jax.nn.silu(zr) * or_ref[rows, :]).astype(BF16), wpb_ref[...])
    ga = _dot(xb, wga_ref[...])
    gb = _dot(xb, wgb_ref[...])
    mix = jax.nn.sigmoid(ga) * ya + jax.nn.sigmoid(gb) * yb
    out = _dot(mix.astype(BF16), wo_ref[...])
    zz = DEEPNORM_ALPHA * x + out
    zm = jnp.mean(zz, axis=-1, keepdims=True)
    zv = jnp.mean(jnp.square(zz - zm), axis=-1, keepdims=True)
    y_ref[rows, :] = (zz - zm) * lax.rsqrt(zv + EPS) * lng_ref[...] + lnb_ref[...]


def _output(x, oa, orr, w16, wpa, wpb, wo, lng, lnb, tm):
  m = x.shape[0]
  row = lambda w: pl.BlockSpec((tm, w), lambda i: (i, 0))
  full = lambda a: pl.BlockSpec(a.shape, lambda i: (0, 0), pipeline_mode=pl.Buffered(1))
  wcol = lambda off, width: pl.BlockSpec((D_MODEL, width), lambda i: (0, off // width),
                                         pipeline_mode=pl.Buffered(1))
  in_specs = [row(D_MODEL), row(A_OUT), row(R_V),
              wcol(_OFF[3], A_OUT), wcol(_OFF[7], R_V), wcol(_OFF[8], D_MODEL),
              wcol(_OFF[9], D_MODEL),
              full(wpa), full(wpb), full(wo), full(lng), full(lnb)]
  args = [x, oa, orr, w16, w16, w16, w16, wpa, wpb, wo, lng, lnb]
  return pl.pallas_call(
      _output_kernel,
      grid=(m // tm,),
      in_specs=in_specs,
      out_specs=row(D_MODEL),
      out_shape=jax.ShapeDtypeStruct((m, D_MODEL), F32),
      compiler_params=_params(("parallel",)),
      name="gates_output",
  )(*args)


def _alibi_slopes():
  n = len(A_GROUPS) * A_HEADS
  return jnp.exp2(-8.0 * jnp.arange(1, n + 1, dtype=F32) / n).reshape(len(A_GROUPS), A_HEADS)


def kernel(x_prompt, x_sample, cache_kv_w128, cache_kv_w512, cache_kv_w2048, state_ret,
           w_in, w_pa, w_pb, w_o, gn_g, ln_g, ln_b):
  bp, tp, _ = x_prompt.shape
  bs, ts, _ = x_sample.shape
  slopes = _alibi_slopes()
  w16 = w_in.astype(BF16)
  wpa = w_pa.astype(BF16)
  wpb = w_pb.astype(BF16)
  wo = w_o.astype(BF16)
  gn = gn_g.reshape(1, R_V)
  lng = ln_g.reshape(1, D_MODEL)
  lnb = ln_b.reshape(1, D_MODEL)

  xp2 = x_prompt.reshape(bp * tp, D_MODEL)
  qkvs, kv_p = [], []
  for g, (w, _) in enumerate(A_GROUPS):
    qkv, tail = _proj_group(xp2, w16, g, bp, tp)
    qkvs.append(qkv)
    kv_p.append(tail.reshape(bp, min(w, tp), 2, A_HEADS, A_HEAD_DIM))
  oa_p = _attn_prompt(qkvs, slopes, bp, tp).reshape(bp * tp, A_OUT)

  cos_p, sin_p = _rope_tables(jnp.arange(tp, dtype=jnp.int32))
  or_p, s_p = _retention(
      x_prompt, w16, cos_p, sin_p, _retention_tables(R_CHUNK, R_CHUNK),
      jnp.zeros((bp, R_HEADS, R_KEY_DIM, R_VAL_DIM), F32), gn, 1024, R_CHUNK)
  y_p = _output(xp2, oa_p, or_p.reshape(bp * tp, R_V), w16, wpa, wpb, wo, lng, lnb, 512)
  xs2 = x_sample.reshape(bs * ts, D_MODEL)
  hs = _matmul(xs2, w16, 0, _OFF[3], F32, bs * ts, A_GW)
  caches = (cache_kv_w128, cache_kv_w512, cache_kv_w2048)
  q_s, news = [], []
  for g in range(3):
    col = lambda part: hs[:, _OFF[part] + g * A_GW:_OFF[part] + (g + 1) * A_GW]
    qg = col(0).reshape(bs, ts * A_HEADS, A_HEAD_DIM)
    q_s.append(jnp.pad(qg, ((0, 0), (0, 128 - ts * A_HEADS), (0, 0))))
    news.append(jnp.concatenate([col(1), col(2)], axis=1).reshape(bs, ts, 8, 128))
  q_s = jnp.stack(q_s, axis=1)
  caches8 = [c.reshape(bs, c.shape[1], 8, 128) for c in caches]
  oa_s, u0, u1, u2 = _attn_sample(q_s, caches8, news, slopes, ts)
  oa_s = oa_s[:, :ts * A_HEADS].reshape(bs * ts, A_OUT)
  kv_s = [u.reshape(bs, u.shape[1], 2, A_HEADS, A_HEAD_DIM) for u in (u0, u1, u2)]

  cpad = 16
  xs_pad = jnp.pad(x_sample, ((0, 0), (0, cpad - ts), (0, 0)))
  cos_s, sin_s = _rope_tables(PAST_LEN + jnp.arange(ts, dtype=jnp.int32))
  padtab = lambda a: jnp.pad(a, ((0, cpad - ts), (0, 0)))
  or_s, s_s = _retention(
      xs_pad, w16, padtab(cos_s), padtab(sin_s),
      _retention_tables(ts, cpad), state_ret.astype(F32), gn, cpad, cpad)
  or_s = or_s[:, :ts].reshape(bs * ts, R_V)
  y_s = _output(xs2, oa_s, or_s, w16, wpa, wpb, wo, lng, lnb, 256)

  return (y_p.reshape(bp, tp, D_MODEL), y_s.reshape(bs, ts, D_MODEL),
          kv_p[0], kv_p[1], kv_p[2], s_p,
          kv_s[0], kv_s[1], kv_s[2], s_s)
```

```python
import functools

import numpy as np
import jax
import jax.numpy as jnp
from jax import lax
from jax.experimental import pallas as pl
from jax.experimental.pallas import tpu as pltpu

D_MODEL = 1024
PAST_LEN = 16384
A_GROUPS = ((128, 1), (512, 4), (2048, 16))
A_HEADS = 4
A_HEAD_DIM = 128
A_GW = A_HEADS * A_HEAD_DIM
A_QKV = len(A_GROUPS) * A_GW
A_OUT = A_GW
N_DIST = 128
R_HEADS = 4
R_KEY_DIM = 128
R_VAL_DIM = 256
R_QK = R_HEADS * R_KEY_DIM
R_V = R_HEADS * R_VAL_DIM
R_CHUNK = 256
ROPE_BASE = 10000.0
EPS = 1e-5
NEG_INF = -1e30
DEEPNORM_ALPHA = 2.0 ** 0.25
_SPLIT = (A_QKV, A_QKV, A_QKV, A_OUT, R_QK, R_QK, R_V, R_V, D_MODEL, D_MODEL)
_OFF = tuple(int(v) for v in np.concatenate([[0], np.cumsum(_SPLIT)]))

BF16 = jnp.bfloat16
F32 = jnp.float32
VMEM_LIMIT = 56 * 1024 * 1024

_NT = (((1,), (1,)), ((), ()))
_TN = (((0,), (0,)), ((), ()))


def _dot(a, b):
  return jnp.dot(a, b, preferred_element_type=F32)


def _dot_nt(a, b):
  return lax.dot_general(a, b, _NT, preferred_element_type=F32)


def _dot_tn(a, b):
  return lax.dot_general(a, b, _TN, preferred_element_type=F32)


def _params(sem):
  return pltpu.CompilerParams(dimension_semantics=sem, vmem_limit_bytes=VMEM_LIMIT)


def _mm_kernel(x_ref, w_ref, o_ref):
  o_ref[...] = _dot(x_ref[...].astype(BF16), w_ref[...]).astype(o_ref.dtype)


def _matmul(x, w, col0, ncols, out_dtype, tm, tn):
  m, k = x.shape
  c0 = col0 // tn
  return pl.pallas_call(
      _mm_kernel,
      grid=(m // tm, ncols // tn),
      in_specs=[pl.BlockSpec((tm, k), lambda i, j: (i, 0)),
                pl.BlockSpec((k, tn), lambda i, j: (0, c0 + j))],
      out_specs=pl.BlockSpec((tm, tn), lambda i, j: (i, j)),
      out_shape=jax.ShapeDtypeStruct((m, ncols), out_dtype),
      compiler_params=_params(("parallel", "arbitrary")),
      name="proj_matmul",
  )(x, w)


PROJ_TM = 1024


def _proj_group_kernel(x_ref, wq_ref, wk_ref, wv_ref, o_ref, tail_ref, *scratch,
                       d, first_tail, tail_rows):
  i = pl.program_id(1)
  tm = x_ref.shape[0]
  n = tm // d
  x = x_ref[...]
  xb = x.astype(BF16)
  nat = []
  if d > 1:
    xs_ref, xp_ref = scratch
    for c in range(D_MODEL // 128):
      xs_ref[c] = x[:, c * 128:(c + 1) * 128]
  nsplit = 2 if d > 1 else 1
  rper = d // nsplit
  w_refs = (wq_ref, wk_ref, wv_ref)

  def deinterleave(sb):
    for r in range(sb * rper, (sb + 1) * rper):
      for c in range(D_MODEL // 128):
        xp_ref[r * n:(r + 1) * n, c * 128:(c + 1) * 128] = (
            xs_ref[c, pl.ds(r, n, stride=d), :].astype(BF16))

  def project(sb, part):
    perm = xb if d == 1 else xp_ref[sb * rper * n:(sb + 1) * rper * n, :]
    res = _dot(perm, w_refs[part][...])
    nat.append(res)
    for r in range(sb * rper, (sb + 1) * rper):
      lo = (r - sb * rper) * n
      for h in range(A_HEADS):
        o_ref[r, part * A_HEADS + h] = res[lo:lo + n, h * 128:(h + 1) * 128].astype(BF16)

  if d == 1:
    for part in range(3):
      project(0, part)
  else:
    deinterleave(0)
    project(0, 0)
    deinterleave(1)
    project(0, 1)
    project(0, 2)
    for part in range(3):
      project(1, part)

  @pl.when(i >= first_tail)
  def _():
    if d == 1:
      kv = nat[1:]
    else:
      kv = [_dot(xb, wk_ref[...]), _dot(xb, wv_ref[...])]
    for c in range(2 * A_HEADS):
      src = kv[c // A_HEADS][tm - tail_rows:, (c % A_HEADS) * 128:(c % A_HEADS + 1) * 128]
      tail_ref[pl.ds(c, tail_rows, stride=2 * A_HEADS), :] = src


def _proj_group(x2, w16, g, batch, seq):
  w, d = A_GROUPS[g]
  w = min(w, seq)
  tm = PROJ_TM
  nblk = seq // tm
  tail_rows = min(tm, w)
  first_tail = nblk - max(w // tm, 1)
  ls = seq // d
  wspec = lambda part: pl.BlockSpec((D_MODEL, A_GW), lambda b, i: (0, 3 * part + g))
  scratch = []
  if d > 1:
    scratch = [pltpu.VMEM((D_MODEL // 128, tm, 128), F32), pltpu.VMEM((tm, D_MODEL), BF16)]
  return pl.pallas_call(
      functools.partial(_proj_group_kernel, d=d, first_tail=first_tail, tail_rows=tail_rows),
      grid=(batch, nblk),
      in_specs=[pl.BlockSpec((tm, D_MODEL), lambda b, i: (b * nblk + i, 0)),
                wspec(0), wspec(1), wspec(2)],
      out_specs=[pl.BlockSpec((None, d, 3 * A_HEADS, tm // d, 128), lambda b, i: (b, 0, 0, i, 0)),
                 pl.BlockSpec((None, tail_rows * 8, 128),
                              lambda b, i: (b, jnp.maximum(i - first_tail, 0), 0))],
      out_shape=[jax.ShapeDtypeStruct((batch, d, 3 * A_HEADS, ls, 128), BF16),
                 jax.ShapeDtypeStruct((batch, w * 8, 128), F32)],
      scratch_shapes=scratch,
      compiler_params=_params(("parallel", "arbitrary")),
      name="proj_group",
  )(x2, w16, w16, w16)


ATT_BQ = 128
ATT_SET = 4
LOG2E = 1.4426950408889634


def _attn_prompt_kernel(slopes_ref, *refs, seq):
  qkv = refs[:9]
  o_ref = refs[9]
  oacc = refs[10:13]
  lacc = refs[13:16]
  bias_ref, s_buf, p_buf, l_buf = refs[16:20]
  h = pl.program_id(0)
  scale = A_HEAD_DIM ** -0.5

  @pl.when(pl.program_id(1) == 0)
  def _():
    for g, (_, d) in enumerate(A_GROUPS):
      ls = seq // d
      bq = min(ATT_BQ, ls)
      klen = min(bq + N_DIST, ls)
      slope = slopes_ref[g, h]
      qi = lax.broadcasted_iota(jnp.int32, (bq, klen), 0)
      kj = lax.broadcasted_iota(jnp.int32, (bq, klen), 1)
      for which, off in enumerate((0, N_DIST)):
        delta = qi - kj + off
        valid = (delta >= 0) & (delta <= N_DIST)
        dist = (delta * d).astype(F32)
        bias_ref[2 * g + which, :bq, :klen] = jnp.where(valid, -(slope * LOG2E) * dist, NEG_INF)

  for g, (_, d) in enumerate(A_GROUPS):
    q_ref, k_ref, v_ref = qkv[3 * g:3 * g + 3]
    ls = seq // d
    bq = min(ATT_BQ, ls)
    nqb = ls // bq
    klen = min(bq + N_DIST, ls)

    def coords(idx, nqb=nqb, bq=bq):
      r = idx // nqb
      qb = idx % nqb
      q0 = pl.multiple_of(qb * bq, bq)
      k0 = pl.multiple_of(jnp.maximum(qb * bq - N_DIST, 0), N_DIST)
      return r, qb, q0, k0

    def put_rows(ref, idx, val, d=d, bq=bq):
      r, _, q0, _ = coords(idx)
      if d == 1:
        ref[pl.ds(q0, bq), :] = val
      else:
        ref[pl.ds(r + q0 * d, bq, stride=d), :] = val

    def scores(idx, slot, bq=bq, klen=klen, q_ref=q_ref, k_ref=k_ref):
      r, _, q0, k0 = coords(idx)
      s_buf[slot, :bq, :klen] = _dot_nt(q_ref[r, pl.ds(q0, bq), :], k_ref[r, pl.ds(k0, klen), :])

    def softmax(idx, slot, g=g, bq=bq, klen=klen):
      _, qb, _, _ = coords(idx)
      s = (s_buf[slot, :bq, :klen] * (scale * LOG2E)
           + bias_ref[2 * g + jnp.minimum(qb, 1), :bq, :klen])
      m = jnp.max(s, axis=-1, keepdims=True)
      p = jnp.exp2(s - m)
      l = jnp.sum(p, axis=-1, keepdims=True)
      p_buf[slot, :bq, :klen] = p.astype(BF16)
      l_buf[slot, :bq, :] = jnp.broadcast_to(l, (bq, 128))
      put_rows(lacc[g], idx, jnp.broadcast_to(m + jnp.log2(l), (bq, 128)))

    def values(idx, slot, g=g, bq=bq, klen=klen, v_ref=v_ref):
      r, _, _, k0 = coords(idx)
      o = _dot(p_buf[slot, :bq, :klen], v_ref[r, pl.ds(k0, klen), :]) / l_buf[slot, :bq, :]
      put_rows(oacc[g], idx, o)

    nblk = d * nqb
    ns = ATT_SET

    def run(stage, first, half):
      for t in range(ns):
        stage(first + t, half * ns + t)

    run(scores, 0, 0)
    run(softmax, 0, 0)
    run(scores, ns, 1)

    def steady(j, carry):
      i = 2 * ns * (j + 1)
      run(values, i - 2 * ns, 0)
      run(softmax, i - ns, 1)
      run(scores, i, 0)
      run(values, i - ns, 1)
      run(softmax, i, 0)
      run(scores, i + ns, 1)
      return carry

    lax.fori_loop(0, (nblk - 2 * ns) // (2 * ns), steady, 0)
    run(values, nblk - 2 * ns, 0)
    run(softmax, nblk - ns, 1)
    run(values, nblk - ns, 1)

  mrows = 512

  def merge(i, carry):
    r0 = pl.multiple_of(i * mrows, mrows)
    l0 = lacc[0][pl.ds(r0, mrows), :]
    l1 = lacc[1][pl.ds(r0, mrows), :]
    l2 = lacc[2][pl.ds(r0, mrows), :]
    mx = jnp.maximum(jnp.maximum(l0, l1), l2)
    e0 = jnp.exp2(l0 - mx)
    e1 = jnp.exp2(l1 - mx)
    e2 = jnp.exp2(l2 - mx)
    den = e0 + e1 + e2
    acc = (e0 * oacc[0][pl.ds(r0, mrows), :] + e1 * oacc[1][pl.ds(r0, mrows), :]
           + e2 * oacc[2][pl.ds(r0, mrows), :])
    o_ref[pl.ds(r0, mrows), :] = acc / den
    return carry

  lax.fori_loop(0, seq // mrows, merge, 0)


def _attn_prompt(qkvs, slopes, batch, seq):
  in_specs = [pl.BlockSpec(memory_space=pltpu.SMEM)]
  args = [slopes]
  for g, (_, d) in enumerate(A_GROUPS):
    ls = seq // d
    for part in range(3):
      in_specs.append(pl.BlockSpec(
          (None, d, None, ls, 128),
          lambda h, b, part=part: (b, 0, part * A_HEADS + h, 0, 0)))
      args.append(qkvs[g])
  scratch = ([pltpu.VMEM((seq, 128), F32) for _ in range(6)]
             + [pltpu.VMEM((6, ATT_BQ, ATT_BQ + N_DIST), F32),
                pltpu.VMEM((2 * ATT_SET, ATT_BQ, ATT_BQ + N_DIST), F32),
                pltpu.VMEM((2 * ATT_SET, ATT_BQ, ATT_BQ + N_DIST), BF16),
                pltpu.VMEM((2 * ATT_SET, ATT_BQ, 128), F32)])
  return pl.pallas_call(
      functools.partial(_attn_prompt_kernel, seq=seq),
      grid=(A_HEADS, batch),
      in_specs=in_specs,
      out_specs=pl.BlockSpec((None, seq, 128), lambda h, b: (b, 0, h)),
      out_shape=jax.ShapeDtypeStruct((batch, seq, A_OUT), F32),
      scratch_shapes=scratch,
      compiler_params=_params(("parallel", "arbitrary")),
      name="attn_prompt",
  )(*args)


def _attn_sample_kernel(slopes_ref, q_ref, c0_ref, c1_ref, c2_ref, n0_ref, n1_ref, n2_ref,
                        xs_ref, wqk_ref, wv_ref, cos_ref, sin_ref, dmat_ref, qdec_ref, kdec_ref,
                        cdec_ref, s0_ref, gn_ref,
                        o_ref, u0_ref, u1_ref, u2_ref, or_ref, sout_ref,
                        sems, qk_scr, v_scr, *, tq, chunk):
  b = pl.program_id(0)
  cache_refs = (c0_ref, c1_ref, c2_ref)
  new_refs = (n0_ref, n1_ref, n2_ref)
  out_refs = (u0_ref, u1_ref, u2_ref)

  copies = []
  for g, (w, _) in enumerate(A_GROUPS):
    body = pltpu.make_async_copy(cache_refs[g].at[0, pl.ds(tq, w - tq)],
                                 out_refs[g].at[b, pl.ds(0, w - tq)], sems.at[2 * g])
    tail = pltpu.make_async_copy(new_refs[g].at[0], out_refs[g].at[b, pl.ds(w - tq, tq)],
                                 sems.at[2 * g + 1])
    body.start()
    tail.start()
    copies += [body, tail]

  scale = A_HEAD_DIM ** -0.5
  nrow = N_DIST * 8 + tq * 8
  ri = lax.broadcasted_iota(jnp.int32, (nrow, 128), 0)
  cj = lax.broadcasted_iota(jnp.int32, (nrow, 128), 1)
  slot = ri % 8
  key = ri // 8
  tcol = cj // A_HEADS
  hcol = cj % A_HEADS
  col_ok = cj < tq * A_HEADS
  lse_g = []
  ot_g = []
  for g, (w, d) in enumerate(A_GROUPS):
    qg = q_ref[g]
    xn = new_refs[g][0].reshape(tq * 8, 128)
    nres = min(d, tq)
    xs = []
    s = jnp.zeros((nrow, 128), F32)
    col1 = lax.broadcasted_iota(jnp.int32, (128, 1), 0) // A_HEADS
    for rho in range(nres):
      if d == 1:
        xc = cache_refs[g][0]
      else:
        xc = cache_refs[g][0, pl.ds(rho, N_DIST, stride=d), :, :]
      xc = xc.reshape(N_DIST * 8, 128)
      x = jnp.concatenate([xc, xn], axis=0).astype(BF16)
      xs.append(x)
      qm = jnp.where(col1 % d == rho, qg, 0.0).astype(BF16)
      s = s + _dot_nt(x, qm)
    qt = tcol // d
    nidx = key - N_DIST
    in_cache = key < N_DIST
    valid_c = in_cache & (key >= qt)
    dist_c = d * (N_DIST + qt - key)
    valid_n = (~in_cache) & (nidx <= tcol) & ((tcol - nidx) % d == 0)
    dist_n = tcol - nidx
    valid = (valid_c | valid_n) & (slot == hcol) & col_ok
    dist = jnp.where(in_cache, dist_c, dist_n).astype(F32)
    slope = jnp.zeros((nrow, 128), F32)
    for hh in range(A_HEADS):
      slope = jnp.where(hcol == hh, slopes_ref[g, hh], slope)
    s = jnp.where(valid, s * scale - slope * dist, NEG_INF)
    m = jnp.max(s, axis=0, keepdims=True)
    p = jnp.where(valid, jnp.exp(s - m), 0.0)
    l = jnp.sum(p, axis=0, keepdims=True)
    l = jnp.where(l > 0.0, l, 1.0)
    pn = p / l
    pv = pltpu.roll(pn, 4, 0)
    ot = jnp.zeros((128, 128), F32)
    col_row = lax.broadcasted_iota(jnp.int32, (1, 128), 1) // A_HEADS
    for rho in range(nres):
      pm = jnp.where(col_row % d == rho, pv, 0.0).astype(BF16)
      ot = ot + _dot_tn(xs[rho], pm)
    ot_g.append(ot)
    lse_g.append(m + jnp.log(l))
  mx = jnp.maximum(jnp.maximum(lse_g[0], lse_g[1]), lse_g[2])
  es = [jnp.exp(v - mx) for v in lse_g]
  den = es[0] + es[1] + es[2]
  ot = (es[0] * ot_g[0] + es[1] * ot_g[1] + es[2] * ot_g[2]) / den
  o_ref[...] = ot.T

  sout_ref[...] = s0_ref[...]
  _retention_rows(xs_ref.at[0], wqk_ref, wv_ref, cos_ref, sin_ref, dmat_ref, qdec_ref, kdec_ref,
                  cdec_ref, gn_ref, or_ref.at[0], sout_ref.at[0], qk_scr, v_scr, chunk)

  for c in copies:
    c.wait()


def _attn_sample(q, caches, news, slopes, tq, ret):
  batch = q.shape[0]
  xs, w16, cos, sin, (dmat, qdec, kdec, cdec), s0, gn = ret
  cpad = xs.shape[1]
  in_specs = [pl.BlockSpec(memory_space=pltpu.SMEM),
              pl.BlockSpec((None, 3, 128, 128), lambda b: (b, 0, 0, 0))]
  args = [slopes, q]
  for g, (w, d) in enumerate(A_GROUPS):
    in_specs.append(pl.BlockSpec((1, w, 8, 128), lambda b: (b, 0, 0, 0)))
    args.append(caches[g])
  for g in range(3):
    in_specs.append(pl.BlockSpec((1, tq, 8, 128), lambda b: (b, 0, 0, 0)))
    args.append(news[g])
  const = lambda a: pl.BlockSpec(a.shape, lambda b: (0,) * a.ndim)
  wslab = lambda off: pl.BlockSpec((D_MODEL, R_V), lambda b: (0, off // R_V),
                                   pipeline_mode=pl.Buffered(1))
  state_spec = pl.BlockSpec((1, R_HEADS, R_KEY_DIM, R_VAL_DIM), lambda b: (b, 0, 0, 0))
  in_specs += [pl.BlockSpec((1, cpad, D_MODEL), lambda b: (b, 0, 0)),
               wslab(_OFF[4]), wslab(_OFF[6]), const(cos), const(sin),
               const(dmat), const(qdec), const(kdec), const(cdec), state_spec, const(gn)]
  args += [xs, w16, w16, cos, sin, dmat, qdec, kdec, cdec, s0, gn]
  out_shape = [jax.ShapeDtypeStruct((batch, 128, 128), F32)]
  out_specs = [pl.BlockSpec((None, 128, 128), lambda b: (b, 0, 0))]
  for g, (w, _) in enumerate(A_GROUPS):
    out_shape.append(jax.ShapeDtypeStruct((batch, w, 8, 128), F32))
    out_specs.append(pl.BlockSpec(memory_space=pl.ANY))
  out_shape += [jax.ShapeDtypeStruct((batch, cpad, R_V), F32),
                jax.ShapeDtypeStruct(s0.shape, F32)]
  out_specs += [pl.BlockSpec((1, cpad, R_V), lambda b: (b, 0, 0)), state_spec]
  return pl.pallas_call(
      functools.partial(_attn_sample_kernel, tq=tq, chunk=cpad),
      grid=(batch,),
      in_specs=in_specs,
      out_specs=out_specs,
      out_shape=out_shape,
      scratch_shapes=[pltpu.SemaphoreType.DMA((2 * len(A_GROUPS),)),
                      pltpu.VMEM((cpad, 2 * R_QK), F32), pltpu.VMEM((cpad, R_V), BF16)],
      compiler_params=_params(("arbitrary",)),
      name="attn_sample",
  )(*args)


def _retention_rows(x_ref, wqk_ref, wv_ref, cos_ref, sin_ref, dmat_ref, qdec_ref, kdec_ref,
                    cdec_ref, gn_ref, o_ref, state, qk_ref, v_ref, chunk):
  xb = x_ref[...].astype(BF16)
  qk_ref[...] = _dot(xb, wqk_ref[...])
  v_ref[...] = _dot(xb, wv_ref[...]).astype(BF16)
  kscale = R_KEY_DIM ** -0.5
  for ci in range(x_ref.shape[0] // chunk):
    rows = pl.ds(ci * chunk, chunk)
    cosv = cos_ref[rows, :]
    sinv = sin_ref[rows, :]
    for h in range(R_HEADS):
      qh = qk_ref[rows, h * 128:(h + 1) * 128]
      kh = qk_ref[rows, R_QK + h * 128:R_QK + (h + 1) * 128]
      qrot = qh * cosv + pltpu.roll(qh, 64, 1) * sinv
      krot = (kh * cosv + pltpu.roll(kh, 64, 1) * sinv) * kscale
      qb = qrot.astype(BF16)
      vh = v_ref[rows, h * 256:(h + 1) * 256]
      att = _dot_nt(qb, krot.astype(BF16)) * dmat_ref[h]
      sh = state[h]
      o = _dot(att.astype(BF16), vh) + _dot(qb, sh.astype(BF16)) * qdec_ref[h]
      state[h] = cdec_ref[h] * sh + _dot_tn((krot * kdec_ref[h]).astype(BF16), vh)
      mu = jnp.mean(o, axis=-1, keepdims=True)
      var = jnp.mean(jnp.square(o - mu), axis=-1, keepdims=True)
      on = (o - mu) * lax.rsqrt(var + EPS) * gn_ref[:, h * 256:(h + 1) * 256]
      o_ref[rows, h * 256:(h + 1) * 256] = on


def _retention_kernel(x_ref, wqk_ref, wv_ref, cos_ref, sin_ref, dmat_ref, qdec_ref, kdec_ref,
                      cdec_ref, s0_ref, gn_ref, o_ref, sout_ref, state, qk_ref, v_ref, *, chunk):
  j = pl.program_id(1)

  @pl.when(j == 0)
  def _():
    state[...] = s0_ref[...]

  _retention_rows(x_ref, wqk_ref, wv_ref, cos_ref, sin_ref, dmat_ref, qdec_ref, kdec_ref,
                  cdec_ref, gn_ref, o_ref, state, qk_ref, v_ref, chunk)

  @pl.when(j == pl.num_programs(1) - 1)
  def _():
    sout_ref[...] = state[...]


def _retention(x, w16, cos, sin, tables, s0, gn, rows_per_step, chunk):
  batch, t, _ = x.shape
  dmat, qdec, kdec, cdec = tables
  tc = rows_per_step
  const3 = lambda b, j: (0, 0, 0)
  wslab = lambda off: pl.BlockSpec((D_MODEL, R_V), lambda b, j: (0, off // R_V),
                                   pipeline_mode=pl.Buffered(1))
  return pl.pallas_call(
      functools.partial(_retention_kernel, chunk=chunk),
      grid=(batch, t // tc),
      in_specs=[pl.BlockSpec((None, tc, D_MODEL), lambda b, j: (b, j, 0)),
                wslab(_OFF[4]), wslab(_OFF[6]),
                pl.BlockSpec((tc, 128), lambda b, j: (j, 0)),
                pl.BlockSpec((tc, 128), lambda b, j: (j, 0)),
                pl.BlockSpec(dmat.shape, const3),
                pl.BlockSpec(qdec.shape, const3),
                pl.BlockSpec(kdec.shape, const3),
                pl.BlockSpec(cdec.shape, const3),
                pl.BlockSpec((None, R_HEADS, R_KEY_DIM, R_VAL_DIM), lambda b, j: (b, 0, 0, 0)),
                pl.BlockSpec((1, R_V), lambda b, j: (0, 0))],
      out_specs=[pl.BlockSpec((None, tc, R_V), lambda b, j: (b, j, 0)),
                 pl.BlockSpec((None, R_HEADS, R_KEY_DIM, R_VAL_DIM), lambda b, j: (b, 0, 0, 0))],
      out_shape=[jax.ShapeDtypeStruct((batch, t, R_V), F32),
                 jax.ShapeDtypeStruct((batch, R_HEADS, R_KEY_DIM, R_VAL_DIM), F32)],
      scratch_shapes=[pltpu.VMEM((R_HEADS, R_KEY_DIM, R_VAL_DIM), F32),
                      pltpu.VMEM((tc, 2 * R_QK), F32), pltpu.VMEM((tc, R_V), BF16)],
      compiler_params=_params(("parallel", "arbitrary")),
      name="retention",
  )(x, w16, w16, cos, sin, dmat, qdec, kdec, cdec, s0, gn)


def _retention_tables(c_true, c_pad):
  lg = jnp.log1p(-jnp.exp2(-5.0 - jnp.arange(R_HEADS, dtype=F32)))
  i = jnp.arange(c_pad, dtype=F32)
  live = i < c_true
  diff = i[:, None] - i[None, :]
  causal = (diff >= 0) & live[:, None] & live[None, :]
  dmat = jnp.where(causal[None], jnp.exp(jnp.where(causal, diff, 0.0)[None] * lg[:, None, None]), 0.0)
  qdec = jnp.where(live[None, :], jnp.exp((i[None, :] + 1.0) * lg[:, None]), 0.0)
  kdec = jnp.where(live[None, :], jnp.exp((c_true - 1.0 - i)[None, :] * lg[:, None]), 0.0)
  cdec = jnp.exp(c_true * lg)
  qdec = jnp.broadcast_to(qdec[:, :, None], (R_HEADS, c_pad, R_VAL_DIM))
  kdec = jnp.broadcast_to(kdec[:, :, None], (R_HEADS, c_pad, R_KEY_DIM))
  cdec = jnp.broadcast_to(cdec[:, None, None], (R_HEADS, R_KEY_DIM, R_VAL_DIM))
  return dmat, qdec, kdec, cdec


def _rope_tables(pos):
  half = R_KEY_DIM // 2
  inv = 1.0 / (ROPE_BASE ** jnp.linspace(0.0, 1.0, half, dtype=F32))
  ang = pos.astype(F32)[:, None] * inv[None, :]
  cos = jnp.cos(ang)
  sin = jnp.sin(ang)
  return jnp.concatenate([cos, cos], axis=-1), jnp.concatenate([-sin, sin], axis=-1)


def _output_kernel(x_ref, oa_ref, or_ref, wza_ref, wzr_ref, wga_ref, wgb_ref, wpa_ref, wpb_ref,
                   wo_ref, lng_ref, lnb_ref, y_ref):
  tm = x_ref.shape[0]
  nsub = 2 if tm % 32 == 0 else 1
  sub = tm // nsub
  for si in range(nsub):
    rows = pl.ds(si * sub, sub)
    x = x_ref[rows, :]
    xb = x.astype(BF16)
    za = _dot(xb, wza_ref[...])
    ya = _dot((jax.nn.silu(za) * oa_ref[rows, :]).astype(BF16), wpa_ref[...])
    zr = _dot(xb, wzr_ref[...])
    yb = _dot((jax.nn.silu(zr) * or_ref[rows, :]).astype(BF16), wpb_ref[...])
    ga = _dot(xb, wga_ref[...])
    gb = _dot(xb, wgb_ref[...])
    mix = jax.nn.sigmoid(ga) * ya + jax.nn.sigmoid(gb) * yb
    out = _dot(mix.astype(BF16), wo_ref[...])
    zz = DEEPNORM_ALPHA * x + out
    zm = jnp.mean(zz, axis=-1, keepdims=True)
    zv = jnp.mean(jnp.square(zz - zm), axis=-1, keepdims=True)
    y_ref[rows, :] = (zz - zm) * lax.rsqrt(zv + EPS) * lng_ref[...] + lnb_ref[...]


def _output(x, oa, orr, w16, wpa, wpb, wo, lng, lnb, tm):
  m = x.shape[0]
  row = lambda w: pl.BlockSpec((tm, w), lambda i: (i, 0))
  full = lambda a: pl.BlockSpec(a.shape, lambda i: (0, 0), pipeline_mode=pl.Buffered(1))
  wcol = lambda off, width: pl.BlockSpec((D_MODEL, width), lambda i: (0, off // width),
                                         pipeline_mode=pl.Buffered(1))
  in_specs = [row(D_MODEL), row(A_OUT), row(R_V),
              wcol(_OFF[3], A_OUT), wcol(_OFF[7], R_V), wcol(_OFF[8], D_MODEL),
              wcol(_OFF[9], D_MODEL),
              full(wpa), full(wpb), full(wo), full(lng), full(lnb)]
  args = [x, oa, orr, w16, w16, w16, w16, wpa, wpb, wo, lng, lnb]
  return pl.pallas_call(
      _output_kernel,
      grid=(m // tm,),
      in_specs=in_specs,
      out_specs=row(D_MODEL),
      out_shape=jax.ShapeDtypeStruct((m, D_MODEL), F32),
      compiler_params=_params(("parallel",)),
      name="gates_output",
  )(*args)


def _alibi_slopes():
  n = len(A_GROUPS) * A_HEADS
  return jnp.exp2(-8.0 * jnp.arange(1, n + 1, dtype=F32) / n).reshape(len(A_GROUPS), A_HEADS)


def kernel(x_prompt, x_sample, cache_kv_w128, cache_kv_w512, cache_kv_w2048, state_ret,
           w_in, w_pa, w_pb, w_o, gn_g, ln_g, ln_b):
  bp, tp, _ = x_prompt.shape
  bs, ts, _ = x_sample.shape
  slopes = _alibi_slopes()
  w16 = w_in.astype(BF16)
  wpa = w_pa.astype(BF16)
  wpb = w_pb.astype(BF16)
  wo = w_o.astype(BF16)
  gn = gn_g.reshape(1, R_V)
  lng = ln_g.reshape(1, D_MODEL)
  lnb = ln_b.reshape(1, D_MODEL)

  xp2 = x_prompt.reshape(bp * tp, D_MODEL)
  qkvs, kv_p = [], []
  for g, (w, _) in enumerate(A_GROUPS):
    qkv, tail = _proj_group(xp2, w16, g, bp, tp)
    qkvs.append(qkv)
    kv_p.append(tail.reshape(bp, min(w, tp), 2, A_HEADS, A_HEAD_DIM))
  oa_p = _attn_prompt(qkvs, slopes, bp, tp).reshape(bp * tp, A_OUT)

  cos_p, sin_p = _rope_tables(jnp.arange(tp, dtype=jnp.int32))
  or_p, s_p = _retention(
      x_prompt, w16, cos_p, sin_p, _retention_tables(R_CHUNK, R_CHUNK),
      jnp.zeros((bp, R_HEADS, R_KEY_DIM, R_VAL_DIM), F32), gn, 1024, R_CHUNK)
  y_p = _output(xp2, oa_p, or_p.reshape(bp * tp, R_V), w16, wpa, wpb, wo, lng, lnb, 512)
  xs2 = x_sample.reshape(bs * ts, D_MODEL)
  hs = _matmul(xs2, w16, 0, _OFF[3], F32, bs * ts, A_GW)
  caches = (cache_kv_w128, cache_kv_w512, cache_kv_w2048)
  q_s, news = [], []
  for g in range(3):
    col = lambda part: hs[:, _OFF[part] + g * A_GW:_OFF[part] + (g + 1) * A_GW]
    qg = col(0).reshape(bs, ts * A_HEADS, A_HEAD_DIM)
    q_s.append(jnp.pad(qg, ((0, 0), (0, 128 - ts * A_HEADS), (0, 0))))
    news.append(jnp.concatenate([col(1), col(2)], axis=1).reshape(bs, ts, 8, 128))
  q_s = jnp.stack(q_s, axis=1)
  caches8 = [c.reshape(bs, c.shape[1], 8, 128) for c in caches]
  cpad = 16
  xs_pad = jnp.pad(x_sample, ((0, 0), (0, cpad - ts), (0, 0)))
  cos_s, sin_s = _rope_tables(PAST_LEN + jnp.arange(ts, dtype=jnp.int32))
  padtab = lambda a: jnp.pad(a, ((0, cpad - ts), (0, 0)))
  oa_s, u0, u1, u2, or_s, s_s = _attn_sample(
      q_s, caches8, news, slopes, ts,
      (xs_pad, w16, padtab(cos_s), padtab(sin_s), _retention_tables(ts, cpad),
       state_ret.astype(F32), gn))
  oa_s = oa_s[:, :ts * A_HEADS].reshape(bs * ts, A_OUT)
  kv_s = [u.reshape(bs, u.shape[1], 2, A_HEADS, A_HEAD_DIM) for u in (u0, u1, u2)]
  or_s = or_s[:, :ts].reshape(bs * ts, R_V)
  y_s = _output(xs2, oa_s, or_s, w16, wpa, wpb, wo, lng, lnb, 256)

  return (y_p.reshape(bp, tp, D_MODEL), y_s.reshape(bs, ts, D_MODEL),
          kv_p[0], kv_p[1], kv_p[2], s_p,
          kv_s[0], kv_s[1], kv_s[2], s_s)
```

```python
import functools

import numpy as np
import jax
import jax.numpy as jnp
from jax import lax
from jax.experimental import pallas as pl
from jax.experimental.pallas import tpu as pltpu

D_MODEL = 1024
PAST_LEN = 16384
A_GROUPS = ((128, 1), (512, 4), (2048, 16))
A_HEADS = 4
A_HEAD_DIM = 128
A_GW = A_HEADS * A_HEAD_DIM
A_QKV = len(A_GROUPS) * A_GW
A_OUT = A_GW
N_DIST = 128
R_HEADS = 4
R_KEY_DIM = 128
R_VAL_DIM = 256
R_QK = R_HEADS * R_KEY_DIM
R_V = R_HEADS * R_VAL_DIM
R_CHUNK = 256
ROPE_BASE = 10000.0
EPS = 1e-5
NEG_INF = -1e30
DEEPNORM_ALPHA = 2.0 ** 0.25
_SPLIT = (A_QKV, A_QKV, A_QKV, A_OUT, R_QK, R_QK, R_V, R_V, D_MODEL, D_MODEL)
_OFF = tuple(int(v) for v in np.concatenate([[0], np.cumsum(_SPLIT)]))

BF16 = jnp.bfloat16
F32 = jnp.float32
VMEM_LIMIT = 56 * 1024 * 1024

_NT = (((1,), (1,)), ((), ()))
_TN = (((0,), (0,)), ((), ()))


def _dot(a, b):
  return jnp.dot(a, b, preferred_element_type=F32)


def _dot_nt(a, b):
  return lax.dot_general(a, b, _NT, preferred_element_type=F32)


def _dot_tn(a, b):
  return lax.dot_general(a, b, _TN, preferred_element_type=F32)


def _params(sem):
  return pltpu.CompilerParams(dimension_semantics=sem, vmem_limit_bytes=VMEM_LIMIT)


def _mm_kernel(x_ref, w_ref, o_ref):
  o_ref[...] = _dot(x_ref[...].astype(BF16), w_ref[...]).astype(o_ref.dtype)


def _matmul(x, w, col0, ncols, out_dtype, tm, tn):
  m, k = x.shape
  c0 = col0 // tn
  return pl.pallas_call(
      _mm_kernel,
      grid=(m // tm, ncols // tn),
      in_specs=[pl.BlockSpec((tm, k), lambda i, j: (i, 0)),
                pl.BlockSpec((k, tn), lambda i, j: (0, c0 + j))],
      out_specs=pl.BlockSpec((tm, tn), lambda i, j: (i, j)),
      out_shape=jax.ShapeDtypeStruct((m, ncols), out_dtype),
      compiler_params=_params(("parallel", "arbitrary")),
      name="proj_matmul",
  )(x, w)


PROJ_TM = 1024


def _proj_group_kernel(x_ref, wq_ref, wk_ref, wv_ref, o_ref, tail_ref, *scratch,
                       d, first_tail, tail_rows):
  i = pl.program_id(1)
  tm = x_ref.shape[0]
  n = tm // d
  x = x_ref[...]
  xb = x.astype(BF16)
  nat = []
  if d > 1:
    xs_ref, xp_ref = scratch
    for c in range(D_MODEL // 128):
      xs_ref[c] = x[:, c * 128:(c + 1) * 128]
  nsplit = 2 if d > 1 else 1
  rper = d // nsplit
  w_refs = (wq_ref, wk_ref, wv_ref)

  def deinterleave(sb):
    for r in range(sb * rper, (sb + 1) * rper):
      for c in range(D_MODEL // 128):
        xp_ref[r * n:(r + 1) * n, c * 128:(c + 1) * 128] = (
            xs_ref[c, pl.ds(r, n, stride=d), :].astype(BF16))

  def project(sb, part):
    perm = xb if d == 1 else xp_ref[sb * rper * n:(sb + 1) * rper * n, :]
    res = _dot(perm, w_refs[part][...])
    nat.append(res)
    for r in range(sb * rper, (sb + 1) * rper):
      lo = (r - sb * rper) * n
      for h in range(A_HEADS):
        o_ref[r, part * A_HEADS + h] = res[lo:lo + n, h * 128:(h + 1) * 128].astype(BF16)

  if d == 1:
    for part in range(3):
      project(0, part)
  else:
    deinterleave(0)
    project(0, 0)
    deinterleave(1)
    project(0, 1)
    project(0, 2)
    for part in range(3):
      project(1, part)

  @pl.when(i >= first_tail)
  def _():
    if d == 1:
      kv = nat[1:]
    else:
      kv = [_dot(xb, wk_ref[...]), _dot(xb, wv_ref[...])]
    for c in range(2 * A_HEADS):
      src = kv[c // A_HEADS][tm - tail_rows:, (c % A_HEADS) * 128:(c % A_HEADS + 1) * 128]
      tail_ref[pl.ds(c, tail_rows, stride=2 * A_HEADS), :] = src


def _proj_group(x2, w16, g, batch, seq):
  w, d = A_GROUPS[g]
  w = min(w, seq)
  tm = PROJ_TM
  nblk = seq // tm
  tail_rows = min(tm, w)
  first_tail = nblk - max(w // tm, 1)
  ls = seq // d
  wspec = lambda part: pl.BlockSpec((D_MODEL, A_GW), lambda b, i: (0, 3 * part + g))
  scratch = []
  if d > 1:
    scratch = [pltpu.VMEM((D_MODEL // 128, tm, 128), F32), pltpu.VMEM((tm, D_MODEL), BF16)]
  return pl.pallas_call(
      functools.partial(_proj_group_kernel, d=d, first_tail=first_tail, tail_rows=tail_rows),
      grid=(batch, nblk),
      in_specs=[pl.BlockSpec((tm, D_MODEL), lambda b, i: (b * nblk + i, 0)),
                wspec(0), wspec(1), wspec(2)],
      out_specs=[pl.BlockSpec((None, d, 3 * A_HEADS, tm // d, 128), lambda b, i: (b, 0, 0, i, 0)),
                 pl.BlockSpec((None, tail_rows * 8, 128),
                              lambda b, i: (b, jnp.maximum(i - first_tail, 0), 0))],
      out_shape=[jax.ShapeDtypeStruct((batch, d, 3 * A_HEADS, ls, 128), BF16),
                 jax.ShapeDtypeStruct((batch, w * 8, 128), F32)],
      scratch_shapes=scratch,
      compiler_params=_params(("parallel", "arbitrary")),
      name="proj_group",
  )(x2, w16, w16, w16)


ATT_BQ = 128
ATT_SET = 4
LOG2E = 1.4426950408889634


def _attn_prompt_kernel(slopes_ref, *refs, seq):
  qkv = refs[:9]
  o_ref = refs[9]
  oacc = refs[10:13]
  lacc = refs[13:16]
  bias_ref, s_buf, p_buf, l_buf = refs[16:20]
  h = pl.program_id(0)
  scale = A_HEAD_DIM ** -0.5

  @pl.when(pl.program_id(1) == 0)
  def _():
    for g, (_, d) in enumerate(A_GROUPS):
      ls = seq // d
      bq = min(ATT_BQ, ls)
      klen = min(bq + N_DIST, ls)
      slope = slopes_ref[g, h]
      qi = lax.broadcasted_iota(jnp.int32, (bq, klen), 0)
      kj = lax.broadcasted_iota(jnp.int32, (bq, klen), 1)
      for which, off in enumerate((0, N_DIST)):
        delta = qi - kj + off
        valid = (delta >= 0) & (delta <= N_DIST)
        dist = (delta * d).astype(F32)
        bias_ref[2 * g + which, :bq, :klen] = jnp.where(valid, -(slope * LOG2E) * dist, NEG_INF)

  for g, (_, d) in enumerate(A_GROUPS):
    q_ref, k_ref, v_ref = qkv[3 * g:3 * g + 3]
    ls = seq // d
    bq = min(ATT_BQ, ls)
    nqb = ls // bq
    klen = min(bq + N_DIST, ls)

    def coords(idx, nqb=nqb, bq=bq):
      r = idx // nqb
      qb = idx % nqb
      q0 = pl.multiple_of(qb * bq, bq)
      k0 = pl.multiple_of(jnp.maximum(qb * bq - N_DIST, 0), N_DIST)
      return r, qb, q0, k0

    def put_rows(ref, idx, val, d=d, bq=bq):
      r, _, q0, _ = coords(idx)
      if d == 1:
        ref[pl.ds(q0, bq), :] = val
      else:
        ref[pl.ds(r + q0 * d, bq, stride=d), :] = val

    def scores(idx, slot, bq=bq, klen=klen, q_ref=q_ref, k_ref=k_ref):
      r, _, q0, k0 = coords(idx)
      s_buf[slot, :bq, :klen] = _dot_nt(q_ref[r, pl.ds(q0, bq), :], k_ref[r, pl.ds(k0, klen), :])

    def softmax(idx, slot, g=g, bq=bq, klen=klen):
      _, qb, _, _ = coords(idx)
      s = (s_buf[slot, :bq, :klen] * (scale * LOG2E)
           + bias_ref[2 * g + jnp.minimum(qb, 1), :bq, :klen])
      m = jnp.max(s, axis=-1, keepdims=True)
      p = jnp.exp2(s - m)
      l = jnp.sum(p, axis=-1, keepdims=True)
      p_buf[slot, :bq, :klen] = p.astype(BF16)
      l_buf[slot, :bq, :] = jnp.broadcast_to(l, (bq, 128))
      put_rows(lacc[g], idx, jnp.broadcast_to(m + jnp.log2(l), (bq, 128)))

    def values(idx, slot, g=g, bq=bq, klen=klen, v_ref=v_ref):
      r, _, _, k0 = coords(idx)
      o = _dot(p_buf[slot, :bq, :klen], v_ref[r, pl.ds(k0, klen), :]) / l_buf[slot, :bq, :]
      put_rows(oacc[g], idx, o)

    nblk = d * nqb
    ns = ATT_SET

    def run(stage, first, half):
      for t in range(ns):
        stage(first + t, half * ns + t)

    run(scores, 0, 0)
    run(softmax, 0, 0)
    run(scores, ns, 1)

    def steady(j, carry):
      i = 2 * ns * (j + 1)
      run(values, i - 2 * ns, 0)
      run(softmax, i - ns, 1)
      run(scores, i, 0)
      run(values, i - ns, 1)
      run(softmax, i, 0)
      run(scores, i + ns, 1)
      return carry

    lax.fori_loop(0, (nblk - 2 * ns) // (2 * ns), steady, 0)
    run(values, nblk - 2 * ns, 0)
    run(softmax, nblk - ns, 1)
    run(values, nblk - ns, 1)

  mrows = 512

  def merge(i, carry):
    r0 = pl.multiple_of(i * mrows, mrows)
    l0 = lacc[0][pl.ds(r0, mrows), :]
    l1 = lacc[1][pl.ds(r0, mrows), :]
    l2 = lacc[2][pl.ds(r0, mrows), :]
    mx = jnp.maximum(jnp.maximum(l0, l1), l2)
    e0 = jnp.exp2(l0 - mx)
    e1 = jnp.exp2(l1 - mx)
    e2 = jnp.exp2(l2 - mx)
    den = e0 + e1 + e2
    acc = (e0 * oacc[0][pl.ds(r0, mrows), :] + e1 * oacc[1][pl.ds(r0, mrows), :]
           + e2 * oacc[2][pl.ds(r0, mrows), :])
    o_ref[pl.ds(r0, mrows), :] = acc / den
    return carry

  lax.fori_loop(0, seq // mrows, merge, 0)


def _attn_prompt(qkvs, slopes, batch, seq):
  in_specs = [pl.BlockSpec(memory_space=pltpu.SMEM)]
  args = [slopes]
  for g, (_, d) in enumerate(A_GROUPS):
    ls = seq // d
    for part in range(3):
      in_specs.append(pl.BlockSpec(
          (None, d, None, ls, 128),
          lambda h, b, part=part: (b, 0, part * A_HEADS + h, 0, 0)))
      args.append(qkvs[g])
  scratch = ([pltpu.VMEM((seq, 128), F32) for _ in range(6)]
             + [pltpu.VMEM((6, ATT_BQ, ATT_BQ + N_DIST), F32),
                pltpu.VMEM((2 * ATT_SET, ATT_BQ, ATT_BQ + N_DIST), F32),
                pltpu.VMEM((2 * ATT_SET, ATT_BQ, ATT_BQ + N_DIST), BF16),
                pltpu.VMEM((2 * ATT_SET, ATT_BQ, 128), F32)])
  return pl.pallas_call(
      functools.partial(_attn_prompt_kernel, seq=seq),
      grid=(A_HEADS, batch),
      in_specs=in_specs,
      out_specs=pl.BlockSpec((None, seq, 128), lambda h, b: (b, 0, h)),
      out_shape=jax.ShapeDtypeStruct((batch, seq, A_OUT), F32),
      scratch_shapes=scratch,
      compiler_params=_params(("parallel", "arbitrary")),
      name="attn_prompt",
  )(*args)


def _attn_sample_kernel(slopes_ref, q_ref, c0_ref, c1_ref, c2_ref, n0_ref, n1_ref, n2_ref,
                        o_ref, u0_ref, u1_ref, u2_ref, sems, bias_ref, *, tq):
  b = pl.program_id(0)
  cache_refs = (c0_ref, c1_ref, c2_ref)
  new_refs = (n0_ref, n1_ref, n2_ref)
  out_refs = (u0_ref, u1_ref, u2_ref)

  copies = []
  for g, (w, _) in enumerate(A_GROUPS):
    body = pltpu.make_async_copy(cache_refs[g].at[0, pl.ds(tq, w - tq)],
                                 out_refs[g].at[b, pl.ds(0, w - tq)], sems.at[2 * g])
    tail = pltpu.make_async_copy(new_refs[g].at[0], out_refs[g].at[b, pl.ds(w - tq, tq)],
                                 sems.at[2 * g + 1])
    body.start()
    tail.start()
    copies += [body, tail]

  scale = A_HEAD_DIM ** -0.5
  nreal = N_DIST * 8 + tq * 8
  nrow = -(-nreal // 128) * 128

  @pl.when(b == 0)
  def _():
    ri = lax.broadcasted_iota(jnp.int32, (nrow, 128), 0)
    cj = lax.broadcasted_iota(jnp.int32, (nrow, 128), 1)
    slot = ri % 8
    key = ri // 8
    tcol = cj // A_HEADS
    hcol = cj % A_HEADS
    col_ok = cj < tq * A_HEADS
    for g, (_, d) in enumerate(A_GROUPS):
      qt = tcol // d
      nidx = key - N_DIST
      in_cache = key < N_DIST
      valid_c = in_cache & (key >= qt)
      dist_c = d * (N_DIST + qt - key)
      valid_n = (~in_cache) & (nidx <= tcol) & ((tcol - nidx) % d == 0)
      dist_n = tcol - nidx
      valid = (valid_c | valid_n) & (slot == hcol) & col_ok & (ri < nreal)
      dist = jnp.where(in_cache, dist_c, dist_n).astype(F32)
      slope = jnp.zeros((nrow, 128), F32)
      for hh in range(A_HEADS):
        slope = jnp.where(hcol == hh, slopes_ref[g, hh], slope)
      bias_ref[g] = jnp.where(valid, -slope * dist, NEG_INF)

  qrow_t = lax.broadcasted_iota(jnp.int32, (128, 1), 0) // A_HEADS
  xpad = jnp.zeros((nrow - nreal, 128), F32)
  lse_g = []
  o_g = []
  for g, (w, d) in enumerate(A_GROUPS):
    qg = q_ref[g]
    xn = new_refs[g][0].reshape(tq * 8, 128)
    nres = min(d, tq)
    xs, qs = [], []
    for rho in range(nres):
      if d == 1:
        xc = cache_refs[g][0]
      else:
        xc = cache_refs[g][0, pl.ds(rho, N_DIST, stride=d), :, :]
      xc = xc.reshape(N_DIST * 8, 128)
      xs.append(jnp.concatenate([xc, xn, xpad], axis=0).astype(BF16))
      qs.append(qg if d == 1 else jnp.where(qrow_t % d == rho, qg, 0.0))
    xcat = xs[0] if nres == 1 else jnp.concatenate(xs, axis=1)
    qcat = (qs[0] if nres == 1 else jnp.concatenate(qs, axis=1)).astype(BF16)
    s = _dot_nt(xcat, qcat) * scale + bias_ref[g]
    m = jnp.max(s, axis=0, keepdims=True)
    p = jnp.exp(s - m)
    l = jnp.sum(p, axis=0, keepdims=True)
    pn = p / l
    pvt = pltpu.roll(pn, 4, 0).T.astype(BF16)
    full = _dot(pvt, xcat)
    o = full[:, :128]
    for rho in range(1, nres):
      o = jnp.where(qrow_t % d == rho, full[:, rho * 128:(rho + 1) * 128], o)
    o_g.append(o)
    lse_g.append(jnp.broadcast_to(m + jnp.log(l), (128, 128)).T)
  mx = jnp.maximum(jnp.maximum(lse_g[0], lse_g[1]), lse_g[2])
  es = [jnp.exp(v - mx) for v in lse_g]
  den = es[0] + es[1] + es[2]
  o_ref[...] = (es[0] * o_g[0] + es[1] * o_g[1] + es[2] * o_g[2]) / den

  for c in copies:
    c.wait()


def _attn_sample(q, caches, news, slopes, tq):
  batch = q.shape[0]
  in_specs = [pl.BlockSpec(memory_space=pltpu.SMEM),
              pl.BlockSpec((None, 3, 128, 128), lambda b: (b, 0, 0, 0))]
  args = [slopes, q]
  for g, (w, d) in enumerate(A_GROUPS):
    in_specs.append(pl.BlockSpec((1, w, 8, 128), lambda b: (b, 0, 0, 0)))
    args.append(caches[g])
  for g in range(3):
    in_specs.append(pl.BlockSpec((1, tq, 8, 128), lambda b: (b, 0, 0, 0)))
    args.append(news[g])
  out_shape = [jax.ShapeDtypeStruct((batch, 128, 128), F32)]
  out_specs = [pl.BlockSpec((None, 128, 128), lambda b: (b, 0, 0))]
  for g, (w, _) in enumerate(A_GROUPS):
    out_shape.append(jax.ShapeDtypeStruct((batch, w, 8, 128), F32))
    out_specs.append(pl.BlockSpec(memory_space=pl.ANY))
  return pl.pallas_call(
      functools.partial(_attn_sample_kernel, tq=tq),
      grid=(batch,),
      in_specs=in_specs,
      out_specs=out_specs,
      out_shape=out_shape,
      scratch_shapes=[pltpu.SemaphoreType.DMA((2 * len(A_GROUPS),)),
                      pltpu.VMEM((len(A_GROUPS), -(-(N_DIST + tq) * 8 // 128) * 128, 128), F32)],
      compiler_params=_params(("arbitrary",)),
      name="attn_sample",
  )(*args)


def _retention_rows(x_ref, wqk_ref, wv_ref, cos_ref, sin_ref, dmat_ref, qdec_ref, kdec_ref,
                    cdec_ref, gn_ref, o_ref, state, qk_ref, v_ref, chunk):
  xb = x_ref[...].astype(BF16)
  qk_ref[...] = _dot(xb, wqk_ref[...])
  v_ref[...] = _dot(xb, wv_ref[...]).astype(BF16)
  kscale = R_KEY_DIM ** -0.5
  for ci in range(x_ref.shape[0] // chunk):
    rows = pl.ds(ci * chunk, chunk)
    cosv = cos_ref[rows, :]
    sinv = sin_ref[rows, :]
    for h in range(R_HEADS):
      qh = qk_ref[rows, h * 128:(h + 1) * 128]
      kh = qk_ref[rows, R_QK + h * 128:R_QK + (h + 1) * 128]
      qrot = qh * cosv + pltpu.roll(qh, 64, 1) * sinv
      krot = (kh * cosv + pltpu.roll(kh, 64, 1) * sinv) * kscale
      qb = qrot.astype(BF16)
      vh = v_ref[rows, h * 256:(h + 1) * 256]
      att = _dot_nt(qb, krot.astype(BF16)) * dmat_ref[h]
      sh = state[h]
      o = _dot(att.astype(BF16), vh) + _dot(qb, sh.astype(BF16)) * qdec_ref[h]
      state[h] = cdec_ref[h] * sh + _dot_tn((krot * kdec_ref[h]).astype(BF16), vh)
      mu = jnp.mean(o, axis=-1, keepdims=True)
      var = jnp.mean(jnp.square(o - mu), axis=-1, keepdims=True)
      on = (o - mu) * lax.rsqrt(var + EPS) * gn_ref[:, h * 256:(h + 1) * 256]
      o_ref[rows, h * 256:(h + 1) * 256] = on


def _retention_kernel(x_ref, wqk_ref, wv_ref, cos_ref, sin_ref, dmat_ref, qdec_ref, kdec_ref,
                      cdec_ref, s0_ref, gn_ref, o_ref, sout_ref, state, qk_ref, v_ref, *, chunk):
  j = pl.program_id(1)

  @pl.when(j == 0)
  def _():
    state[...] = s0_ref[...]

  _retention_rows(x_ref, wqk_ref, wv_ref, cos_ref, sin_ref, dmat_ref, qdec_ref, kdec_ref,
                  cdec_ref, gn_ref, o_ref, state, qk_ref, v_ref, chunk)

  @pl.when(j == pl.num_programs(1) - 1)
  def _():
    sout_ref[...] = state[...]


def _retention(x, w16, cos, sin, tables, s0, gn, rows_per_step, chunk):
  batch, t, _ = x.shape
  dmat, qdec, kdec, cdec = tables
  tc = rows_per_step
  const3 = lambda b, j: (0, 0, 0)
  wslab = lambda off: pl.BlockSpec((D_MODEL, R_V), lambda b, j: (0, off // R_V),
                                   pipeline_mode=pl.Buffered(1))
  return pl.pallas_call(
      functools.partial(_retention_kernel, chunk=chunk),
      grid=(batch, t // tc),
      in_specs=[pl.BlockSpec((None, tc, D_MODEL), lambda b, j: (b, j, 0)),
                wslab(_OFF[4]), wslab(_OFF[6]),
                pl.BlockSpec((tc, 128), lambda b, j: (j, 0)),
                pl.BlockSpec((tc, 128), lambda b, j: (j, 0)),
                pl.BlockSpec(dmat.shape, const3),
                pl.BlockSpec(qdec.shape, const3),
                pl.BlockSpec(kdec.shape, const3),
                pl.BlockSpec(cdec.shape, const3),
                pl.BlockSpec((None, R_HEADS, R_KEY_DIM, R_VAL_DIM), lambda b, j: (b, 0, 0, 0)),
                pl.BlockSpec((1, R_V), lambda b, j: (0, 0))],
      out_specs=[pl.BlockSpec((None, tc, R_V), lambda b, j: (b, j, 0)),
                 pl.BlockSpec((None, R_HEADS, R_KEY_DIM, R_VAL_DIM), lambda b, j: (b, 0, 0, 0))],
      out_shape=[jax.ShapeDtypeStruct((batch, t, R_V), F32),
                 jax.ShapeDtypeStruct((batch, R_HEADS, R_KEY_DIM, R_VAL_DIM), F32)],
      scratch_shapes=[pltpu.VMEM((R_HEADS, R_KEY_DIM, R_VAL_DIM), F32),
                      pltpu.VMEM((tc, 2 * R_QK), F32), pltpu.VMEM((tc, R_V), BF16)],
      compiler_params=_params(("parallel", "arbitrary")),
      name="retention",
  )(x, w16, w16, cos, sin, dmat, qdec, kdec, cdec, s0, gn)


def _retention_tables(c_true, c_pad):
  lg = jnp.log1p(-jnp.exp2(-5.0 - jnp.arange(R_HEADS, dtype=F32)))
  i = jnp.arange(c_pad, dtype=F32)
  live = i < c_true
  diff = i[:, None] - i[None, :]
  causal = (diff >= 0) & live[:, None] & live[None, :]
  dmat = jnp.where(causal[None], jnp.exp(jnp.where(causal, diff, 0.0)[None] * lg[:, None, None]), 0.0)
  qdec = jnp.where(live[None, :], jnp.exp((i[None, :] + 1.0) * lg[:, None]), 0.0)
  kdec = jnp.where(live[None, :], jnp.exp((c_true - 1.0 - i)[None, :] * lg[:, None]), 0.0)
  cdec = jnp.exp(c_true * lg)
  qdec = jnp.broadcast_to(qdec[:, :, None], (R_HEADS, c_pad, R_VAL_DIM))
  kdec = jnp.broadcast_to(kdec[:, :, None], (R_HEADS, c_pad, R_KEY_DIM))
  cdec = jnp.broadcast_to(cdec[:, None, None], (R_HEADS, R_KEY_DIM, R_VAL_DIM))
  return dmat, qdec, kdec, cdec


def _rope_tables(pos):
  half = R_KEY_DIM // 2
  inv = 1.0 / (ROPE_BASE ** jnp.linspace(0.0, 1.0, half, dtype=F32))
  ang = pos.astype(F32)[:, None] * inv[None, :]
  cos = jnp.cos(ang)
  sin = jnp.sin(ang)
  return jnp.concatenate([cos, cos], axis=-1), jnp.concatenate([-sin, sin], axis=-1)


def _output_kernel(x_ref, oa_ref, or_ref, wza_ref, wzr_ref, wga_ref, wgb_ref, wpa_ref, wpb_ref,
                   wo_ref, lng_ref, lnb_ref, y_ref):
  tm = x_ref.shape[0]
  nsub = 2 if tm % 32 == 0 else 1
  sub = tm // nsub
  for si in range(nsub):
    rows = pl.ds(si * sub, sub)
    x = x_ref[rows, :]
    xb = x.astype(BF16)
    za = _dot(xb, wza_ref[...])
    ya = _dot((jax.nn.silu(za) * oa_ref[rows, :]).astype(BF16), wpa_ref[...])
    zr = _dot(xb, wzr_ref[...])
    yb = _dot((jax.nn.silu(zr) * or_ref[rows, :]).astype(BF16), wpb_ref[...])
    ga = _dot(xb, wga_ref[...])
    gb = _dot(xb, wgb_ref[...])
    mix = jax.nn.sigmoid(ga) * ya + jax.nn.sigmoid(gb) * yb
    out = _dot(mix.astype(BF16), wo_ref[...])
    zz = DEEPNORM_ALPHA * x + out
    zm = jnp.mean(zz, axis=-1, keepdims=True)
    zv = jnp.mean(jnp.square(zz - zm), axis=-1, keepdims=True)
    y_ref[rows, :] = (zz - zm) * lax.rsqrt(zv + EPS) * lng_ref[...] + lnb_ref[...]


def _output(x, oa, orr, w16, wpa, wpb, wo, lng, lnb, tm):
  m = x.shape[0]
  row = lambda w: pl.BlockSpec((tm, w), lambda i: (i, 0))
  full = lambda a: pl.BlockSpec(a.shape, lambda i: (0, 0), pipeline_mode=pl.Buffered(1))
  wcol = lambda off, width: pl.BlockSpec((D_MODEL, width), lambda i: (0, off // width),
                                         pipeline_mode=pl.Buffered(1))
  in_specs = [row(D_MODEL), row(A_OUT), row(R_V),
              wcol(_OFF[3], A_OUT), wcol(_OFF[7], R_V), wcol(_OFF[8], D_MODEL),
              wcol(_OFF[9], D_MODEL),
              full(wpa), full(wpb), full(wo), full(lng), full(lnb)]
  args = [x, oa, orr, w16, w16, w16, w16, wpa, wpb, wo, lng, lnb]
  return pl.pallas_call(
      _output_kernel,
      grid=(m // tm,),
      in_specs=in_specs,
      out_specs=row(D_MODEL),
      out_shape=jax.ShapeDtypeStruct((m, D_MODEL), F32),
      compiler_params=_params(("parallel",)),
      name="gates_output",
  )(*args)


def _alibi_slopes():
  n = len(A_GROUPS) * A_HEADS
  return jnp.exp2(-8.0 * jnp.arange(1, n + 1, dtype=F32) / n).reshape(len(A_GROUPS), A_HEADS)


def kernel(x_prompt, x_sample, cache_kv_w128, cache_kv_w512, cache_kv_w2048, state_ret,
           w_in, w_pa, w_pb, w_o, gn_g, ln_g, ln_b):
  bp, tp, _ = x_prompt.shape
  bs, ts, _ = x_sample.shape
  slopes = _alibi_slopes()
  w16 = w_in.astype(BF16)
  wpa = w_pa.astype(BF16)
  wpb = w_pb.astype(BF16)
  wo = w_o.astype(BF16)
  gn = gn_g.reshape(1, R_V)
  lng = ln_g.reshape(1, D_MODEL)
  lnb = ln_b.reshape(1, D_MODEL)

  xp2 = x_prompt.reshape(bp * tp, D_MODEL)
  qkvs, kv_p = [], []
  for g, (w, _) in enumerate(A_GROUPS):
    qkv, tail = _proj_group(xp2, w16, g, bp, tp)
    qkvs.append(qkv)
    kv_p.append(tail.reshape(bp, min(w, tp), 2, A_HEADS, A_HEAD_DIM))
  oa_p = _attn_prompt(qkvs, slopes, bp, tp).reshape(bp * tp, A_OUT)

  cos_p, sin_p = _rope_tables(jnp.arange(tp, dtype=jnp.int32))
  or_p, s_p = _retention(
      x_prompt, w16, cos_p, sin_p, _retention_tables(R_CHUNK, R_CHUNK),
      jnp.zeros((bp, R_HEADS, R_KEY_DIM, R_VAL_DIM), F32), gn, 1024, R_CHUNK)
  y_p = _output(xp2, oa_p, or_p.reshape(bp * tp, R_V), w16, wpa, wpb, wo, lng, lnb, 512)
  xs2 = x_sample.reshape(bs * ts, D_MODEL)
  hs = _matmul(xs2, w16, 0, _OFF[3], F32, bs * ts, A_GW)
  caches = (cache_kv_w128, cache_kv_w512, cache_kv_w2048)
  q_s, news = [], []
  for g in range(3):
    col = lambda part: hs[:, _OFF[part] + g * A_GW:_OFF[part] + (g + 1) * A_GW]
    qg = col(0).reshape(bs, ts * A_HEADS, A_HEAD_DIM)
    q_s.append(jnp.pad(qg, ((0, 0), (0, 128 - ts * A_HEADS), (0, 0))))
    news.append(jnp.concatenate([col(1), col(2)], axis=1).reshape(bs, ts, 8, 128))
  q_s = jnp.stack(q_s, axis=1)
  caches8 = [c.reshape(bs, c.shape[1], 8, 128) for c in caches]
  cpad = 16
  xs_pad = jnp.pad(x_sample, ((0, 0), (0, cpad - ts), (0, 0)))
  cos_s, sin_s = _rope_tables(PAST_LEN + jnp.arange(ts, dtype=jnp.int32))
  padtab = lambda a: jnp.pad(a, ((0, cpad - ts), (0, 0)))
  oa_s, u0, u1, u2 = _attn_sample(q_s, caches8, news, slopes, ts)
  oa_s = oa_s[:, :ts * A_HEADS].reshape(bs * ts, A_OUT)
  kv_s = [u.reshape(bs, u.shape[1], 2, A_HEADS, A_HEAD_DIM) for u in (u0, u1, u2)]
  or_s, s_s = _retention(
      xs_pad, w16, padtab(cos_s), padtab(sin_s),
      _retention_tables(ts, cpad), state_ret.astype(F32), gn, cpad, cpad)
  or_s = or_s[:, :ts].reshape(bs * ts, R_V)
  y_s = _output(xs2, oa_s, or_s, w16, wpa, wpb, wo, lng, lnb, 256)

  return (y_p.reshape(bp, tp, D_MODEL), y_s.reshape(bs, ts, D_MODEL),
          kv_p[0], kv_p[1], kv_p[2], s_p,
          kv_s[0], kv_s[1], kv_s[2], s_s)
```

```python
import functools

import numpy as np
import jax
import jax.numpy as jnp
from jax import lax
from jax.experimental import pallas as pl
from jax.experimental.pallas import tpu as pltpu

D_MODEL = 1024
PAST_LEN = 16384
A_GROUPS = ((128, 1), (512, 4), (2048, 16))
A_HEADS = 4
A_HEAD_DIM = 128
A_GW = A_HEADS * A_HEAD_DIM
A_QKV = len(A_GROUPS) * A_GW
A_OUT = A_GW
N_DIST = 128
R_HEADS = 4
R_KEY_DIM = 128
R_VAL_DIM = 256
R_QK = R_HEADS * R_KEY_DIM
R_V = R_HEADS * R_VAL_DIM
R_CHUNK = 256
ROPE_BASE = 10000.0
EPS = 1e-5
NEG_INF = -1e30
DEEPNORM_ALPHA = 2.0 ** 0.25
_SPLIT = (A_QKV, A_QKV, A_QKV, A_OUT, R_QK, R_QK, R_V, R_V, D_MODEL, D_MODEL)
_OFF = tuple(int(v) for v in np.concatenate([[0], np.cumsum(_SPLIT)]))

BF16 = jnp.bfloat16
F32 = jnp.float32
VMEM_LIMIT = 56 * 1024 * 1024

_NT = (((1,), (1,)), ((), ()))
_TN = (((0,), (0,)), ((), ()))


def _dot(a, b):
  return jnp.dot(a, b, preferred_element_type=F32)


def _dot_nt(a, b):
  return lax.dot_general(a, b, _NT, preferred_element_type=F32)


def _dot_tn(a, b):
  return lax.dot_general(a, b, _TN, preferred_element_type=F32)


def _params(sem):
  return pltpu.CompilerParams(dimension_semantics=sem, vmem_limit_bytes=VMEM_LIMIT)


def _mm_kernel(x_ref, w_ref, o_ref):
  o_ref[...] = _dot(x_ref[...].astype(BF16), w_ref[...]).astype(o_ref.dtype)


def _matmul(x, w, col0, ncols, out_dtype, tm, tn):
  m, k = x.shape
  c0 = col0 // tn
  return pl.pallas_call(
      _mm_kernel,
      grid=(m // tm, ncols // tn),
      in_specs=[pl.BlockSpec((tm, k), lambda i, j: (i, 0)),
                pl.BlockSpec((k, tn), lambda i, j: (0, c0 + j))],
      out_specs=pl.BlockSpec((tm, tn), lambda i, j: (i, j)),
      out_shape=jax.ShapeDtypeStruct((m, ncols), out_dtype),
      compiler_params=_params(("parallel", "arbitrary")),
      name="proj_matmul",
  )(x, w)


PROJ_TM = 1024


def _proj_group_kernel(x_ref, wq_ref, wk_ref, wv_ref, o_ref, tail_ref, *scratch,
                       d, first_tail, tail_rows):
  i = pl.program_id(1)
  tm = x_ref.shape[0]
  n = tm // d
  x = x_ref[...]
  xb = x.astype(BF16)
  nat = []
  if d > 1:
    xs_ref, xp_ref = scratch
    for c in range(D_MODEL // 128):
      xs_ref[c] = x[:, c * 128:(c + 1) * 128]
  nsplit = 2 if d > 1 else 1
  rper = d // nsplit
  w_refs = (wq_ref, wk_ref, wv_ref)

  def deinterleave(sb):
    for r in range(sb * rper, (sb + 1) * rper):
      for c in range(D_MODEL // 128):
        xp_ref[r * n:(r + 1) * n, c * 128:(c + 1) * 128] = (
            xs_ref[c, pl.ds(r, n, stride=d), :].astype(BF16))

  def project(sb, part):
    perm = xb if d == 1 else xp_ref[sb * rper * n:(sb + 1) * rper * n, :]
    res = _dot(perm, w_refs[part][...])
    nat.append(res)
    for r in range(sb * rper, (sb + 1) * rper):
      lo = (r - sb * rper) * n
      for h in range(A_HEADS):
        o_ref[r, part * A_HEADS + h] = res[lo:lo + n, h * 128:(h + 1) * 128].astype(BF16)

  if d == 1:
    for part in range(3):
      project(0, part)
  else:
    deinterleave(0)
    project(0, 0)
    deinterleave(1)
    project(0, 1)
    project(0, 2)
    for part in range(3):
      project(1, part)

  @pl.when(i >= first_tail)
  def _():
    if d == 1:
      kv = nat[1:]
    else:
      kv = [_dot(xb, wk_ref[...]), _dot(xb, wv_ref[...])]
    for c in range(2 * A_HEADS):
      src = kv[c // A_HEADS][tm - tail_rows:, (c % A_HEADS) * 128:(c % A_HEADS + 1) * 128]
      tail_ref[pl.ds(c, tail_rows, stride=2 * A_HEADS), :] = src


def _proj_group(x2, w16, g, batch, seq):
  w, d = A_GROUPS[g]
  w = min(w, seq)
  tm = PROJ_TM
  nblk = seq // tm
  tail_rows = min(tm, w)
  first_tail = nblk - max(w // tm, 1)
  ls = seq // d
  wspec = lambda part: pl.BlockSpec((D_MODEL, A_GW), lambda b, i: (0, 3 * part + g))
  scratch = []
  if d > 1:
    scratch = [pltpu.VMEM((D_MODEL // 128, tm, 128), F32), pltpu.VMEM((tm, D_MODEL), BF16)]
  return pl.pallas_call(
      functools.partial(_proj_group_kernel, d=d, first_tail=first_tail, tail_rows=tail_rows),
      grid=(batch, nblk),
      in_specs=[pl.BlockSpec((tm, D_MODEL), lambda b, i: (b * nblk + i, 0)),
                wspec(0), wspec(1), wspec(2)],
      out_specs=[pl.BlockSpec((None, d, 3 * A_HEADS, tm // d, 128), lambda b, i: (b, 0, 0, i, 0)),
                 pl.BlockSpec((None, tail_rows * 8, 128),
                              lambda b, i: (b, jnp.maximum(i - first_tail, 0), 0))],
      out_shape=[jax.ShapeDtypeStruct((batch, d, 3 * A_HEADS, ls, 128), BF16),
                 jax.ShapeDtypeStruct((batch, w * 8, 128), F32)],
      scratch_shapes=scratch,
      compiler_params=_params(("parallel", "arbitrary")),
      name="proj_group",
  )(x2, w16, w16, w16)


ATT_BQ = 128
ATT_SET = 4
LOG2E = 1.4426950408889634


def _attn_prompt_kernel(slopes_ref, *refs, seq):
  qkv = refs[:9]
  o_ref = refs[9]
  oacc = refs[10:13]
  lacc = refs[13:16]
  bias_ref, s_buf, p_buf, l_buf = refs[16:20]
  h = pl.program_id(0)
  scale = A_HEAD_DIM ** -0.5

  @pl.when(pl.program_id(1) == 0)
  def _():
    for g, (_, d) in enumerate(A_GROUPS):
      ls = seq // d
      bq = min(ATT_BQ, ls)
      klen = min(bq + N_DIST, ls)
      slope = slopes_ref[g, h]
      qi = lax.broadcasted_iota(jnp.int32, (bq, klen), 0)
      kj = lax.broadcasted_iota(jnp.int32, (bq, klen), 1)
      for which, off in enumerate((0, N_DIST)):
        delta = qi - kj + off
        valid = (delta >= 0) & (delta <= N_DIST)
        dist = (delta * d).astype(F32)
        bias_ref[2 * g + which, :bq, :klen] = jnp.where(valid, -(slope * LOG2E) * dist, NEG_INF)

  for g, (_, d) in enumerate(A_GROUPS):
    q_ref, k_ref, v_ref = qkv[3 * g:3 * g + 3]
    ls = seq // d
    bq = min(ATT_BQ, ls)
    nqb = ls // bq
    klen = min(bq + N_DIST, ls)

    def coords(idx, nqb=nqb, bq=bq):
      r = idx // nqb
      qb = idx % nqb
      q0 = pl.multiple_of(qb * bq, bq)
      k0 = pl.multiple_of(jnp.maximum(qb * bq - N_DIST, 0), N_DIST)
      return r, qb, q0, k0

    def put_rows(ref, idx, val, d=d, bq=bq):
      r, _, q0, _ = coords(idx)
      if d == 1:
        ref[pl.ds(q0, bq), :] = val
      else:
        ref[pl.ds(r + q0 * d, bq, stride=d), :] = val

    def scores(idx, slot, bq=bq, klen=klen, q_ref=q_ref, k_ref=k_ref):
      r, _, q0, k0 = coords(idx)
      s_buf[slot, :bq, :klen] = _dot_nt(q_ref[r, pl.ds(q0, bq), :], k_ref[r, pl.ds(k0, klen), :])

    def softmax(idx, slot, g=g, bq=bq, klen=klen):
      _, qb, _, _ = coords(idx)
      s = (s_buf[slot, :bq, :klen] * (scale * LOG2E)
           + bias_ref[2 * g + jnp.minimum(qb, 1), :bq, :klen])
      m = jnp.max(s, axis=-1, keepdims=True)
      p = jnp.exp2(s - m)
      l = jnp.sum(p, axis=-1, keepdims=True)
      p_buf[slot, :bq, :klen] = p.astype(BF16)
      l_buf[slot, :bq, :] = jnp.broadcast_to(l, (bq, 128))
      put_rows(lacc[g], idx, jnp.broadcast_to(m + jnp.log2(l), (bq, 128)))

    def values(idx, slot, g=g, bq=bq, klen=klen, v_ref=v_ref):
      r, _, _, k0 = coords(idx)
      o = _dot(p_buf[slot, :bq, :klen], v_ref[r, pl.ds(k0, klen), :]) / l_buf[slot, :bq, :]
      put_rows(oacc[g], idx, o)

    nblk = d * nqb
    ns = ATT_SET

    def run(stage, first, half):
      for t in range(ns):
        stage(first + t, half * ns + t)

    run(scores, 0, 0)
    run(softmax, 0, 0)
    run(scores, ns, 1)

    def steady(j, carry):
      i = 2 * ns * (j + 1)
      run(values, i - 2 * ns, 0)
      run(softmax, i - ns, 1)
      run(scores, i, 0)
      run(values, i - ns, 1)
      run(softmax, i, 0)
      run(scores, i + ns, 1)
      return carry

    lax.fori_loop(0, (nblk - 2 * ns) // (2 * ns), steady, 0)
    run(values, nblk - 2 * ns, 0)
    run(softmax, nblk - ns, 1)
    run(values, nblk - ns, 1)

  mrows = 512

  def merge(i, carry):
    r0 = pl.multiple_of(i * mrows, mrows)
    l0 = lacc[0][pl.ds(r0, mrows), :]
    l1 = lacc[1][pl.ds(r0, mrows), :]
    l2 = lacc[2][pl.ds(r0, mrows), :]
    mx = jnp.maximum(jnp.maximum(l0, l1), l2)
    e0 = jnp.exp2(l0 - mx)
    e1 = jnp.exp2(l1 - mx)
    e2 = jnp.exp2(l2 - mx)
    den = e0 + e1 + e2
    acc = (e0 * oacc[0][pl.ds(r0, mrows), :] + e1 * oacc[1][pl.ds(r0, mrows), :]
           + e2 * oacc[2][pl.ds(r0, mrows), :])
    o_ref[pl.ds(r0, mrows), :] = acc / den
    return carry

  lax.fori_loop(0, seq // mrows, merge, 0)


def _attn_prompt(qkvs, slopes, batch, seq):
  in_specs = [pl.BlockSpec(memory_space=pltpu.SMEM)]
  args = [slopes]
  for g, (_, d) in enumerate(A_GROUPS):
    ls = seq // d
    for part in range(3):
      in_specs.append(pl.BlockSpec(
          (None, d, None, ls, 128),
          lambda h, b, part=part: (b, 0, part * A_HEADS + h, 0, 0)))
      args.append(qkvs[g])
  scratch = ([pltpu.VMEM((seq, 128), F32) for _ in range(6)]
             + [pltpu.VMEM((6, ATT_BQ, ATT_BQ + N_DIST), F32),
                pltpu.VMEM((2 * ATT_SET, ATT_BQ, ATT_BQ + N_DIST), F32),
                pltpu.VMEM((2 * ATT_SET, ATT_BQ, ATT_BQ + N_DIST), BF16),
                pltpu.VMEM((2 * ATT_SET, ATT_BQ, 128), F32)])
  return pl.pallas_call(
      functools.partial(_attn_prompt_kernel, seq=seq),
      grid=(A_HEADS, batch),
      in_specs=in_specs,
      out_specs=pl.BlockSpec((None, seq, 128), lambda h, b: (b, 0, h)),
      out_shape=jax.ShapeDtypeStruct((batch, seq, A_OUT), F32),
      scratch_shapes=scratch,
      compiler_params=_params(("parallel", "arbitrary")),
      name="attn_prompt",
  )(*args)


def _attn_sample_kernel(slopes_ref, q_ref, c0_ref, c1_ref, c2_ref, n0_ref, n1_ref, n2_ref,
                        o_ref, u0_ref, u1_ref, u2_ref, sems, bias_ref, *, tq):
  b = pl.program_id(0)
  cache_refs = (c0_ref, c1_ref, c2_ref)
  new_refs = (n0_ref, n1_ref, n2_ref)
  out_refs = (u0_ref, u1_ref, u2_ref)

  copies = []
  for g, (w, _) in enumerate(A_GROUPS):
    body = pltpu.make_async_copy(cache_refs[g].at[0, pl.ds(tq, w - tq)],
                                 out_refs[g].at[b, pl.ds(0, w - tq)], sems.at[2 * g])
    tail = pltpu.make_async_copy(new_refs[g].at[0], out_refs[g].at[b, pl.ds(w - tq, tq)],
                                 sems.at[2 * g + 1])
    body.start()
    tail.start()
    copies += [body, tail]

  scale = A_HEAD_DIM ** -0.5
  nreal = N_DIST * 8 + tq * 8
  nrow = -(-nreal // 128) * 128

  @pl.when(b == 0)
  def _():
    ri = lax.broadcasted_iota(jnp.int32, (nrow, 128), 0)
    cj = lax.broadcasted_iota(jnp.int32, (nrow, 128), 1)
    slot = ri % 8
    key = ri // 8
    tcol = cj // A_HEADS
    hcol = cj % A_HEADS
    col_ok = cj < tq * A_HEADS
    for g, (_, d) in enumerate(A_GROUPS):
      qt = tcol // d
      nidx = key - N_DIST
      in_cache = key < N_DIST
      valid_c = in_cache & (key >= qt)
      dist_c = d * (N_DIST + qt - key)
      valid_n = (~in_cache) & (nidx <= tcol) & ((tcol - nidx) % d == 0)
      dist_n = tcol - nidx
      valid = (valid_c | valid_n) & (slot == hcol) & col_ok & (ri < nreal)
      dist = jnp.where(in_cache, dist_c, dist_n).astype(F32)
      slope = jnp.zeros((nrow, 128), F32)
      for hh in range(A_HEADS):
        slope = jnp.where(hcol == hh, slopes_ref[g, hh], slope)
      bias_ref[g] = jnp.where(valid, -slope * dist, NEG_INF)

  qrow_t = lax.broadcasted_iota(jnp.int32, (128, 1), 0) // A_HEADS
  xpad = jnp.zeros((nrow - nreal, 128), F32)
  lse_g = []
  o_g = []
  for g, (w, d) in enumerate(A_GROUPS):
    qg = q_ref[g]
    xn = new_refs[g][0].reshape(tq * 8, 128)
    nres = min(d, tq)
    xs, qs = [], []
    for rho in range(nres):
      if d == 1:
        xc = cache_refs[g][0]
      else:
        xc = cache_refs[g][0, pl.ds(rho, N_DIST, stride=d), :, :]
      xc = xc.reshape(N_DIST * 8, 128)
      xs.append(jnp.concatenate([xc, xn, xpad], axis=0).astype(BF16))
      qs.append(qg if d == 1 else jnp.where(qrow_t % d == rho, qg, 0.0))
    xcat = xs[0] if nres == 1 else jnp.concatenate(xs, axis=1)
    qcat = (qs[0] if nres == 1 else jnp.concatenate(qs, axis=1)).astype(BF16)
    s = _dot_nt(xcat, qcat) * scale + bias_ref[g]
    m = jnp.max(s, axis=0, keepdims=True)
    p = jnp.exp(s - m)
    l = jnp.sum(p, axis=0, keepdims=True)
    pn = p / l
    pvt = pltpu.roll(pn, 4, 0).T.astype(BF16)
    full = _dot(pvt, xcat)
    o = full[:, :128]
    for rho in range(1, nres):
      o = jnp.where(qrow_t % d == rho, full[:, rho * 128:(rho + 1) * 128], o)
    o_g.append(o)
    lse_g.append(jnp.broadcast_to(m + jnp.log(l), (128, 128)).T)
  mx = jnp.maximum(jnp.maximum(lse_g[0], lse_g[1]), lse_g[2])
  es = [jnp.exp(v - mx) for v in lse_g]
  den = es[0] + es[1] + es[2]
  o_ref[...] = (es[0] * o_g[0] + es[1] * o_g[1] + es[2] * o_g[2]) / den

  for c in copies:
    c.wait()


def _attn_sample(q, caches, news, slopes, tq):
  batch = q.shape[0]
  in_specs = [pl.BlockSpec(memory_space=pltpu.SMEM),
              pl.BlockSpec((None, 3, 128, 128), lambda b: (b, 0, 0, 0))]
  args = [slopes, q]
  for g, (w, d) in enumerate(A_GROUPS):
    in_specs.append(pl.BlockSpec((1, w, 8, 128), lambda b: (b, 0, 0, 0)))
    args.append(caches[g])
  for g in range(3):
    in_specs.append(pl.BlockSpec((1, tq, 8, 128), lambda b: (b, 0, 0, 0)))
    args.append(news[g])
  out_shape = [jax.ShapeDtypeStruct((batch, 128, 128), F32)]
  out_specs = [pl.BlockSpec((None, 128, 128), lambda b: (b, 0, 0))]
  for g, (w, _) in enumerate(A_GROUPS):
    out_shape.append(jax.ShapeDtypeStruct((batch, w, 8, 128), F32))
    out_specs.append(pl.BlockSpec(memory_space=pl.ANY))
  return pl.pallas_call(
      functools.partial(_attn_sample_kernel, tq=tq),
      grid=(batch,),
      in_specs=in_specs,
      out_specs=out_specs,
      out_shape=out_shape,
      scratch_shapes=[pltpu.SemaphoreType.DMA((2 * len(A_GROUPS),)),
                      pltpu.VMEM((len(A_GROUPS), -(-(N_DIST + tq) * 8 // 128) * 128, 128), F32)],
      compiler_params=_params(("arbitrary",)),
      name="attn_sample",
  )(*args)


def _retention_kernel(x_ref, wqk_ref, wv_ref, cos_ref, sin_ref, dmat_ref, qdec_ref, kdec_ref,
                      cdec_ref, s0_ref, gn_ref, o_ref, sout_ref, state, qk_ref, v_ref, *, chunk):
  j = pl.program_id(1)
  nb, tc, _ = x_ref.shape

  @pl.when(j == 0)
  def _():
    state[...] = s0_ref[...]

  xb = x_ref[...].reshape(nb * tc, D_MODEL).astype(BF16)
  qk_ref[...] = _dot(xb, wqk_ref[...])
  v_ref[...] = _dot(xb, wv_ref[...]).astype(BF16)
  kscale = R_KEY_DIM ** -0.5
  for bi in range(nb):
    for ci in range(tc // chunk):
      rows = pl.ds(bi * tc + ci * chunk, chunk)
      seq_rows = pl.ds(ci * chunk, chunk)
      cosv = cos_ref[seq_rows, :]
      sinv = sin_ref[seq_rows, :]
      for h in range(R_HEADS):
        qh = qk_ref[rows, h * 128:(h + 1) * 128]
        kh = qk_ref[rows, R_QK + h * 128:R_QK + (h + 1) * 128]
        qrot = qh * cosv + pltpu.roll(qh, 64, 1) * sinv
        krot = (kh * cosv + pltpu.roll(kh, 64, 1) * sinv) * kscale
        qb = qrot.astype(BF16)
        vh = v_ref[rows, h * 256:(h + 1) * 256]
        att = _dot_nt(qb, krot.astype(BF16)) * dmat_ref[h]
        sh = state[bi, h]
        o = _dot(att.astype(BF16), vh) + _dot(qb, sh.astype(BF16)) * qdec_ref[h]
        state[bi, h] = cdec_ref[h] * sh + _dot_tn((krot * kdec_ref[h]).astype(BF16), vh)
        mu = jnp.mean(o, axis=-1, keepdims=True)
        var = jnp.mean(jnp.square(o - mu), axis=-1, keepdims=True)
        on = (o - mu) * lax.rsqrt(var + EPS) * gn_ref[:, h * 256:(h + 1) * 256]
        o_ref[bi, seq_rows, h * 256:(h + 1) * 256] = on

  @pl.when(j == pl.num_programs(1) - 1)
  def _():
    sout_ref[...] = state[...]


def _retention(x, w16, cos, sin, tables, s0, gn, rows_per_step, chunk, nb):
  batch, t, _ = x.shape
  dmat, qdec, kdec, cdec = tables
  tc = rows_per_step
  const3 = lambda b, j: (0, 0, 0)
  wslab = lambda off: pl.BlockSpec((D_MODEL, R_V), lambda b, j: (0, off // R_V),
                                   pipeline_mode=pl.Buffered(1))
  state_spec = pl.BlockSpec((nb, R_HEADS, R_KEY_DIM, R_VAL_DIM), lambda b, j: (b, 0, 0, 0))
  return pl.pallas_call(
      functools.partial(_retention_kernel, chunk=chunk),
      grid=(batch // nb, t // tc),
      in_specs=[pl.BlockSpec((nb, tc, D_MODEL), lambda b, j: (b, j, 0)),
                wslab(_OFF[4]), wslab(_OFF[6]),
                pl.BlockSpec((tc, 128), lambda b, j: (j, 0)),
                pl.BlockSpec((tc, 128), lambda b, j: (j, 0)),
                pl.BlockSpec(dmat.shape, const3),
                pl.BlockSpec(qdec.shape, const3),
                pl.BlockSpec(kdec.shape, const3),
                pl.BlockSpec(cdec.shape, const3),
                state_spec,
                pl.BlockSpec((1, R_V), lambda b, j: (0, 0))],
      out_specs=[pl.BlockSpec((nb, tc, R_V), lambda b, j: (b, j, 0)), state_spec],
      out_shape=[jax.ShapeDtypeStruct((batch, t, R_V), F32),
                 jax.ShapeDtypeStruct((batch, R_HEADS, R_KEY_DIM, R_VAL_DIM), F32)],
      scratch_shapes=[pltpu.VMEM((nb, R_HEADS, R_KEY_DIM, R_VAL_DIM), F32),
                      pltpu.VMEM((nb * tc, 2 * R_QK), F32), pltpu.VMEM((nb * tc, R_V), BF16)],
      compiler_params=_params(("parallel", "arbitrary")),
      name="retention",
  )(x, w16, w16, cos, sin, dmat, qdec, kdec, cdec, s0, gn)


def _retention_tables(c_true, c_pad):
  lg = jnp.log1p(-jnp.exp2(-5.0 - jnp.arange(R_HEADS, dtype=F32)))
  i = jnp.arange(c_pad, dtype=F32)
  live = i < c_true
  diff = i[:, None] - i[None, :]
  causal = (diff >= 0) & live[:, None] & live[None, :]
  dmat = jnp.where(causal[None], jnp.exp(jnp.where(causal, diff, 0.0)[None] * lg[:, None, None]), 0.0)
  qdec = jnp.where(live[None, :], jnp.exp((i[None, :] + 1.0) * lg[:, None]), 0.0)
  kdec = jnp.where(live[None, :], jnp.exp((c_true - 1.0 - i)[None, :] * lg[:, None]), 0.0)
  cdec = jnp.exp(c_true * lg)
  qdec = jnp.broadcast_to(qdec[:, :, None], (R_HEADS, c_pad, R_VAL_DIM))
  kdec = jnp.broadcast_to(kdec[:, :, None], (R_HEADS, c_pad, R_KEY_DIM))
  cdec = jnp.broadcast_to(cdec[:, None, None], (R_HEADS, R_KEY_DIM, R_VAL_DIM))
  return dmat, qdec, kdec, cdec


def _rope_tables(pos):
  half = R_KEY_DIM // 2
  inv = 1.0 / (ROPE_BASE ** jnp.linspace(0.0, 1.0, half, dtype=F32))
  ang = pos.astype(F32)[:, None] * inv[None, :]
  cos = jnp.cos(ang)
  sin = jnp.sin(ang)
  return jnp.concatenate([cos, cos], axis=-1), jnp.concatenate([-sin, sin], axis=-1)


def _output_kernel(x_ref, oa_ref, or_ref, wza_ref, wzr_ref, wga_ref, wgb_ref, wpa_ref, wpb_ref,
                   wo_ref, lng_ref, lnb_ref, y_ref):
  tm = x_ref.shape[0]
  nsub = 2 if tm % 32 == 0 else 1
  sub = tm // nsub
  for si in range(nsub):
    rows = pl.ds(si * sub, sub)
    x = x_ref[rows, :]
    xb = x.astype(BF16)
    za = _dot(xb, wza_ref[...])
    ya = _dot((jax.nn.silu(za) * oa_ref[rows, :]).astype(BF16), wpa_ref[...])
    zr = _dot(xb, wzr_ref[...])
    yb = _dot((jax.nn.silu(zr) * or_ref[rows, :]).astype(BF16), wpb_ref[...])
    ga = _dot(xb, wga_ref[...])
    gb = _dot(xb, wgb_ref[...])
    mix = jax.nn.sigmoid(ga) * ya + jax.nn.sigmoid(gb) * yb
    out = _dot(mix.astype(BF16), wo_ref[...])
    zz = DEEPNORM_ALPHA * x + out
    zm = jnp.mean(zz, axis=-1, keepdims=True)
    zv = jnp.mean(jnp.square(zz - zm), axis=-1, keepdims=True)
    y_ref[rows, :] = (zz - zm) * lax.rsqrt(zv + EPS) * lng_ref[...] + lnb_ref[...]


def _output(x, oa, orr, w16, wpa, wpb, wo, lng, lnb, tm):
  m = x.shape[0]
  row = lambda w: pl.BlockSpec((tm, w), lambda i: (i, 0))
  full = lambda a: pl.BlockSpec(a.shape, lambda i: (0, 0), pipeline_mode=pl.Buffered(1))
  wcol = lambda off, width: pl.BlockSpec((D_MODEL, width), lambda i: (0, off // width),
                                         pipeline_mode=pl.Buffered(1))
  in_specs = [row(D_MODEL), row(A_OUT), row(R_V),
              wcol(_OFF[3], A_OUT), wcol(_OFF[7], R_V), wcol(_OFF[8], D_MODEL),
              wcol(_OFF[9], D_MODEL),
              full(wpa), full(wpb), full(wo), full(lng), full(lnb)]
  args = [x, oa, orr, w16, w16, w16, w16, wpa, wpb, wo, lng, lnb]
  return pl.pallas_call(
      _output_kernel,
      grid=(m // tm,),
      in_specs=in_specs,
      out_specs=row(D_MODEL),
      out_shape=jax.ShapeDtypeStruct((m, D_MODEL), F32),
      compiler_params=_params(("parallel",)),
      name="gates_output",
  )(*args)


def _alibi_slopes():
  n = len(A_GROUPS) * A_HEADS
  return jnp.exp2(-8.0 * jnp.arange(1, n + 1, dtype=F32) / n).reshape(len(A_GROUPS), A_HEADS)


def kernel(x_prompt, x_sample, cache_kv_w128, cache_kv_w512, cache_kv_w2048, state_ret,
           w_in, w_pa, w_pb, w_o, gn_g, ln_g, ln_b):
  bp, tp, _ = x_prompt.shape
  bs, ts, _ = x_sample.shape
  slopes = _alibi_slopes()
  w16 = w_in.astype(BF16)
  wpa = w_pa.astype(BF16)
  wpb = w_pb.astype(BF16)
  wo = w_o.astype(BF16)
  gn = gn_g.reshape(1, R_V)
  lng = ln_g.reshape(1, D_MODEL)
  lnb = ln_b.reshape(1, D_MODEL)

  xp2 = x_prompt.reshape(bp * tp, D_MODEL)
  qkvs, kv_p = [], []
  for g, (w, _) in enumerate(A_GROUPS):
    qkv, tail = _proj_group(xp2, w16, g, bp, tp)
    qkvs.append(qkv)
    kv_p.append(tail.reshape(bp, min(w, tp), 2, A_HEADS, A_HEAD_DIM))
  oa_p = _attn_prompt(qkvs, slopes, bp, tp).reshape(bp * tp, A_OUT)

  cos_p, sin_p = _rope_tables(jnp.arange(tp, dtype=jnp.int32))
  or_p, s_p = _retention(
      x_prompt, w16, cos_p, sin_p, _retention_tables(R_CHUNK, R_CHUNK),
      jnp.zeros((bp, R_HEADS, R_KEY_DIM, R_VAL_DIM), F32), gn, 1024, R_CHUNK, 1)
  y_p = _output(xp2, oa_p, or_p.reshape(bp * tp, R_V), w16, wpa, wpb, wo, lng, lnb, 512)
  xs2 = x_sample.reshape(bs * ts, D_MODEL)
  hs = _matmul(xs2, w16, 0, _OFF[3], F32, bs * ts, A_GW)
  caches = (cache_kv_w128, cache_kv_w512, cache_kv_w2048)
  q_s, news = [], []
  for g in range(3):
    col = lambda part: hs[:, _OFF[part] + g * A_GW:_OFF[part] + (g + 1) * A_GW]
    qg = col(0).reshape(bs, ts * A_HEADS, A_HEAD_DIM)
    q_s.append(jnp.pad(qg, ((0, 0), (0, 128 - ts * A_HEADS), (0, 0))))
    news.append(jnp.concatenate([col(1), col(2)], axis=1).reshape(bs, ts, 8, 128))
  q_s = jnp.stack(q_s, axis=1)
  caches8 = [c.reshape(bs, c.shape[1], 8, 128) for c in caches]
  cpad = 16
  xs_pad = jnp.pad(x_sample, ((0, 0), (0, cpad - ts), (0, 0)))
  cos_s, sin_s = _rope_tables(PAST_LEN + jnp.arange(ts, dtype=jnp.int32))
  padtab = lambda a: jnp.pad(a, ((0, cpad - ts), (0, 0)))
  oa_s, u0, u1, u2 = _attn_sample(q_s, caches8, news, slopes, ts)
  oa_s = oa_s[:, :ts * A_HEADS].reshape(bs * ts, A_OUT)
  kv_s = [u.reshape(bs, u.shape[1], 2, A_HEADS, A_HEAD_DIM) for u in (u0, u1, u2)]
  or_s, s_s = _retention(
      xs_pad, w16, padtab(cos_s), padtab(sin_s),
      _retention_tables(ts, cpad), state_ret.astype(F32), gn, cpad, cpad, 8)
  or_s = or_s[:, :ts].reshape(bs * ts, R_V)
  y_s = _output(xs2, oa_s, or_s, w16, wpa, wpb, wo, lng, lnb, 256)

  return (y_p.reshape(bp, tp, D_MODEL), y_s.reshape(bs, ts, D_MODEL),
          kv_p[0], kv_p[1], kv_p[2], s_p,
          kv_s[0], kv_s[1], kv_s[2], s_s)
```

```python
import functools

import numpy as np
import jax
import jax.numpy as jnp
from jax import lax
from jax.experimental import pallas as pl
from jax.experimental.pallas import tpu as pltpu
from jax.experimental.pallas import tpu_sc as plsc

D_MODEL = 1024
PAST_LEN = 16384
A_GROUPS = ((128, 1), (512, 4), (2048, 16))
A_HEADS = 4
A_HEAD_DIM = 128
A_GW = A_HEADS * A_HEAD_DIM
A_QKV = len(A_GROUPS) * A_GW
A_OUT = A_GW
N_DIST = 128
R_HEADS = 4
R_KEY_DIM = 128
R_VAL_DIM = 256
R_QK = R_HEADS * R_KEY_DIM
R_V = R_HEADS * R_VAL_DIM
R_CHUNK = 256
ROPE_BASE = 10000.0
EPS = 1e-5
NEG_INF = -1e30
DEEPNORM_ALPHA = 2.0 ** 0.25
_SPLIT = (A_QKV, A_QKV, A_QKV, A_OUT, R_QK, R_QK, R_V, R_V, D_MODEL, D_MODEL)
_OFF = tuple(int(v) for v in np.concatenate([[0], np.cumsum(_SPLIT)]))

BF16 = jnp.bfloat16
F32 = jnp.float32
VMEM_LIMIT = 56 * 1024 * 1024

_NT = (((1,), (1,)), ((), ()))
_TN = (((0,), (0,)), ((), ()))


def _dot(a, b):
  return jnp.dot(a, b, preferred_element_type=F32)


def _dot_nt(a, b):
  return lax.dot_general(a, b, _NT, preferred_element_type=F32)


def _dot_tn(a, b):
  return lax.dot_general(a, b, _TN, preferred_element_type=F32)


def _params(sem):
  return pltpu.CompilerParams(dimension_semantics=sem, vmem_limit_bytes=VMEM_LIMIT)


def _mm_kernel(x_ref, w_ref, o_ref):
  o_ref[...] = _dot(x_ref[...].astype(BF16), w_ref[...]).astype(o_ref.dtype)


def _matmul(x, w, col0, ncols, out_dtype, tm, tn):
  m, k = x.shape
  c0 = col0 // tn
  return pl.pallas_call(
      _mm_kernel,
      grid=(m // tm, ncols // tn),
      in_specs=[pl.BlockSpec((tm, k), lambda i, j: (i, 0)),
                pl.BlockSpec((k, tn), lambda i, j: (0, c0 + j))],
      out_specs=pl.BlockSpec((tm, tn), lambda i, j: (i, j)),
      out_shape=jax.ShapeDtypeStruct((m, ncols), out_dtype),
      compiler_params=_params(("parallel", "arbitrary")),
      name="proj_matmul",
  )(x, w)


PROJ_TM = 1024


def _proj_group_kernel(x_ref, wq_ref, wk_ref, wv_ref, o_ref, tail_ref, *scratch,
                       d, first_tail, tail_rows):
  i = pl.program_id(1)
  tm = x_ref.shape[0]
  n = tm // d
  x = x_ref[...]
  xb = x.astype(BF16)
  nat = []
  if d > 1:
    xs_ref, xp_ref = scratch
    for c in range(D_MODEL // 128):
      xs_ref[c] = x[:, c * 128:(c + 1) * 128]
  nsplit = 2 if d > 1 else 1
  rper = d // nsplit
  w_refs = (wq_ref, wk_ref, wv_ref)

  def deinterleave(sb):
    for r in range(sb * rper, (sb + 1) * rper):
      for c in range(D_MODEL // 128):
        xp_ref[r * n:(r + 1) * n, c * 128:(c + 1) * 128] = (
            xs_ref[c, pl.ds(r, n, stride=d), :].astype(BF16))

  def project(sb, part):
    perm = xb if d == 1 else xp_ref[sb * rper * n:(sb + 1) * rper * n, :]
    res = _dot(perm, w_refs[part][...])
    nat.append(res)
    for r in range(sb * rper, (sb + 1) * rper):
      lo = (r - sb * rper) * n
      for h in range(A_HEADS):
        o_ref[r, part * A_HEADS + h] = res[lo:lo + n, h * 128:(h + 1) * 128].astype(BF16)

  if d == 1:
    for part in range(3):
      project(0, part)
  else:
    deinterleave(0)
    project(0, 0)
    deinterleave(1)
    project(0, 1)
    project(0, 2)
    for part in range(3):
      project(1, part)

  @pl.when(i >= first_tail)
  def _():
    if d == 1:
      kv = nat[1:]
    else:
      kv = [_dot(xb, wk_ref[...]), _dot(xb, wv_ref[...])]
    for c in range(2 * A_HEADS):
      src = kv[c // A_HEADS][tm - tail_rows:, (c % A_HEADS) * 128:(c % A_HEADS + 1) * 128]
      tail_ref[pl.ds(c, tail_rows, stride=2 * A_HEADS), :] = src


def _proj_group(x2, w16, g, batch, seq):
  w, d = A_GROUPS[g]
  w = min(w, seq)
  tm = PROJ_TM
  nblk = seq // tm
  tail_rows = min(tm, w)
  first_tail = nblk - max(w // tm, 1)
  ls = seq // d
  wspec = lambda part: pl.BlockSpec((D_MODEL, A_GW), lambda b, i: (0, 3 * part + g))
  scratch = []
  if d > 1:
    scratch = [pltpu.VMEM((D_MODEL // 128, tm, 128), F32), pltpu.VMEM((tm, D_MODEL), BF16)]
  return pl.pallas_call(
      functools.partial(_proj_group_kernel, d=d, first_tail=first_tail, tail_rows=tail_rows),
      grid=(batch, nblk),
      in_specs=[pl.BlockSpec((tm, D_MODEL), lambda b, i: (b * nblk + i, 0)),
                wspec(0), wspec(1), wspec(2)],
      out_specs=[pl.BlockSpec((None, d, 3 * A_HEADS, tm // d, 128), lambda b, i: (b, 0, 0, i, 0)),
                 pl.BlockSpec((None, tail_rows * 8, 128),
                              lambda b, i: (b, jnp.maximum(i - first_tail, 0), 0))],
      out_shape=[jax.ShapeDtypeStruct((batch, d, 3 * A_HEADS, ls, 128), BF16),
                 jax.ShapeDtypeStruct((batch, w * 8, 128), F32)],
      scratch_shapes=scratch,
      compiler_params=_params(("parallel", "arbitrary")),
      name="proj_group",
  )(x2, w16, w16, w16)


ATT_BQ = 128
ATT_SET = 4
LOG2E = 1.4426950408889634


def _attn_prompt_kernel(slopes_ref, *refs, seq):
  qkv = refs[:9]
  o_ref = refs[9]
  oacc = refs[10:13]
  lacc = refs[13:16]
  bias_ref, s_buf, p_buf, l_buf = refs[16:20]
  h = pl.program_id(0)
  scale = A_HEAD_DIM ** -0.5

  @pl.when(pl.program_id(1) == 0)
  def _():
    for g, (_, d) in enumerate(A_GROUPS):
      ls = seq // d
      bq = min(ATT_BQ, ls)
      klen = min(bq + N_DIST, ls)
      slope = slopes_ref[g, h]
      qi = lax.broadcasted_iota(jnp.int32, (bq, klen), 0)
      kj = lax.broadcasted_iota(jnp.int32, (bq, klen), 1)
      for which, off in enumerate((0, N_DIST)):
        delta = qi - kj + off
        valid = (delta >= 0) & (delta <= N_DIST)
        dist = (delta * d).astype(F32)
        bias_ref[2 * g + which, :bq, :klen] = jnp.where(valid, -(slope * LOG2E) * dist, NEG_INF)

  for g, (_, d) in enumerate(A_GROUPS):
    q_ref, k_ref, v_ref = qkv[3 * g:3 * g + 3]
    ls = seq // d
    bq = min(ATT_BQ, ls)
    nqb = ls // bq
    klen = min(bq + N_DIST, ls)

    def coords(idx, nqb=nqb, bq=bq):
      r = idx // nqb
      qb = idx % nqb
      q0 = pl.multiple_of(qb * bq, bq)
      k0 = pl.multiple_of(jnp.maximum(qb * bq - N_DIST, 0), N_DIST)
      return r, qb, q0, k0

    def put_rows(ref, idx, val, d=d, bq=bq):
      r, _, q0, _ = coords(idx)
      if d == 1:
        ref[pl.ds(q0, bq), :] = val
      else:
        ref[pl.ds(r + q0 * d, bq, stride=d), :] = val

    def scores(idx, slot, bq=bq, klen=klen, q_ref=q_ref, k_ref=k_ref):
      r, _, q0, k0 = coords(idx)
      s_buf[slot, :bq, :klen] = _dot_nt(q_ref[r, pl.ds(q0, bq), :], k_ref[r, pl.ds(k0, klen), :])

    def softmax(idx, slot, g=g, bq=bq, klen=klen):
      _, qb, _, _ = coords(idx)
      s = (s_buf[slot, :bq, :klen] * (scale * LOG2E)
           + bias_ref[2 * g + jnp.minimum(qb, 1), :bq, :klen])
      m = jnp.max(s, axis=-1, keepdims=True)
      p = jnp.exp2(s - m)
      l = jnp.sum(p, axis=-1, keepdims=True)
      p_buf[slot, :bq, :klen] = p.astype(BF16)
      l_buf[slot, :bq, :] = jnp.broadcast_to(l, (bq, 128))
      put_rows(lacc[g], idx, jnp.broadcast_to(m + jnp.log2(l), (bq, 128)))

    def values(idx, slot, g=g, bq=bq, klen=klen, v_ref=v_ref):
      r, _, _, k0 = coords(idx)
      o = _dot(p_buf[slot, :bq, :klen], v_ref[r, pl.ds(k0, klen), :]) / l_buf[slot, :bq, :]
      put_rows(oacc[g], idx, o)

    nblk = d * nqb
    ns = ATT_SET

    def run(stage, first, half):
      for t in range(ns):
        stage(first + t, half * ns + t)

    run(scores, 0, 0)
    run(softmax, 0, 0)
    run(scores, ns, 1)

    def steady(j, carry):
      i = 2 * ns * (j + 1)
      run(values, i - 2 * ns, 0)
      run(softmax, i - ns, 1)
      run(scores, i, 0)
      run(values, i - ns, 1)
      run(softmax, i, 0)
      run(scores, i + ns, 1)
      return carry

    lax.fori_loop(0, (nblk - 2 * ns) // (2 * ns), steady, 0)
    run(values, nblk - 2 * ns, 0)
    run(softmax, nblk - ns, 1)
    run(values, nblk - ns, 1)

  mrows = 512

  def merge(i, carry):
    r0 = pl.multiple_of(i * mrows, mrows)
    l0 = lacc[0][pl.ds(r0, mrows), :]
    l1 = lacc[1][pl.ds(r0, mrows), :]
    l2 = lacc[2][pl.ds(r0, mrows), :]
    mx = jnp.maximum(jnp.maximum(l0, l1), l2)
    e0 = jnp.exp2(l0 - mx)
    e1 = jnp.exp2(l1 - mx)
    e2 = jnp.exp2(l2 - mx)
    den = e0 + e1 + e2
    acc = (e0 * oacc[0][pl.ds(r0, mrows), :] + e1 * oacc[1][pl.ds(r0, mrows), :]
           + e2 * oacc[2][pl.ds(r0, mrows), :])
    o_ref[pl.ds(r0, mrows), :] = acc / den
    return carry

  lax.fori_loop(0, seq // mrows, merge, 0)


def _attn_prompt(qkvs, slopes, batch, seq):
  in_specs = [pl.BlockSpec(memory_space=pltpu.SMEM)]
  args = [slopes]
  for g, (_, d) in enumerate(A_GROUPS):
    ls = seq // d
    for part in range(3):
      in_specs.append(pl.BlockSpec(
          (None, d, None, ls, 128),
          lambda h, b, part=part: (b, 0, part * A_HEADS + h, 0, 0)))
      args.append(qkvs[g])
  scratch = ([pltpu.VMEM((seq, 128), F32) for _ in range(6)]
             + [pltpu.VMEM((6, ATT_BQ, ATT_BQ + N_DIST), F32),
                pltpu.VMEM((2 * ATT_SET, ATT_BQ, ATT_BQ + N_DIST), F32),
                pltpu.VMEM((2 * ATT_SET, ATT_BQ, ATT_BQ + N_DIST), BF16),
                pltpu.VMEM((2 * ATT_SET, ATT_BQ, 128), F32)])
  return pl.pallas_call(
      functools.partial(_attn_prompt_kernel, seq=seq),
      grid=(A_HEADS, batch),
      in_specs=in_specs,
      out_specs=pl.BlockSpec((None, seq, 128), lambda h, b: (b, 0, h)),
      out_shape=jax.ShapeDtypeStruct((batch, seq, A_OUT), F32),
      scratch_shapes=scratch,
      compiler_params=_params(("parallel", "arbitrary")),
      name="attn_prompt",
  )(*args)


SC_ROLLED = 2
SC_CHUNK_ROWS = 24


def _sc_roll(cache, shift):
  batch, w, slots, lanes = cache.shape
  rows = w * slots
  chunk = SC_CHUNK_ROWS * slots
  nchunk = (w - shift) // SC_CHUNK_ROWS
  assert nchunk * SC_CHUNK_ROWS == w - shift
  mesh = plsc.VectorSubcoreMesh(core_axis_name="c", subcore_axis_name="s")
  nsub = mesh.num_subcores
  assert mesh.num_cores * nsub == batch, "one vector subcore per batch entry"

  def body(x_hbm, o_hbm, buf):
    base = (lax.axis_index("c") * nsub + lax.axis_index("s")) * rows

    @pl.loop(0, nchunk)
    def _(i):
      pltpu.sync_copy(x_hbm.at[pl.ds(base + shift * slots + i * chunk, chunk)], buf)
      pltpu.sync_copy(buf, o_hbm.at[pl.ds(base + i * chunk, chunk)])

  flat = cache.reshape(batch * rows, lanes)
  out = pl.kernel(body, out_type=jax.ShapeDtypeStruct(flat.shape, flat.dtype), mesh=mesh,
                  scratch_types=[pltpu.VMEM((chunk, lanes), cache.dtype)])(flat)
  return out.reshape(cache.shape)


def _attn_sample_kernel(slopes_ref, q_ref, c0_ref, c1_ref, c2_ref, n0_ref, n1_ref, n2_ref,
                        rolled_ref, o_ref, u0_ref, u1_ref, u2_ref, sems, bias_ref, *, tq):
  b = pl.program_id(0)
  cache_refs = (c0_ref, c1_ref, c2_ref)
  new_refs = (n0_ref, n1_ref, n2_ref)
  out_refs = (u0_ref, u1_ref, u2_ref)
  del rolled_ref

  copies = []
  for g, (w, _) in enumerate(A_GROUPS):
    tail = pltpu.make_async_copy(new_refs[g].at[0], out_refs[g].at[b, pl.ds(w - tq, tq)],
                                 sems.at[2 * g + 1])
    tail.start()
    copies.append(tail)
    if g != SC_ROLLED:
      body = pltpu.make_async_copy(cache_refs[g].at[0, pl.ds(tq, w - tq)],
                                   out_refs[g].at[b, pl.ds(0, w - tq)], sems.at[2 * g])
      body.start()
      copies.append(body)

  scale = A_HEAD_DIM ** -0.5
  nreal = N_DIST * 8 + tq * 8
  nrow = -(-nreal // 128) * 128

  @pl.when(b == 0)
  def _():
    ri = lax.broadcasted_iota(jnp.int32, (nrow, 128), 0)
    cj = lax.broadcasted_iota(jnp.int32, (nrow, 128), 1)
    slot = ri % 8
    key = ri // 8
    tcol = cj // A_HEADS
    hcol = cj % A_HEADS
    col_ok = cj < tq * A_HEADS
    for g, (_, d) in enumerate(A_GROUPS):
      qt = tcol // d
      nidx = key - N_DIST
      in_cache = key < N_DIST
      valid_c = in_cache & (key >= qt)
      dist_c = d * (N_DIST + qt - key)
      valid_n = (~in_cache) & (nidx <= tcol) & ((tcol - nidx) % d == 0)
      dist_n = tcol - nidx
      valid = (valid_c | valid_n) & (slot == hcol) & col_ok & (ri < nreal)
      dist = jnp.where(in_cache, dist_c, dist_n).astype(F32)
      slope = jnp.zeros((nrow, 128), F32)
      for hh in range(A_HEADS):
        slope = jnp.where(hcol == hh, slopes_ref[g, hh], slope)
      bias_ref[g] = jnp.where(valid, -slope * dist, NEG_INF)

  qrow_t = lax.broadcasted_iota(jnp.int32, (128, 1), 0) // A_HEADS
  xpad = jnp.zeros((nrow - nreal, 128), F32)
  lse_g = []
  o_g = []
  for g, (w, d) in enumerate(A_GROUPS):
    qg = q_ref[g]
    xn = new_refs[g][0].reshape(tq * 8, 128)
    nres = min(d, tq)
    xs, qs = [], []
    for rho in range(nres):
      if g == SC_ROLLED:
        xc = cache_refs[g][0, :, rho, :, :]
      elif d == 1:
        xc = cache_refs[g][0]
      else:
        xc = cache_refs[g][0, pl.ds(rho, N_DIST, stride=d), :, :]
      xc = xc.reshape(N_DIST * 8, 128)
      xs.append(jnp.concatenate([xc, xn, xpad], axis=0).astype(BF16))
      qs.append(qg if d == 1 else jnp.where(qrow_t % d == rho, qg, 0.0))
    xcat = xs[0] if nres == 1 else jnp.concatenate(xs, axis=1)
    qcat = (qs[0] if nres == 1 else jnp.concatenate(qs, axis=1)).astype(BF16)
    s = _dot_nt(xcat, qcat) * scale + bias_ref[g]
    m = jnp.max(s, axis=0, keepdims=True)
    p = jnp.exp(s - m)
    l = jnp.sum(p, axis=0, keepdims=True)
    pn = p / l
    pvt = pltpu.roll(pn, 4, 0).T.astype(BF16)
    full = _dot(pvt, xcat)
    o = full[:, :128]
    for rho in range(1, nres):
      o = jnp.where(qrow_t % d == rho, full[:, rho * 128:(rho + 1) * 128], o)
    o_g.append(o)
    lse_g.append(jnp.broadcast_to(m + jnp.log(l), (128, 128)).T)
  mx = jnp.maximum(jnp.maximum(lse_g[0], lse_g[1]), lse_g[2])
  es = [jnp.exp(v - mx) for v in lse_g]
  den = es[0] + es[1] + es[2]
  o_ref[...] = (es[0] * o_g[0] + es[1] * o_g[1] + es[2] * o_g[2]) / den

  for c in copies:
    c.wait()


def _attn_sample(q, caches, news, slopes, tq):
  batch = q.shape[0]
  in_specs = [pl.BlockSpec(memory_space=pltpu.SMEM),
              pl.BlockSpec((None, 3, 128, 128), lambda b: (b, 0, 0, 0))]
  args = [slopes, q]
  for g, (w, d) in enumerate(A_GROUPS):
    if g == SC_ROLLED:
      in_specs.append(pl.BlockSpec((1, N_DIST, min(d, tq), 8, 128), lambda b: (b, 0, 0, 0, 0)))
      args.append(caches[g].reshape(batch, N_DIST, d, 8, 128))
    else:
      in_specs.append(pl.BlockSpec((1, w, 8, 128), lambda b: (b, 0, 0, 0)))
      args.append(caches[g])
  for g in range(3):
    in_specs.append(pl.BlockSpec((1, tq, 8, 128), lambda b: (b, 0, 0, 0)))
    args.append(news[g])
  in_specs.append(pl.BlockSpec(memory_space=pl.ANY))
  args.append(_sc_roll(caches[SC_ROLLED], tq))
  rolled_in = len(args) - 1
  out_shape = [jax.ShapeDtypeStruct((batch, 128, 128), F32)]
  out_specs = [pl.BlockSpec((None, 128, 128), lambda b: (b, 0, 0))]
  for g, (w, _) in enumerate(A_GROUPS):
    out_shape.append(jax.ShapeDtypeStruct((batch, w, 8, 128), F32))
    out_specs.append(pl.BlockSpec(memory_space=pl.ANY))
  return pl.pallas_call(
      functools.partial(_attn_sample_kernel, tq=tq),
      grid=(batch,),
      in_specs=in_specs,
      out_specs=out_specs,
      out_shape=out_shape,
      input_output_aliases={rolled_in: 1 + SC_ROLLED},
      scratch_shapes=[pltpu.SemaphoreType.DMA((2 * len(A_GROUPS),)),
                      pltpu.VMEM((len(A_GROUPS), -(-(N_DIST + tq) * 8 // 128) * 128, 128), F32)],
      compiler_params=_params(("arbitrary",)),
      name="attn_sample",
  )(*args)


def _retention_kernel(x_ref, wqk_ref, wv_ref, cos_ref, sin_ref, dmat_ref, qdec_ref, kdec_ref,
                      cdec_ref, s0_ref, gn_ref, o_ref, sout_ref, state, qk_ref, v_ref, *, chunk):
  j = pl.program_id(1)
  nb, tc, _ = x_ref.shape

  @pl.when(j == 0)
  def _():
    state[...] = s0_ref[...]

  xb = x_ref[...].reshape(nb * tc, D_MODEL).astype(BF16)
  qk_ref[...] = _dot(xb, wqk_ref[...])
  v_ref[...] = _dot(xb, wv_ref[...]).astype(BF16)
  kscale = R_KEY_DIM ** -0.5
  for bi in range(nb):
    for ci in range(tc // chunk):
      rows = pl.ds(bi * tc + ci * chunk, chunk)
      seq_rows = pl.ds(ci * chunk, chunk)
      cosv = cos_ref[seq_rows, :]
      sinv = sin_ref[seq_rows, :]
      for h in range(R_HEADS):
        qh = qk_ref[rows, h * 128:(h + 1) * 128]
        kh = qk_ref[rows, R_QK + h * 128:R_QK + (h + 1) * 128]
        qrot = qh * cosv + pltpu.roll(qh, 64, 1) * sinv
        krot = (kh * cosv + pltpu.roll(kh, 64, 1) * sinv) * kscale
        qb = qrot.astype(BF16)
        vh = v_ref[rows, h * 256:(h + 1) * 256]
        att = _dot_nt(qb, krot.astype(BF16)) * dmat_ref[h]
        sh = state[bi, h]
        o = _dot(att.astype(BF16), vh) + _dot(qb, sh.astype(BF16)) * qdec_ref[h]
        state[bi, h] = cdec_ref[h] * sh + _dot_tn((krot * kdec_ref[h]).astype(BF16), vh)
        mu = jnp.mean(o, axis=-1, keepdims=True)
        var = jnp.mean(jnp.square(o - mu), axis=-1, keepdims=True)
        on = (o - mu) * lax.rsqrt(var + EPS) * gn_ref[:, h * 256:(h + 1) * 256]
        o_ref[bi, seq_rows, h * 256:(h + 1) * 256] = on

  @pl.when(j == pl.num_programs(1) - 1)
  def _():
    sout_ref[...] = state[...]


def _retention(x, w16, cos, sin, tables, s0, gn, rows_per_step, chunk, nb):
  batch, t, _ = x.shape
  dmat, qdec, kdec, cdec = tables
  tc = rows_per_step
  const3 = lambda b, j: (0, 0, 0)
  wslab = lambda off: pl.BlockSpec((D_MODEL, R_V), lambda b, j: (0, off // R_V),
                                   pipeline_mode=pl.Buffered(1))
  state_spec = pl.BlockSpec((nb, R_HEADS, R_KEY_DIM, R_VAL_DIM), lambda b, j: (b, 0, 0, 0))
  return pl.pallas_call(
      functools.partial(_retention_kernel, chunk=chunk),
      grid=(batch // nb, t // tc),
      in_specs=[pl.BlockSpec((nb, tc, D_MODEL), lambda b, j: (b, j, 0)),
                wslab(_OFF[4]), wslab(_OFF[6]),
                pl.BlockSpec((tc, 128), lambda b, j: (j, 0)),
                pl.BlockSpec((tc, 128), lambda b, j: (j, 0)),
                pl.BlockSpec(dmat.shape, const3),
                pl.BlockSpec(qdec.shape, const3),
                pl.BlockSpec(kdec.shape, const3),
                pl.BlockSpec(cdec.shape, const3),
                state_spec,
                pl.BlockSpec((1, R_V), lambda b, j: (0, 0))],
      out_specs=[pl.BlockSpec((nb, tc, R_V), lambda b, j: (b, j, 0)), state_spec],
      out_shape=[jax.ShapeDtypeStruct((batch, t, R_V), F32),
                 jax.ShapeDtypeStruct((batch, R_HEADS, R_KEY_DIM, R_VAL_DIM), F32)],
      scratch_shapes=[pltpu.VMEM((nb, R_HEADS, R_KEY_DIM, R_VAL_DIM), F32),
                      pltpu.VMEM((nb * tc, 2 * R_QK), F32), pltpu.VMEM((nb * tc, R_V), BF16)],
      compiler_params=_params(("parallel", "arbitrary")),
      name="retention",
  )(x, w16, w16, cos, sin, dmat, qdec, kdec, cdec, s0, gn)


def _retention_tables(c_true, c_pad):
  lg = jnp.log1p(-jnp.exp2(-5.0 - jnp.arange(R_HEADS, dtype=F32)))
  i = jnp.arange(c_pad, dtype=F32)
  live = i < c_true
  diff = i[:, None] - i[None, :]
  causal = (diff >= 0) & live[:, None] & live[None, :]
  dmat = jnp.where(causal[None], jnp.exp(jnp.where(causal, diff, 0.0)[None] * lg[:, None, None]), 0.0)
  qdec = jnp.where(live[None, :], jnp.exp((i[None, :] + 1.0) * lg[:, None]), 0.0)
  kdec = jnp.where(live[None, :], jnp.exp((c_true - 1.0 - i)[None, :] * lg[:, None]), 0.0)
  cdec = jnp.exp(c_true * lg)
  qdec = jnp.broadcast_to(qdec[:, :, None], (R_HEADS, c_pad, R_VAL_DIM))
  kdec = jnp.broadcast_to(kdec[:, :, None], (R_HEADS, c_pad, R_KEY_DIM))
  cdec = jnp.broadcast_to(cdec[:, None, None], (R_HEADS, R_KEY_DIM, R_VAL_DIM))
  return dmat, qdec, kdec, cdec


def _rope_tables(pos):
  half = R_KEY_DIM // 2
  inv = 1.0 / (ROPE_BASE ** jnp.linspace(0.0, 1.0, half, dtype=F32))
  ang = pos.astype(F32)[:, None] * inv[None, :]
  cos = jnp.cos(ang)
  sin = jnp.sin(ang)
  return jnp.concatenate([cos, cos], axis=-1), jnp.concatenate([-sin, sin], axis=-1)


def _output_kernel(x_ref, oa_ref, or_ref, wza_ref, wzr_ref, wga_ref, wgb_ref, wpa_ref, wpb_ref,
                   wo_ref, lng_ref, lnb_ref, y_ref):
  tm = x_ref.shape[0]
  nsub = 2 if tm % 32 == 0 else 1
  sub = tm // nsub
  for si in range(nsub):
    rows = pl.ds(si * sub, sub)
    x = x_ref[rows, :]
    xb = x.astype(BF16)
    za = _dot(xb, wza_ref[...])
    ya = _dot((jax.nn.silu(za) * oa_ref[rows, :]).astype(BF16), wpa_ref[...])
    zr = _dot(xb, wzr_ref[...])
    yb = _dot((jax.nn.silu(zr) * or_ref[rows, :]).astype(BF16), wpb_ref[...])
    ga = _dot(xb, wga_ref[...])
    gb = _dot(xb, wgb_ref[...])
    mix = jax.nn.sigmoid(ga) * ya + jax.nn.sigmoid(gb) * yb
    out = _dot(mix.astype(BF16), wo_ref[...])
    zz = DEEPNORM_ALPHA * x + out
    zm = jnp.mean(zz, axis=-1, keepdims=True)
    zv = jnp.mean(jnp.square(zz - zm), axis=-1, keepdims=True)
    y_ref[rows, :] = (zz - zm) * lax.rsqrt(zv + EPS) * lng_ref[...] + lnb_ref[...]


def _output(x, oa, orr, w16, wpa, wpb, wo, lng, lnb, tm):
  m = x.shape[0]
  row = lambda w: pl.BlockSpec((tm, w), lambda i: (i, 0))
  full = lambda a: pl.BlockSpec(a.shape, lambda i: (0, 0), pipeline_mode=pl.Buffered(1))
  wcol = lambda off, width: pl.BlockSpec((D_MODEL, width), lambda i: (0, off // width),
                                         pipeline_mode=pl.Buffered(1))
  in_specs = [row(D_MODEL), row(A_OUT), row(R_V),
              wcol(_OFF[3], A_OUT), wcol(_OFF[7], R_V), wcol(_OFF[8], D_MODEL),
              wcol(_OFF[9], D_MODEL),
              full(wpa), full(wpb), full(wo), full(lng), full(lnb)]
  args = [x, oa, orr, w16, w16, w16, w16, wpa, wpb, wo, lng, lnb]
  return pl.pallas_call(
      _output_kernel,
      grid=(m // tm,),
      in_specs=in_specs,
      out_specs=row(D_MODEL),
      out_shape=jax.ShapeDtypeStruct((m, D_MODEL), F32),
      compiler_params=_params(("parallel",)),
      name="gates_output",
  )(*args)


def _alibi_slopes():
  n = len(A_GROUPS) * A_HEADS
  return jnp.exp2(-8.0 * jnp.arange(1, n + 1, dtype=F32) / n).reshape(len(A_GROUPS), A_HEADS)


def kernel(x_prompt, x_sample, cache_kv_w128, cache_kv_w512, cache_kv_w2048, state_ret,
           w_in, w_pa, w_pb, w_o, gn_g, ln_g, ln_b):
  bp, tp, _ = x_prompt.shape
  bs, ts, _ = x_sample.shape
  slopes = _alibi_slopes()
  w16 = w_in.astype(BF16)
  wpa = w_pa.astype(BF16)
  wpb = w_pb.astype(BF16)
  wo = w_o.astype(BF16)
  gn = gn_g.reshape(1, R_V)
  lng = ln_g.reshape(1, D_MODEL)
  lnb = ln_b.reshape(1, D_MODEL)

  xp2 = x_prompt.reshape(bp * tp, D_MODEL)
  qkvs, kv_p = [], []
  for g, (w, _) in enumerate(A_GROUPS):
    qkv, tail = _proj_group(xp2, w16, g, bp, tp)
    qkvs.append(qkv)
    kv_p.append(tail.reshape(bp, min(w, tp), 2, A_HEADS, A_HEAD_DIM))
  oa_p = _attn_prompt(qkvs, slopes, bp, tp).reshape(bp * tp, A_OUT)

  cos_p, sin_p = _rope_tables(jnp.arange(tp, dtype=jnp.int32))
  or_p, s_p = _retention(
      x_prompt, w16, cos_p, sin_p, _retention_tables(R_CHUNK, R_CHUNK),
      jnp.zeros((bp, R_HEADS, R_KEY_DIM, R_VAL_DIM), F32), gn, 1024, R_CHUNK, 1)
  y_p = _output(xp2, oa_p, or_p.reshape(bp * tp, R_V), w16, wpa, wpb, wo, lng, lnb, 512)
  xs2 = x_sample.reshape(bs * ts, D_MODEL)
  hs = _matmul(xs2, w16, 0, _OFF[3], F32, bs * ts, A_GW)
  caches = (cache_kv_w128, cache_kv_w512, cache_kv_w2048)
  q_s, news = [], []
  for g in range(3):
    col = lambda part: hs[:, _OFF[part] + g * A_GW:_OFF[part] + (g + 1) * A_GW]
    qg = col(0).reshape(bs, ts * A_HEADS, A_HEAD_DIM)
    q_s.append(jnp.pad(qg, ((0, 0), (0, 128 - ts * A_HEADS), (0, 0))))
    news.append(jnp.concatenate([col(1), col(2)], axis=1).reshape(bs, ts, 8, 128))
  q_s = jnp.stack(q_s, axis=1)
  caches8 = [c.reshape(bs, c.shape[1], 8, 128) for c in caches]
  cpad = 16
  xs_pad = jnp.pad(x_sample, ((0, 0), (0, cpad - ts), (0, 0)))
  cos_s, sin_s = _rope_tables(PAST_LEN + jnp.arange(ts, dtype=jnp.int32))
  padtab = lambda a: jnp.pad(a, ((0, cpad - ts), (0, 0)))
  oa_s, u0, u1, u2 = _attn_sample(q_s, caches8, news, slopes, ts)
  oa_s = oa_s[:, :ts * A_HEADS].reshape(bs * ts, A_OUT)
  kv_s = [u.reshape(bs, u.shape[1], 2, A_HEADS, A_HEAD_DIM) for u in (u0, u1, u2)]
  or_s, s_s = _retention(
      xs_pad, w16, padtab(cos_s), padtab(sin_s),
      _retention_tables(ts, cpad), state_ret.astype(F32), gn, cpad, cpad, 8)
  or_s = or_s[:, :ts].reshape(bs * ts, R_V)
  y_s = _output(xs2, oa_s, or_s, w16, wpa, wpb, wo, lng, lnb, 256)

  return (y_p.reshape(bp, tp, D_MODEL), y_s.reshape(bs, ts, D_MODEL),
          kv_p[0], kv_p[1], kv_p[2], s_p,
          kv_s[0], kv_s[1], kv_s[2], s_s)
```

```python
import functools

import numpy as np
import jax
import jax.numpy as jnp
from jax import lax
from jax.experimental import pallas as pl
from jax.experimental.pallas import tpu as pltpu
from jax.experimental.pallas import tpu_sc as plsc

D_MODEL = 1024
PAST_LEN = 16384
A_GROUPS = ((128, 1), (512, 4), (2048, 16))
A_HEADS = 4
A_HEAD_DIM = 128
A_GW = A_HEADS * A_HEAD_DIM
A_QKV = len(A_GROUPS) * A_GW
A_OUT = A_GW
N_DIST = 128
R_HEADS = 4
R_KEY_DIM = 128
R_VAL_DIM = 256
R_QK = R_HEADS * R_KEY_DIM
R_V = R_HEADS * R_VAL_DIM
R_CHUNK = 256
ROPE_BASE = 10000.0
EPS = 1e-5
NEG_INF = -1e30
DEEPNORM_ALPHA = 2.0 ** 0.25
_SPLIT = (A_QKV, A_QKV, A_QKV, A_OUT, R_QK, R_QK, R_V, R_V, D_MODEL, D_MODEL)
_OFF = tuple(int(v) for v in np.concatenate([[0], np.cumsum(_SPLIT)]))

BF16 = jnp.bfloat16
F32 = jnp.float32
VMEM_LIMIT = 56 * 1024 * 1024

_NT = (((1,), (1,)), ((), ()))
_TN = (((0,), (0,)), ((), ()))


def _dot(a, b):
  return jnp.dot(a, b, preferred_element_type=F32)


def _dot_nt(a, b):
  return lax.dot_general(a, b, _NT, preferred_element_type=F32)


def _dot_tn(a, b):
  return lax.dot_general(a, b, _TN, preferred_element_type=F32)


def _params(sem):
  return pltpu.CompilerParams(dimension_semantics=sem, vmem_limit_bytes=VMEM_LIMIT)


def _mm_kernel(x_ref, w_ref, o_ref):
  o_ref[...] = _dot(x_ref[...].astype(BF16), w_ref[...]).astype(o_ref.dtype)


def _matmul(x, w, col0, ncols, out_dtype, tm, tn):
  m, k = x.shape
  c0 = col0 // tn
  return pl.pallas_call(
      _mm_kernel,
      grid=(m // tm, ncols // tn),
      in_specs=[pl.BlockSpec((tm, k), lambda i, j: (i, 0)),
                pl.BlockSpec((k, tn), lambda i, j: (0, c0 + j))],
      out_specs=pl.BlockSpec((tm, tn), lambda i, j: (i, j)),
      out_shape=jax.ShapeDtypeStruct((m, ncols), out_dtype),
      compiler_params=_params(("parallel", "arbitrary")),
      name="proj_matmul",
  )(x, w)


PROJ_TM = 1024


def _proj_group_kernel(x_ref, wq_ref, wk_ref, wv_ref, o_ref, tail_ref, *scratch,
                       d, first_tail, tail_rows):
  i = pl.program_id(1)
  tm = x_ref.shape[0]
  n = tm // d
  x = x_ref[...]
  xb = x.astype(BF16)
  nat = []
  if d > 1:
    xs_ref, xp_ref = scratch
    for c in range(D_MODEL // 128):
      xs_ref[c] = x[:, c * 128:(c + 1) * 128]
  nsplit = 2 if d > 1 else 1
  rper = d // nsplit
  w_refs = (wq_ref, wk_ref, wv_ref)

  def deinterleave(sb):
    for r in range(sb * rper, (sb + 1) * rper):
      for c in range(D_MODEL // 128):
        xp_ref[r * n:(r + 1) * n, c * 128:(c + 1) * 128] = (
            xs_ref[c, pl.ds(r, n, stride=d), :].astype(BF16))

  def project(sb, part):
    perm = xb if d == 1 else xp_ref[sb * rper * n:(sb + 1) * rper * n, :]
    res = _dot(perm, w_refs[part][...])
    nat.append(res)
    for r in range(sb * rper, (sb + 1) * rper):
      lo = (r - sb * rper) * n
      for h in range(A_HEADS):
        o_ref[r, part * A_HEADS + h] = res[lo:lo + n, h * 128:(h + 1) * 128].astype(BF16)

  if d == 1:
    for part in range(3):
      project(0, part)
  else:
    deinterleave(0)
    project(0, 0)
    deinterleave(1)
    project(0, 1)
    project(0, 2)
    for part in range(3):
      project(1, part)

  @pl.when(i >= first_tail)
  def _():
    if d == 1:
      kv = nat[1:]
    else:
      kv = [_dot(xb, wk_ref[...]), _dot(xb, wv_ref[...])]
    for c in range(2 * A_HEADS):
      src = kv[c // A_HEADS][tm - tail_rows:, (c % A_HEADS) * 128:(c % A_HEADS + 1) * 128]
      tail_ref[pl.ds(c, tail_rows, stride=2 * A_HEADS), :] = src


def _proj_group(x2, w16, g, batch, seq):
  w, d = A_GROUPS[g]
  w = min(w, seq)
  tm = PROJ_TM
  nblk = seq // tm
  tail_rows = min(tm, w)
  first_tail = nblk - max(w // tm, 1)
  ls = seq // d
  wspec = lambda part: pl.BlockSpec((D_MODEL, A_GW), lambda b, i: (0, 3 * part + g))
  scratch = []
  if d > 1:
    scratch = [pltpu.VMEM((D_MODEL // 128, tm, 128), F32), pltpu.VMEM((tm, D_MODEL), BF16)]
  return pl.pallas_call(
      functools.partial(_proj_group_kernel, d=d, first_tail=first_tail, tail_rows=tail_rows),
      grid=(batch, nblk),
      in_specs=[pl.BlockSpec((tm, D_MODEL), lambda b, i: (b * nblk + i, 0)),
                wspec(0), wspec(1), wspec(2)],
      out_specs=[pl.BlockSpec((None, d, 3 * A_HEADS, tm // d, 128), lambda b, i: (b, 0, 0, i, 0)),
                 pl.BlockSpec((None, tail_rows * 8, 128),
                              lambda b, i: (b, jnp.maximum(i - first_tail, 0), 0))],
      out_shape=[jax.ShapeDtypeStruct((batch, d, 3 * A_HEADS, ls, 128), BF16),
                 jax.ShapeDtypeStruct((batch, w * 8, 128), F32)],
      scratch_shapes=scratch,
      compiler_params=_params(("parallel", "arbitrary")),
      name="proj_group",
  )(x2, w16, w16, w16)


ATT_BQ = 128
ATT_SET = 4
LOG2E = 1.4426950408889634


def _attn_prompt_kernel(slopes_ref, *refs, seq):
  qkv = refs[:9]
  o_ref = refs[9]
  oacc = refs[10:13]
  lacc = refs[13:16]
  bias_ref, s_buf, p_buf, l_buf = refs[16:20]
  h = pl.program_id(0)
  scale = A_HEAD_DIM ** -0.5

  @pl.when(pl.program_id(1) == 0)
  def _():
    for g, (_, d) in enumerate(A_GROUPS):
      ls = seq // d
      bq = min(ATT_BQ, ls)
      klen = min(bq + N_DIST, ls)
      slope = slopes_ref[g, h]
      qi = lax.broadcasted_iota(jnp.int32, (bq, klen), 0)
      kj = lax.broadcasted_iota(jnp.int32, (bq, klen), 1)
      for which, off in enumerate((0, N_DIST)):
        delta = qi - kj + off
        valid = (delta >= 0) & (delta <= N_DIST)
        dist = (delta * d).astype(F32)
        bias_ref[2 * g + which, :bq, :klen] = jnp.where(valid, -(slope * LOG2E) * dist, NEG_INF)

  for g, (_, d) in enumerate(A_GROUPS):
    q_ref, k_ref, v_ref = qkv[3 * g:3 * g + 3]
    ls = seq // d
    bq = min(ATT_BQ, ls)
    nqb = ls // bq
    klen = min(bq + N_DIST, ls)

    def coords(idx, nqb=nqb, bq=bq):
      r = idx // nqb
      qb = idx % nqb
      q0 = pl.multiple_of(qb * bq, bq)
      k0 = pl.multiple_of(jnp.maximum(qb * bq - N_DIST, 0), N_DIST)
      return r, qb, q0, k0

    def put_rows(ref, idx, val, d=d, bq=bq):
      r, _, q0, _ = coords(idx)
      if d == 1:
        ref[pl.ds(q0, bq), :] = val
      else:
        ref[pl.ds(r + q0 * d, bq, stride=d), :] = val

    def scores(idx, slot, bq=bq, klen=klen, q_ref=q_ref, k_ref=k_ref):
      r, _, q0, k0 = coords(idx)
      s_buf[slot, :bq, :klen] = _dot_nt(q_ref[r, pl.ds(q0, bq), :], k_ref[r, pl.ds(k0, klen), :])

    def softmax(idx, slot, g=g, bq=bq, klen=klen):
      _, qb, _, _ = coords(idx)
      s = (s_buf[slot, :bq, :klen] * (scale * LOG2E)
           + bias_ref[2 * g + jnp.minimum(qb, 1), :bq, :klen])
      m = jnp.max(s, axis=-1, keepdims=True)
      p = jnp.exp2(s - m)
      l = jnp.sum(p, axis=-1, keepdims=True)
      p_buf[slot, :bq, :klen] = p.astype(BF16)
      l_buf[slot, :bq, :] = jnp.broadcast_to(l, (bq, 128))
      put_rows(lacc[g], idx, jnp.broadcast_to(m + jnp.log2(l), (bq, 128)))

    def values(idx, slot, g=g, bq=bq, klen=klen, v_ref=v_ref):
      r, _, _, k0 = coords(idx)
      o = _dot(p_buf[slot, :bq, :klen], v_ref[r, pl.ds(k0, klen), :]) / l_buf[slot, :bq, :]
      put_rows(oacc[g], idx, o)

    nblk = d * nqb
    ns = ATT_SET

    def run(stage, first, half):
      for t in range(ns):
        stage(first + t, half * ns + t)

    run(scores, 0, 0)
    run(softmax, 0, 0)
    run(scores, ns, 1)

    def steady(j, carry):
      i = 2 * ns * (j + 1)
      run(values, i - 2 * ns, 0)
      run(softmax, i - ns, 1)
      run(scores, i, 0)
      run(values, i - ns, 1)
      run(softmax, i, 0)
      run(scores, i + ns, 1)
      return carry

    lax.fori_loop(0, (nblk - 2 * ns) // (2 * ns), steady, 0)
    run(values, nblk - 2 * ns, 0)
    run(softmax, nblk - ns, 1)
    run(values, nblk - ns, 1)

  mrows = 512

  def merge(i, carry):
    r0 = pl.multiple_of(i * mrows, mrows)
    l0 = lacc[0][pl.ds(r0, mrows), :]
    l1 = lacc[1][pl.ds(r0, mrows), :]
    l2 = lacc[2][pl.ds(r0, mrows), :]
    mx = jnp.maximum(jnp.maximum(l0, l1), l2)
    e0 = jnp.exp2(l0 - mx)
    e1 = jnp.exp2(l1 - mx)
    e2 = jnp.exp2(l2 - mx)
    den = e0 + e1 + e2
    acc = (e0 * oacc[0][pl.ds(r0, mrows), :] + e1 * oacc[1][pl.ds(r0, mrows), :]
           + e2 * oacc[2][pl.ds(r0, mrows), :])
    o_ref[pl.ds(r0, mrows), :] = acc / den
    return carry

  lax.fori_loop(0, seq // mrows, merge, 0)


def _attn_prompt(qkvs, slopes, batch, seq):
  in_specs = [pl.BlockSpec(memory_space=pltpu.SMEM)]
  args = [slopes]
  for g, (_, d) in enumerate(A_GROUPS):
    ls = seq // d
    for part in range(3):
      in_specs.append(pl.BlockSpec(
          (None, d, None, ls, 128),
          lambda h, b, part=part: (b, 0, part * A_HEADS + h, 0, 0)))
      args.append(qkvs[g])
  scratch = ([pltpu.VMEM((seq, 128), F32) for _ in range(6)]
             + [pltpu.VMEM((6, ATT_BQ, ATT_BQ + N_DIST), F32),
                pltpu.VMEM((2 * ATT_SET, ATT_BQ, ATT_BQ + N_DIST), F32),
                pltpu.VMEM((2 * ATT_SET, ATT_BQ, ATT_BQ + N_DIST), BF16),
                pltpu.VMEM((2 * ATT_SET, ATT_BQ, 128), F32)])
  return pl.pallas_call(
      functools.partial(_attn_prompt_kernel, seq=seq),
      grid=(A_HEADS, batch),
      in_specs=in_specs,
      out_specs=pl.BlockSpec((None, seq, 128), lambda h, b: (b, 0, h)),
      out_shape=jax.ShapeDtypeStruct((batch, seq, A_OUT), F32),
      scratch_shapes=scratch,
      compiler_params=_params(("parallel", "arbitrary")),
      name="attn_prompt",
  )(*args)


SC_ROLLED = 2
SC_CHUNK_ROWS = 24


def _sc_roll(cache, shift, after):
  batch, w, slots, lanes = cache.shape
  rows = w * slots
  chunk = SC_CHUNK_ROWS * slots
  nchunk = (w - shift) // SC_CHUNK_ROWS
  assert nchunk * SC_CHUNK_ROWS == w - shift
  mesh = plsc.VectorSubcoreMesh(core_axis_name="c", subcore_axis_name="s")
  nsub = mesh.num_subcores
  assert mesh.num_cores * nsub == batch, "one vector subcore per batch entry"

  def body(x_hbm, *rest):
    o_hbm, buf = rest[len(after):]
    base = (lax.axis_index("c") * nsub + lax.axis_index("s")) * rows

    @pl.loop(0, nchunk)
    def _(i):
      pltpu.sync_copy(x_hbm.at[pl.ds(base + shift * slots + i * chunk, chunk)], buf)
      pltpu.sync_copy(buf, o_hbm.at[pl.ds(base + i * chunk, chunk)])

  flat = cache.reshape(batch * rows, lanes)
  out = pl.kernel(body, out_type=jax.ShapeDtypeStruct(flat.shape, flat.dtype), mesh=mesh,
                  scratch_types=[pltpu.VMEM((chunk, lanes), cache.dtype)])(flat, *after)
  return out.reshape(cache.shape)


def _attn_sample_kernel(slopes_ref, q_ref, c0_ref, c1_ref, c2_ref, n0_ref, n1_ref, n2_ref,
                        rolled_ref, o_ref, u0_ref, u1_ref, u2_ref, sems, bias_ref, *, tq):
  b = pl.program_id(0)
  cache_refs = (c0_ref, c1_ref, c2_ref)
  new_refs = (n0_ref, n1_ref, n2_ref)
  out_refs = (u0_ref, u1_ref, u2_ref)
  del rolled_ref

  copies = []
  for g, (w, _) in enumerate(A_GROUPS):
    tail = pltpu.make_async_copy(new_refs[g].at[0], out_refs[g].at[b, pl.ds(w - tq, tq)],
                                 sems.at[2 * g + 1])
    tail.start()
    copies.append(tail)
    if g != SC_ROLLED:
      body = pltpu.make_async_copy(cache_refs[g].at[0, pl.ds(tq, w - tq)],
                                   out_refs[g].at[b, pl.ds(0, w - tq)], sems.at[2 * g])
      body.start()
      copies.append(body)

  scale = A_HEAD_DIM ** -0.5
  nreal = N_DIST * 8 + tq * 8
  nrow = -(-nreal // 128) * 128

  @pl.when(b == 0)
  def _():
    ri = lax.broadcasted_iota(jnp.int32, (nrow, 128), 0)
    cj = lax.broadcasted_iota(jnp.int32, (nrow, 128), 1)
    slot = ri % 8
    key = ri // 8
    tcol = cj // A_HEADS
    hcol = cj % A_HEADS
    col_ok = cj < tq * A_HEADS
    for g, (_, d) in enumerate(A_GROUPS):
      qt = tcol // d
      nidx = key - N_DIST
      in_cache = key < N_DIST
      valid_c = in_cache & (key >= qt)
      dist_c = d * (N_DIST + qt - key)
      valid_n = (~in_cache) & (nidx <= tcol) & ((tcol - nidx) % d == 0)
      dist_n = tcol - nidx
      valid = (valid_c | valid_n) & (slot == hcol) & col_ok & (ri < nreal)
      dist = jnp.where(in_cache, dist_c, dist_n).astype(F32)
      slope = jnp.zeros((nrow, 128), F32)
      for hh in range(A_HEADS):
        slope = jnp.where(hcol == hh, slopes_ref[g, hh], slope)
      bias_ref[g] = jnp.where(valid, -slope * dist, NEG_INF)

  qrow_t = lax.broadcasted_iota(jnp.int32, (128, 1), 0) // A_HEADS
  xpad = jnp.zeros((nrow - nreal, 128), F32)
  lse_g = []
  o_g = []
  for g, (w, d) in enumerate(A_GROUPS):
    qg = q_ref[g]
    xn = new_refs[g][0].reshape(tq * 8, 128)
    nres = min(d, tq)
    xs, qs = [], []
    for rho in range(nres):
      if g == SC_ROLLED:
        xc = cache_refs[g][0, :, rho, :, :]
      elif d == 1:
        xc = cache_refs[g][0]
      else:
        xc = cache_refs[g][0, pl.ds(rho, N_DIST, stride=d), :, :]
      xc = xc.reshape(N_DIST * 8, 128)
      xs.append(jnp.concatenate([xc, xn, xpad], axis=0).astype(BF16))
      qs.append(qg if d == 1 else jnp.where(qrow_t % d == rho, qg, 0.0))
    xcat = xs[0] if nres == 1 else jnp.concatenate(xs, axis=1)
    qcat = (qs[0] if nres == 1 else jnp.concatenate(qs, axis=1)).astype(BF16)
    s = _dot_nt(xcat, qcat) * scale + bias_ref[g]
    m = jnp.max(s, axis=0, keepdims=True)
    p = jnp.exp(s - m)
    l = jnp.sum(p, axis=0, keepdims=True)
    pn = p / l
    pvt = pltpu.roll(pn, 4, 0).T.astype(BF16)
    full = _dot(pvt, xcat)
    o = full[:, :128]
    for rho in range(1, nres):
      o = jnp.where(qrow_t % d == rho, full[:, rho * 128:(rho + 1) * 128], o)
    o_g.append(o)
    lse_g.append(jnp.broadcast_to(m + jnp.log(l), (128, 128)).T)
  mx = jnp.maximum(jnp.maximum(lse_g[0], lse_g[1]), lse_g[2])
  es = [jnp.exp(v - mx) for v in lse_g]
  den = es[0] + es[1] + es[2]
  o_ref[...] = (es[0] * o_g[0] + es[1] * o_g[1] + es[2] * o_g[2]) / den

  for c in copies:
    c.wait()


def _attn_sample(q, caches, news, slopes, tq, after):
  batch = q.shape[0]
  in_specs = [pl.BlockSpec(memory_space=pltpu.SMEM),
              pl.BlockSpec((None, 3, 128, 128), lambda b: (b, 0, 0, 0))]
  args = [slopes, q]
  for g, (w, d) in enumerate(A_GROUPS):
    if g == SC_ROLLED:
      in_specs.append(pl.BlockSpec((1, N_DIST, min(d, tq), 8, 128), lambda b: (b, 0, 0, 0, 0)))
      args.append(caches[g].reshape(batch, N_DIST, d, 8, 128))
    else:
      in_specs.append(pl.BlockSpec((1, w, 8, 128), lambda b: (b, 0, 0, 0)))
      args.append(caches[g])
  for g in range(3):
    in_specs.append(pl.BlockSpec((1, tq, 8, 128), lambda b: (b, 0, 0, 0)))
    args.append(news[g])
  in_specs.append(pl.BlockSpec(memory_space=pl.ANY))
  args.append(_sc_roll(caches[SC_ROLLED], tq, after))
  rolled_in = len(args) - 1
  out_shape = [jax.ShapeDtypeStruct((batch, 128, 128), F32)]
  out_specs = [pl.BlockSpec((None, 128, 128), lambda b: (b, 0, 0))]
  for g, (w, _) in enumerate(A_GROUPS):
    out_shape.append(jax.ShapeDtypeStruct((batch, w, 8, 128), F32))
    out_specs.append(pl.BlockSpec(memory_space=pl.ANY))
  return pl.pallas_call(
      functools.partial(_attn_sample_kernel, tq=tq),
      grid=(batch,),
      in_specs=in_specs,
      out_specs=out_specs,
      out_shape=out_shape,
      input_output_aliases={rolled_in: 1 + SC_ROLLED},
      scratch_shapes=[pltpu.SemaphoreType.DMA((2 * len(A_GROUPS),)),
                      pltpu.VMEM((len(A_GROUPS), -(-(N_DIST + tq) * 8 // 128) * 128, 128), F32)],
      compiler_params=_params(("arbitrary",)),
      name="attn_sample",
  )(*args)


def _retention_kernel(x_ref, wqk_ref, wv_ref, cos_ref, sin_ref, dmat_ref, qdec_ref, kdec_ref,
                      cdec_ref, s0_ref, gn_ref, o_ref, sout_ref, state, qk_ref, v_ref, *, chunk):
  j = pl.program_id(1)
  nb, tc, _ = x_ref.shape

  @pl.when(j == 0)
  def _():
    state[...] = s0_ref[...]

  xb = x_ref[...].reshape(nb * tc, D_MODEL).astype(BF16)
  qk_ref[...] = _dot(xb, wqk_ref[...])
  v_ref[...] = _dot(xb, wv_ref[...]).astype(BF16)
  kscale = R_KEY_DIM ** -0.5
  for bi in range(nb):
    for ci in range(tc // chunk):
      rows = pl.ds(bi * tc + ci * chunk, chunk)
      seq_rows = pl.ds(ci * chunk, chunk)
      cosv = cos_ref[seq_rows, :]
      sinv = sin_ref[seq_rows, :]
      for h in range(R_HEADS):
        qh = qk_ref[rows, h * 128:(h + 1) * 128]
        kh = qk_ref[rows, R_QK + h * 128:R_QK + (h + 1) * 128]
        qrot = qh * cosv + pltpu.roll(qh, 64, 1) * sinv
        krot = (kh * cosv + pltpu.roll(kh, 64, 1) * sinv) * kscale
        qb = qrot.astype(BF16)
        vh = v_ref[rows, h * 256:(h + 1) * 256]
        att = _dot_nt(qb, krot.astype(BF16)) * dmat_ref[h]
        sh = state[bi, h]
        o = _dot(att.astype(BF16), vh) + _dot(qb, sh.astype(BF16)) * qdec_ref[h]
        state[bi, h] = cdec_ref[h] * sh + _dot_tn((krot * kdec_ref[h]).astype(BF16), vh)
        mu = jnp.mean(o, axis=-1, keepdims=True)
        var = jnp.mean(jnp.square(o - mu), axis=-1, keepdims=True)
        on = (o - mu) * lax.rsqrt(var + EPS) * gn_ref[:, h * 256:(h + 1) * 256]
        o_ref[bi, seq_rows, h * 256:(h + 1) * 256] = on

  @pl.when(j == pl.num_programs(1) - 1)
  def _():
    sout_ref[...] = state[...]


def _retention(x, w16, cos, sin, tables, s0, gn, rows_per_step, chunk, nb):
  batch, t, _ = x.shape
  dmat, qdec, kdec, cdec = tables
  tc = rows_per_step
  const3 = lambda b, j: (0, 0, 0)
  wslab = lambda off: pl.BlockSpec((D_MODEL, R_V), lambda b, j: (0, off // R_V),
                                   pipeline_mode=pl.Buffered(1))
  state_spec = pl.BlockSpec((nb, R_HEADS, R_KEY_DIM, R_VAL_DIM), lambda b, j: (b, 0, 0, 0))
  return pl.pallas_call(
      functools.partial(_retention_kernel, chunk=chunk),
      grid=(batch // nb, t // tc),
      in_specs=[pl.BlockSpec((nb, tc, D_MODEL), lambda b, j: (b, j, 0)),
                wslab(_OFF[4]), wslab(_OFF[6]),
                pl.BlockSpec((tc, 128), lambda b, j: (j, 0)),
                pl.BlockSpec((tc, 128), lambda b, j: (j, 0)),
                pl.BlockSpec(dmat.shape, const3),
                pl.BlockSpec(qdec.shape, const3),
                pl.BlockSpec(kdec.shape, const3),
                pl.BlockSpec(cdec.shape, const3),
                state_spec,
                pl.BlockSpec((1, R_V), lambda b, j: (0, 0))],
      out_specs=[pl.BlockSpec((nb, tc, R_V), lambda b, j: (b, j, 0)), state_spec],
      out_shape=[jax.ShapeDtypeStruct((batch, t, R_V), F32),
                 jax.ShapeDtypeStruct((batch, R_HEADS, R_KEY_DIM, R_VAL_DIM), F32)],
      scratch_shapes=[pltpu.VMEM((nb, R_HEADS, R_KEY_DIM, R_VAL_DIM), F32),
                      pltpu.VMEM((nb * tc, 2 * R_QK), F32), pltpu.VMEM((nb * tc, R_V), BF16)],
      compiler_params=_params(("parallel", "arbitrary")),
      name="retention",
  )(x, w16, w16, cos, sin, dmat, qdec, kdec, cdec, s0, gn)


def _retention_tables(c_true, c_pad):
  lg = jnp.log1p(-jnp.exp2(-5.0 - jnp.arange(R_HEADS, dtype=F32)))
  i = jnp.arange(c_pad, dtype=F32)
  live = i < c_true
  diff = i[:, None] - i[None, :]
  causal = (diff >= 0) & live[:, None] & live[None, :]
  dmat = jnp.where(causal[None], jnp.exp(jnp.where(causal, diff, 0.0)[None] * lg[:, None, None]), 0.0)
  qdec = jnp.where(live[None, :], jnp.exp((i[None, :] + 1.0) * lg[:, None]), 0.0)
  kdec = jnp.where(live[None, :], jnp.exp((c_true - 1.0 - i)[None, :] * lg[:, None]), 0.0)
  cdec = jnp.exp(c_true * lg)
  qdec = jnp.broadcast_to(qdec[:, :, None], (R_HEADS, c_pad, R_VAL_DIM))
  kdec = jnp.broadcast_to(kdec[:, :, None], (R_HEADS, c_pad, R_KEY_DIM))
  cdec = jnp.broadcast_to(cdec[:, None, None], (R_HEADS, R_KEY_DIM, R_VAL_DIM))
  return dmat, qdec, kdec, cdec


def _rope_tables(pos):
  half = R_KEY_DIM // 2
  inv = 1.0 / (ROPE_BASE ** jnp.linspace(0.0, 1.0, half, dtype=F32))
  ang = pos.astype(F32)[:, None] * inv[None, :]
  cos = jnp.cos(ang)
  sin = jnp.sin(ang)
  return jnp.concatenate([cos, cos], axis=-1), jnp.concatenate([-sin, sin], axis=-1)


def _output_kernel(x_ref, oa_ref, or_ref, wza_ref, wzr_ref, wga_ref, wgb_ref, wpa_ref, wpb_ref,
                   wo_ref, lng_ref, lnb_ref, y_ref):
  tm = x_ref.shape[0]
  nsub = 2 if tm % 32 == 0 else 1
  sub = tm // nsub
  for si in range(nsub):
    rows = pl.ds(si * sub, sub)
    x = x_ref[rows, :]
    xb = x.astype(BF16)
    za = _dot(xb, wza_ref[...])
    ya = _dot((jax.nn.silu(za) * oa_ref[rows, :]).astype(BF16), wpa_ref[...])
    zr = _dot(xb, wzr_ref[...])
    yb = _dot((jax.nn.silu(zr) * or_ref[rows, :]).astype(BF16), wpb_ref[...])
    ga = _dot(xb, wga_ref[...])
    gb = _dot(xb, wgb_ref[...])
    mix = jax.nn.sigmoid(ga) * ya + jax.nn.sigmoid(gb) * yb
    out = _dot(mix.astype(BF16), wo_ref[...])
    zz = DEEPNORM_ALPHA * x + out
    zm = jnp.mean(zz, axis=-1, keepdims=True)
    zv = jnp.mean(jnp.square(zz - zm), axis=-1, keepdims=True)
    y_ref[rows, :] = (zz - zm) * lax.rsqrt(zv + EPS) * lng_ref[...] + lnb_ref[...]


def _output(x, oa, orr, w16, wpa, wpb, wo, lng, lnb, tm):
  m = x.shape[0]
  row = lambda w: pl.BlockSpec((tm, w), lambda i: (i, 0))
  full = lambda a: pl.BlockSpec(a.shape, lambda i: (0, 0), pipeline_mode=pl.Buffered(1))
  wcol = lambda off, width: pl.BlockSpec((D_MODEL, width), lambda i: (0, off // width),
                                         pipeline_mode=pl.Buffered(1))
  in_specs = [row(D_MODEL), row(A_OUT), row(R_V),
              wcol(_OFF[3], A_OUT), wcol(_OFF[7], R_V), wcol(_OFF[8], D_MODEL),
              wcol(_OFF[9], D_MODEL),
              full(wpa), full(wpb), full(wo), full(lng), full(lnb)]
  args = [x, oa, orr, w16, w16, w16, w16, wpa, wpb, wo, lng, lnb]
  return pl.pallas_call(
      _output_kernel,
      grid=(m // tm,),
      in_specs=in_specs,
      out_specs=row(D_MODEL),
      out_shape=jax.ShapeDtypeStruct((m, D_MODEL), F32),
      compiler_params=_params(("parallel",)),
      name="gates_output",
  )(*args)


def _alibi_slopes():
  n = len(A_GROUPS) * A_HEADS
  return jnp.exp2(-8.0 * jnp.arange(1, n + 1, dtype=F32) / n).reshape(len(A_GROUPS), A_HEADS)


def kernel(x_prompt, x_sample, cache_kv_w128, cache_kv_w512, cache_kv_w2048, state_ret,
           w_in, w_pa, w_pb, w_o, gn_g, ln_g, ln_b):
  bp, tp, _ = x_prompt.shape
  bs, ts, _ = x_sample.shape
  slopes = _alibi_slopes()
  w16 = w_in.astype(BF16)
  wpa = w_pa.astype(BF16)
  wpb = w_pb.astype(BF16)
  wo = w_o.astype(BF16)
  gn = gn_g.reshape(1, R_V)
  lng = ln_g.reshape(1, D_MODEL)
  lnb = ln_b.reshape(1, D_MODEL)

  xp2 = x_prompt.reshape(bp * tp, D_MODEL)
  qkvs, kv_p, tails = [], [], []
  for g, (w, _) in enumerate(A_GROUPS):
    qkv, tail = _proj_group(xp2, w16, g, bp, tp)
    qkvs.append(qkv)
    tails.append(tail)
    kv_p.append(tail.reshape(bp, min(w, tp), 2, A_HEADS, A_HEAD_DIM))
  oa_p = _attn_prompt(qkvs, slopes, bp, tp).reshape(bp * tp, A_OUT)

  cos_p, sin_p = _rope_tables(jnp.arange(tp, dtype=jnp.int32))
  or_p, s_p = _retention(
      x_prompt, w16, cos_p, sin_p, _retention_tables(R_CHUNK, R_CHUNK),
      jnp.zeros((bp, R_HEADS, R_KEY_DIM, R_VAL_DIM), F32), gn, 1024, R_CHUNK, 1)
  y_p = _output(xp2, oa_p, or_p.reshape(bp * tp, R_V), w16, wpa, wpb, wo, lng, lnb, 512)
  xs2 = x_sample.reshape(bs * ts, D_MODEL)
  hs = _matmul(xs2, w16, 0, _OFF[3], F32, bs * ts, A_GW)
  caches = (cache_kv_w128, cache_kv_w512, cache_kv_w2048)
  q_s, news = [], []
  for g in range(3):
    col = lambda part: hs[:, _OFF[part] + g * A_GW:_OFF[part] + (g + 1) * A_GW]
    qg = col(0).reshape(bs, ts * A_HEADS, A_HEAD_DIM)
    q_s.append(jnp.pad(qg, ((0, 0), (0, 128 - ts * A_HEADS), (0, 0))))
    news.append(jnp.concatenate([col(1), col(2)], axis=1).reshape(bs, ts, 8, 128))
  q_s = jnp.stack(q_s, axis=1)
  caches8 = [c.reshape(bs, c.shape[1], 8, 128) for c in caches]
  cpad = 16
  xs_pad = jnp.pad(x_sample, ((0, 0), (0, cpad - ts), (0, 0)))
  cos_s, sin_s = _rope_tables(PAST_LEN + jnp.arange(ts, dtype=jnp.int32))
  padtab = lambda a: jnp.pad(a, ((0, cpad - ts), (0, 0)))
  after = [t[0, :8] for t in tails]
  oa_s, u0, u1, u2 = _attn_sample(q_s, caches8, news, slopes, ts, after)
  oa_s = oa_s[:, :ts * A_HEADS].reshape(bs * ts, A_OUT)
  kv_s = [u.reshape(bs, u.shape[1], 2, A_HEADS, A_HEAD_DIM) for u in (u0, u1, u2)]
  or_s, s_s = _retention(
      xs_pad, w16, padtab(cos_s), padtab(sin_s),
      _retention_tables(ts, cpad), state_ret.astype(F32), gn, cpad, cpad, 8)
  or_s = or_s[:, :ts].reshape(bs * ts, R_V)
  y_s = _output(xs2, oa_s, or_s, w16, wpa, wpb, wo, lng, lnb, 256)

  return (y_p.reshape(bp, tp, D_MODEL), y_s.reshape(bs, ts, D_MODEL),
          kv_p[0], kv_p[1], kv_p[2], s_p,
          kv_s[0], kv_s[1], kv_s[2], s_s)
```

```python
import functools

import numpy as np
import jax
import jax.numpy as jnp
from jax import lax
from jax.experimental import pallas as pl
from jax.experimental.pallas import tpu as pltpu
from jax.experimental.pallas import tpu_sc as plsc

D_MODEL = 1024
PAST_LEN = 16384
A_GROUPS = ((128, 1), (512, 4), (2048, 16))
A_HEADS = 4
A_HEAD_DIM = 128
A_GW = A_HEADS * A_HEAD_DIM
A_QKV = len(A_GROUPS) * A_GW
A_OUT = A_GW
N_DIST = 128
R_HEADS = 4
R_KEY_DIM = 128
R_VAL_DIM = 256
R_QK = R_HEADS * R_KEY_DIM
R_V = R_HEADS * R_VAL_DIM
R_CHUNK = 256
ROPE_BASE = 10000.0
EPS = 1e-5
NEG_INF = -1e30
DEEPNORM_ALPHA = 2.0 ** 0.25
_SPLIT = (A_QKV, A_QKV, A_QKV, A_OUT, R_QK, R_QK, R_V, R_V, D_MODEL, D_MODEL)
_OFF = tuple(int(v) for v in np.concatenate([[0], np.cumsum(_SPLIT)]))

BF16 = jnp.bfloat16
F32 = jnp.float32
VMEM_LIMIT = 56 * 1024 * 1024
RET_ROWS = 1024
OUT_ROWS = 512
SAMPLE_OUT_ROWS = 256
SAMPLE_CHUNK_PAD = 16
SAMPLE_RET_NB = 8

_NT = (((1,), (1,)), ((), ()))
_TN = (((0,), (0,)), ((), ()))


def _dot(a, b):
  return jnp.dot(a, b, preferred_element_type=F32)


def _dot_nt(a, b):
  return lax.dot_general(a, b, _NT, preferred_element_type=F32)


def _dot_tn(a, b):
  return lax.dot_general(a, b, _TN, preferred_element_type=F32)


def _params(sem):
  return pltpu.CompilerParams(dimension_semantics=sem, vmem_limit_bytes=VMEM_LIMIT)


def _mm_kernel(x_ref, w_ref, o_ref):
  o_ref[...] = _dot(x_ref[...].astype(BF16), w_ref[...]).astype(o_ref.dtype)


def _matmul(x, w, col0, ncols, out_dtype, tm, tn):
  m, k = x.shape
  c0 = col0 // tn
  return pl.pallas_call(
      _mm_kernel,
      grid=(m // tm, ncols // tn),
      in_specs=[pl.BlockSpec((tm, k), lambda i, j: (i, 0)),
                pl.BlockSpec((k, tn), lambda i, j: (0, c0 + j))],
      out_specs=pl.BlockSpec((tm, tn), lambda i, j: (i, j)),
      out_shape=jax.ShapeDtypeStruct((m, ncols), out_dtype),
      compiler_params=_params(("parallel", "arbitrary")),
      name="proj_matmul",
  )(x, w)


PROJ_TM = 1024


def _proj_group_kernel(x_ref, wq_ref, wk_ref, wv_ref, o_ref, tail_ref, *scratch,
                       d, first_tail, tail_rows):
  i = pl.program_id(1)
  tm = x_ref.shape[0]
  n = tm // d
  x = x_ref[...]
  xb = x.astype(BF16)
  if d == 1:
    perm = xb
  else:
    xs_ref, xp_ref = scratch
    for c in range(D_MODEL // 128):
      xs_ref[c] = x[:, c * 128:(c + 1) * 128]
    for r in range(d):
      for c in range(D_MODEL // 128):
        xp_ref[r * n:(r + 1) * n, c * 128:(c + 1) * 128] = (
            xs_ref[c, pl.ds(r, n, stride=d), :].astype(BF16))
    perm = xp_ref[...]
  nat = []
  for part, w_ref in enumerate((wq_ref, wk_ref, wv_ref)):
    res = _dot(perm, w_ref[...])
    nat.append(res)
    for r in range(d):
      for h in range(A_HEADS):
        o_ref[r, part * A_HEADS + h] = res[r * n:(r + 1) * n, h * 128:(h + 1) * 128].astype(BF16)

  @pl.when(i >= first_tail)
  def _():
    if d == 1:
      kv = nat[1:]
    else:
      kv = [_dot(xb, wk_ref[...]), _dot(xb, wv_ref[...])]
    for c in range(2 * A_HEADS):
      src = kv[c // A_HEADS][tm - tail_rows:, (c % A_HEADS) * 128:(c % A_HEADS + 1) * 128]
      tail_ref[pl.ds(c, tail_rows, stride=2 * A_HEADS), :] = src


def _proj_group(x2, w16, g, batch, seq):
  w, d = A_GROUPS[g]
  w = min(w, seq)
  tm = PROJ_TM
  nblk = seq // tm
  tail_rows = min(tm, w)
  first_tail = nblk - max(w // tm, 1)
  ls = seq // d
  wspec = lambda part: pl.BlockSpec((D_MODEL, A_GW), lambda b, i: (0, 3 * part + g))
  scratch = []
  if d > 1:
    scratch = [pltpu.VMEM((D_MODEL // 128, tm, 128), F32), pltpu.VMEM((tm, D_MODEL), BF16)]
  return pl.pallas_call(
      functools.partial(_proj_group_kernel, d=d, first_tail=first_tail, tail_rows=tail_rows),
      grid=(batch, nblk),
      in_specs=[pl.BlockSpec((tm, D_MODEL), lambda b, i: (b * nblk + i, 0)),
                wspec(0), wspec(1), wspec(2)],
      out_specs=[pl.BlockSpec((None, d, 3 * A_HEADS, tm // d, 128), lambda b, i: (b, 0, 0, i, 0)),
                 pl.BlockSpec((None, tail_rows * 8, 128),
                              lambda b, i: (b, jnp.maximum(i - first_tail, 0), 0))],
      out_shape=[jax.ShapeDtypeStruct((batch, d, 3 * A_HEADS, ls, 128), BF16),
                 jax.ShapeDtypeStruct((batch, w * 8, 128), F32)],
      scratch_shapes=scratch,
      compiler_params=_params(("parallel", "arbitrary")),
      name="proj_group",
  )(x2, w16, w16, w16)


ATT_BQ = 128
ATT_SET = 4
LOG2E = 1.4426950408889634


def _attn_prompt_kernel(slopes_ref, *refs, seq):
  qkv = refs[:9]
  o_ref = refs[9]
  oacc = refs[10:13]
  lacc = refs[13:16]
  bias_ref, s_buf, p_buf, l_buf = refs[16:20]
  h = pl.program_id(0)
  scale = A_HEAD_DIM ** -0.5

  @pl.when(pl.program_id(1) == 0)
  def _():
    for g, (_, d) in enumerate(A_GROUPS):
      ls = seq // d
      bq = min(ATT_BQ, ls)
      klen = min(bq + N_DIST, ls)
      slope = slopes_ref[g, h]
      qi = lax.broadcasted_iota(jnp.int32, (bq, klen), 0)
      kj = lax.broadcasted_iota(jnp.int32, (bq, klen), 1)
      for which, off in enumerate((0, N_DIST)):
        delta = qi - kj + off
        valid = (delta >= 0) & (delta <= N_DIST)
        dist = (delta * d).astype(F32)
        bias_ref[2 * g + which, :bq, :klen] = jnp.where(valid, -(slope * LOG2E) * dist, NEG_INF)

  for g, (_, d) in enumerate(A_GROUPS):
    q_ref, k_ref, v_ref = qkv[3 * g:3 * g + 3]
    ls = seq // d
    bq = min(ATT_BQ, ls)
    nqb = ls // bq
    klen = min(bq + N_DIST, ls)

    def coords(idx, nqb=nqb, bq=bq):
      r = idx // nqb
      qb = idx % nqb
      q0 = pl.multiple_of(qb * bq, bq)
      k0 = pl.multiple_of(jnp.maximum(qb * bq - N_DIST, 0), N_DIST)
      return r, qb, q0, k0

    def put_rows(ref, idx, val, d=d, bq=bq):
      r, _, q0, _ = coords(idx)
      if d == 1:
        ref[pl.ds(q0, bq), :] = val
      else:
        ref[pl.ds(r + q0 * d, bq, stride=d), :] = val

    def scores(idx, slot, bq=bq, klen=klen, q_ref=q_ref, k_ref=k_ref):
      r, _, q0, k0 = coords(idx)
      s_buf[slot, :bq, :klen] = _dot_nt(q_ref[r, pl.ds(q0, bq), :], k_ref[r, pl.ds(k0, klen), :])

    def softmax(idx, slot, g=g, bq=bq, klen=klen):
      _, qb, _, _ = coords(idx)
      s = (s_buf[slot, :bq, :klen] * (scale * LOG2E)
           + bias_ref[2 * g + jnp.minimum(qb, 1), :bq, :klen])
      m = jnp.max(s, axis=-1, keepdims=True)
      p = jnp.exp2(s - m)
      l = jnp.sum(p, axis=-1, keepdims=True)
      p_buf[slot, :bq, :klen] = p.astype(BF16)
      l_buf[slot, :bq, :] = jnp.broadcast_to(l, (bq, 128))
      put_rows(lacc[g], idx, jnp.broadcast_to(m + jnp.log2(l), (bq, 128)))

    def values(idx, slot, g=g, bq=bq, klen=klen, v_ref=v_ref):
      r, _, _, k0 = coords(idx)
      o = _dot(p_buf[slot, :bq, :klen], v_ref[r, pl.ds(k0, klen), :]) / l_buf[slot, :bq, :]
      put_rows(oacc[g], idx, o)

    nblk = d * nqb
    ns = ATT_SET

    def run(stage, first, half):
      for t in range(ns):
        stage(first + t, half * ns + t)

    run(scores, 0, 0)
    run(softmax, 0, 0)
    run(scores, ns, 1)

    def steady(j, carry):
      i = 2 * ns * (j + 1)
      run(values, i - 2 * ns, 0)
      run(softmax, i - ns, 1)
      run(scores, i, 0)
      run(values, i - ns, 1)
      run(softmax, i, 0)
      run(scores, i + ns, 1)
      return carry

    lax.fori_loop(0, (nblk - 2 * ns) // (2 * ns), steady, 0)
    run(values, nblk - 2 * ns, 0)
    run(softmax, nblk - ns, 1)
    run(values, nblk - ns, 1)

  mrows = 512

  def merge(i, carry):
    r0 = pl.multiple_of(i * mrows, mrows)
    l0 = lacc[0][pl.ds(r0, mrows), :]
    l1 = lacc[1][pl.ds(r0, mrows), :]
    l2 = lacc[2][pl.ds(r0, mrows), :]
    mx = jnp.maximum(jnp.maximum(l0, l1), l2)
    e0 = jnp.exp2(l0 - mx)
    e1 = jnp.exp2(l1 - mx)
    e2 = jnp.exp2(l2 - mx)
    den = e0 + e1 + e2
    acc = (e0 * oacc[0][pl.ds(r0, mrows), :] + e1 * oacc[1][pl.ds(r0, mrows), :]
           + e2 * oacc[2][pl.ds(r0, mrows), :])
    o_ref[pl.ds(r0, mrows), :] = acc / den
    return carry

  lax.fori_loop(0, seq // mrows, merge, 0)


def _attn_prompt(qkvs, slopes, batch, seq):
  in_specs = [pl.BlockSpec(memory_space=pltpu.SMEM)]
  args = [slopes]
  for g, (_, d) in enumerate(A_GROUPS):
    ls = seq // d
    for part in range(3):
      in_specs.append(pl.BlockSpec(
          (None, d, None, ls, 128),
          lambda h, b, part=part: (b, 0, part * A_HEADS + h, 0, 0)))
      args.append(qkvs[g])
  scratch = ([pltpu.VMEM((seq, 128), F32) for _ in range(6)]
             + [pltpu.VMEM((6, ATT_BQ, ATT_BQ + N_DIST), F32),
                pltpu.VMEM((2 * ATT_SET, ATT_BQ, ATT_BQ + N_DIST), F32),
                pltpu.VMEM((2 * ATT_SET, ATT_BQ, ATT_BQ + N_DIST), BF16),
                pltpu.VMEM((2 * ATT_SET, ATT_BQ, 128), F32)])
  return pl.pallas_call(
      functools.partial(_attn_prompt_kernel, seq=seq),
      grid=(A_HEADS, batch),
      in_specs=in_specs,
      out_specs=pl.BlockSpec((None, seq, 128), lambda h, b: (b, 0, h)),
      out_shape=jax.ShapeDtypeStruct((batch, seq, A_OUT), F32),
      scratch_shapes=scratch,
      compiler_params=_params(("parallel", "arbitrary")),
      name="attn_prompt",
  )(*args)


SC_ROLLED = 2
SC_CHUNK_ROWS = 24
SAMPLE_NB = 1


def _sc_roll(caches, shift, after):
  batch, _, slots, lanes = caches[0].shape
  chunk = SC_CHUNK_ROWS * slots
  rows = [c.shape[1] * slots for c in caches]
  nchunks = [(c.shape[1] - shift) // SC_CHUNK_ROWS for c in caches]
  assert all(n * SC_CHUNK_ROWS == c.shape[1] - shift for n, c in zip(nchunks, caches))
  mesh = plsc.VectorSubcoreMesh(core_axis_name="c", subcore_axis_name="s")
  nsub = mesh.num_subcores
  assert mesh.num_cores * nsub == batch, "one vector subcore per batch entry"
  n = len(caches)

  def body(*refs):
    x_hbm = refs[:n]
    o_hbm = refs[n + len(after):2 * n + len(after)]
    buf = refs[-1]
    entry = lax.axis_index("c") * nsub + lax.axis_index("s")
    for g in range(n):
      base = entry * rows[g]

      @pl.loop(0, nchunks[g])
      def _(i, g=g, base=base):
        pltpu.sync_copy(x_hbm[g].at[pl.ds(base + shift * slots + i * chunk, chunk)], buf)
        pltpu.sync_copy(buf, o_hbm[g].at[pl.ds(base + i * chunk, chunk)])

  flats = [c.reshape(batch * r, lanes) for c, r in zip(caches, rows)]
  outs = pl.kernel(body, out_type=[jax.ShapeDtypeStruct(f.shape, f.dtype) for f in flats],
                   mesh=mesh,
                   scratch_types=[pltpu.VMEM((chunk, lanes), caches[0].dtype)])(*flats, *after)
  return [o.reshape(c.shape) for o, c in zip(outs, caches)]


def _attn_sample_kernel(slopes_ref, q_ref, c0_ref, c1_ref, c2_ref, n0_ref, n1_ref, n2_ref,
                        r0_ref, r1_ref, r2_ref, o_ref, u0_ref, u1_ref, u2_ref, sems, bias_ref,
                        *, tq):
  b = pl.program_id(0)
  cache_refs = (c0_ref, c1_ref, c2_ref)
  new_refs = (n0_ref, n1_ref, n2_ref)
  out_refs = (u0_ref, u1_ref, u2_ref)
  del r0_ref, r1_ref, r2_ref

  nb = q_ref.shape[0]
  copies = []
  for e in range(nb):
    for g, (w, _) in enumerate(A_GROUPS):
      tail = pltpu.make_async_copy(new_refs[g].at[e], out_refs[g].at[b * nb + e, pl.ds(w - tq, tq)],
                                   sems.at[e * len(A_GROUPS) + g])
      tail.start()
      copies.append(tail)

  scale = A_HEAD_DIM ** -0.5
  nreal = N_DIST * 8 + tq * 8
  nrow = -(-nreal // 128) * 128

  @pl.when(b == 0)
  def _():
    ri = lax.broadcasted_iota(jnp.int32, (nrow, 128), 0)
    cj = lax.broadcasted_iota(jnp.int32, (nrow, 128), 1)
    slot = ri % 8
    key = ri // 8
    tcol = cj // A_HEADS
    hcol = cj % A_HEADS
    col_ok = cj < tq * A_HEADS
    for g, (_, d) in enumerate(A_GROUPS):
      qt = tcol // d
      nidx = key - N_DIST
      in_cache = key < N_DIST
      valid_c = in_cache & (key >= qt)
      dist_c = d * (N_DIST + qt - key)
      valid_n = (~in_cache) & (nidx <= tcol) & ((tcol - nidx) % d == 0)
      dist_n = tcol - nidx
      valid = (valid_c | valid_n) & (slot == hcol) & col_ok & (ri < nreal)
      dist = jnp.where(in_cache, dist_c, dist_n).astype(F32)
      slope = jnp.zeros((nrow, 128), F32)
      for hh in range(A_HEADS):
        slope = jnp.where(hcol == hh, slopes_ref[g, hh], slope)
      bias_ref[g] = jnp.where(valid, -slope * dist, NEG_INF)

  qrow_t = lax.broadcasted_iota(jnp.int32, (128, 1), 0) // A_HEADS
  xpad = jnp.zeros((nrow - nreal, 128), F32)
  for e in range(nb):
    _attend_entry(e, q_ref, cache_refs, new_refs, bias_ref, o_ref, qrow_t, xpad, scale, tq)

  for c in copies:
    c.wait()


def _attend_entry(e, q_ref, cache_refs, new_refs, bias_ref, o_ref, qrow_t, xpad, scale, tq):
  lse_g = []
  o_g = []
  for g, (w, d) in enumerate(A_GROUPS):
    qg = q_ref[e, g]
    xn = new_refs[g][e].reshape(tq * 8, 128)
    nres = min(d, tq)
    xs, qs = [], []
    for rho in range(nres):
      if g == SC_ROLLED:
        xc = cache_refs[g][e, :, rho, :, :]
      elif d == 1:
        xc = cache_refs[g][e]
      else:
        xc = cache_refs[g][e, pl.ds(rho, N_DIST, stride=d), :, :]
      xc = xc.reshape(N_DIST * 8, 128)
      xs.append(jnp.concatenate([xc, xn, xpad], axis=0).astype(BF16))
      qs.append(qg if d == 1 else jnp.where(qrow_t % d == rho, qg, 0.0))
    xcat = xs[0] if nres == 1 else jnp.concatenate(xs, axis=1)
    qcat = (qs[0] if nres == 1 else jnp.concatenate(qs, axis=1)).astype(BF16)
    s = _dot_nt(xcat, qcat) * scale + bias_ref[g]
    m = jnp.max(s, axis=0, keepdims=True)
    p = jnp.exp(s - m)
    l = jnp.sum(p, axis=0, keepdims=True)
    pn = p / l
    pvt = pltpu.roll(pn, 4, 0).T.astype(BF16)
    full = _dot(pvt, xcat)
    o = full[:, :128]
    for rho in range(1, nres):
      o = jnp.where(qrow_t % d == rho, full[:, rho * 128:(rho + 1) * 128], o)
    o_g.append(o)
    lse_g.append(jnp.broadcast_to(m + jnp.log(l), (128, 128)).T)
  mx = jnp.maximum(jnp.maximum(lse_g[0], lse_g[1]), lse_g[2])
  es = [jnp.exp(v - mx) for v in lse_g]
  den = es[0] + es[1] + es[2]
  o_ref[e] = (es[0] * o_g[0] + es[1] * o_g[1] + es[2] * o_g[2]) / den


def _attn_sample(q, caches, news, slopes, tq, after):
  batch = q.shape[0]
  nb = SAMPLE_NB
  in_specs = [pl.BlockSpec(memory_space=pltpu.SMEM),
              pl.BlockSpec((nb, 3, 128, 128), lambda b: (b, 0, 0, 0))]
  args = [slopes, q]
  for g, (w, d) in enumerate(A_GROUPS):
    if g == SC_ROLLED:
      in_specs.append(pl.BlockSpec((nb, N_DIST, min(d, tq), 8, 128), lambda b: (b, 0, 0, 0, 0)))
      args.append(caches[g].reshape(batch, N_DIST, d, 8, 128))
    else:
      in_specs.append(pl.BlockSpec((nb, w, 8, 128), lambda b: (b, 0, 0, 0)))
      args.append(caches[g])
  for g in range(3):
    in_specs.append(pl.BlockSpec((nb, tq, 8, 128), lambda b: (b, 0, 0, 0)))
    args.append(news[g])
  rolled_in = len(args)
  for r in _sc_roll(caches, tq, after):
    in_specs.append(pl.BlockSpec(memory_space=pl.ANY))
    args.append(r)
  out_shape = [jax.ShapeDtypeStruct((batch, 128, 128), F32)]
  out_specs = [pl.BlockSpec((nb, 128, 128), lambda b: (b, 0, 0))]
  for g, (w, _) in enumerate(A_GROUPS):
    out_shape.append(jax.ShapeDtypeStruct((batch, w, 8, 128), F32))
    out_specs.append(pl.BlockSpec(memory_space=pl.ANY))
  return pl.pallas_call(
      functools.partial(_attn_sample_kernel, tq=tq),
      grid=(batch // nb,),
      in_specs=in_specs,
      out_specs=out_specs,
      out_shape=out_shape,
      input_output_aliases={rolled_in + g: 1 + g for g in range(len(A_GROUPS))},
      scratch_shapes=[pltpu.SemaphoreType.DMA((nb * len(A_GROUPS),)),
                      pltpu.VMEM((len(A_GROUPS), -(-(N_DIST + tq) * 8 // 128) * 128, 128), F32)],
      compiler_params=_params(("arbitrary",)),
      name="attn_sample",
  )(*args)


def _retention_kernel(x_ref, wqk_ref, wv_ref, cos_ref, sin_ref, dmat_ref, qdec_ref, kdec_ref,
                      cdec_ref, s0_ref, gn_ref, o_ref, sout_ref, state, qk_ref, v_ref, *, chunk):
  j = pl.program_id(1)
  nb, tc, _ = x_ref.shape

  @pl.when(j == 0)
  def _():
    state[...] = s0_ref[...]

  xb = x_ref[...].reshape(nb * tc, D_MODEL).astype(BF16)
  qk_ref[...] = _dot(xb, wqk_ref[...])
  v_ref[...] = _dot(xb, wv_ref[...]).astype(BF16)
  kscale = R_KEY_DIM ** -0.5
  for bi in range(nb):
    for ci in range(tc // chunk):
      rows = pl.ds(bi * tc + ci * chunk, chunk)
      seq_rows = pl.ds(ci * chunk, chunk)
      cosv = cos_ref[seq_rows, :]
      sinv = sin_ref[seq_rows, :]
      for h in range(R_HEADS):
        qh = qk_ref[rows, h * 128:(h + 1) * 128]
        kh = qk_ref[rows, R_QK + h * 128:R_QK + (h + 1) * 128]
        qrot = qh * cosv + pltpu.roll(qh, 64, 1) * sinv
        krot = (kh * cosv + pltpu.roll(kh, 64, 1) * sinv) * kscale
        qb = qrot.astype(BF16)
        vh = v_ref[rows, h * 256:(h + 1) * 256]
        att = _dot_nt(qb, krot.astype(BF16)) * dmat_ref[h]
        sh = state[bi, h]
        o = _dot(att.astype(BF16), vh) + _dot(qb, sh.astype(BF16)) * qdec_ref[h]
        state[bi, h] = cdec_ref[h] * sh + _dot_tn((krot * kdec_ref[h]).astype(BF16), vh)
        mu = jnp.mean(o, axis=-1, keepdims=True)
        var = jnp.mean(jnp.square(o - mu), axis=-1, keepdims=True)
        on = (o - mu) * lax.rsqrt(var + EPS) * gn_ref[:, h * 256:(h + 1) * 256]
        o_ref[bi, seq_rows, h * 256:(h + 1) * 256] = on

  @pl.when(j == pl.num_programs(1) - 1)
  def _():
    sout_ref[...] = state[...]


def _retention(x, w16, cos, sin, tables, s0, gn, rows_per_step, chunk, nb):
  batch, t, _ = x.shape
  dmat, qdec, kdec, cdec = tables
  tc = rows_per_step
  const3 = lambda b, j: (0, 0, 0)
  wslab = lambda off: pl.BlockSpec((D_MODEL, R_V), lambda b, j: (0, off // R_V),
                                   pipeline_mode=pl.Buffered(1))
  state_spec = pl.BlockSpec((nb, R_HEADS, R_KEY_DIM, R_VAL_DIM), lambda b, j: (b, 0, 0, 0))
  return pl.pallas_call(
      functools.partial(_retention_kernel, chunk=chunk),
      grid=(batch // nb, t // tc),
      in_specs=[pl.BlockSpec((nb, tc, D_MODEL), lambda b, j: (b, j, 0)),
                wslab(_OFF[4]), wslab(_OFF[6]),
                pl.BlockSpec((tc, 128), lambda b, j: (j, 0)),
                pl.BlockSpec((tc, 128), lambda b, j: (j, 0)),
                pl.BlockSpec(dmat.shape, const3),
                pl.BlockSpec(qdec.shape, const3),
                pl.BlockSpec(kdec.shape, const3),
                pl.BlockSpec(cdec.shape, const3),
                state_spec,
                pl.BlockSpec((1, R_V), lambda b, j: (0, 0))],
      out_specs=[pl.BlockSpec((nb, tc, R_V), lambda b, j: (b, j, 0)), state_spec],
      out_shape=[jax.ShapeDtypeStruct((batch, t, R_V), F32),
                 jax.ShapeDtypeStruct((batch, R_HEADS, R_KEY_DIM, R_VAL_DIM), F32)],
      scratch_shapes=[pltpu.VMEM((nb, R_HEADS, R_KEY_DIM, R_VAL_DIM), F32),
                      pltpu.VMEM((nb * tc, 2 * R_QK), F32), pltpu.VMEM((nb * tc, R_V), BF16)],
      compiler_params=_params(("parallel", "arbitrary")),
      name="retention",
  )(x, w16, w16, cos, sin, dmat, qdec, kdec, cdec, s0, gn)


def _retention_tables(c_true, c_pad):
  lg = jnp.log1p(-jnp.exp2(-5.0 - jnp.arange(R_HEADS, dtype=F32)))
  i = jnp.arange(c_pad, dtype=F32)
  live = i < c_true
  diff = i[:, None] - i[None, :]
  causal = (diff >= 0) & live[:, None] & live[None, :]
  dmat = jnp.where(causal[None], jnp.exp(jnp.where(causal, diff, 0.0)[None] * lg[:, None, None]), 0.0)
  qdec = jnp.where(live[None, :], jnp.exp((i[None, :] + 1.0) * lg[:, None]), 0.0)
  kdec = jnp.where(live[None, :], jnp.exp((c_true - 1.0 - i)[None, :] * lg[:, None]), 0.0)
  cdec = jnp.exp(c_true * lg)
  qdec = jnp.broadcast_to(qdec[:, :, None], (R_HEADS, c_pad, R_VAL_DIM))
  kdec = jnp.broadcast_to(kdec[:, :, None], (R_HEADS, c_pad, R_KEY_DIM))
  cdec = jnp.broadcast_to(cdec[:, None, None], (R_HEADS, R_KEY_DIM, R_VAL_DIM))
  return dmat, qdec, kdec, cdec


def _rope_tables(pos):
  half = R_KEY_DIM // 2
  inv = 1.0 / (ROPE_BASE ** jnp.linspace(0.0, 1.0, half, dtype=F32))
  ang = pos.astype(F32)[:, None] * inv[None, :]
  cos = jnp.cos(ang)
  sin = jnp.sin(ang)
  return jnp.concatenate([cos, cos], axis=-1), jnp.concatenate([-sin, sin], axis=-1)


def _output_kernel(x_ref, oa_ref, or_ref, wza_ref, wzr_ref, wga_ref, wgb_ref, wpa_ref, wpb_ref,
                   wo_ref, lng_ref, lnb_ref, y_ref):
  tm = x_ref.shape[0]
  nsub = 2 if tm % 32 == 0 else 1
  sub = tm // nsub
  for si in range(nsub):
    rows = pl.ds(si * sub, sub)
    x = x_ref[rows, :]
    xb = x.astype(BF16)
    za = _dot(xb, wza_ref[...])
    ya = _dot((jax.nn.silu(za) * oa_ref[rows, :]).astype(BF16), wpa_ref[...])
    zr = _dot(xb, wzr_ref[...])
    yb = _dot((jax.nn.silu(zr) * or_ref[rows, :]).astype(BF16), wpb_ref[...])
    ga = _dot(xb, wga_ref[...])
    gb = _dot(xb, wgb_ref[...])
    mix = jax.nn.sigmoid(ga) * ya + jax.nn.sigmoid(gb) * yb
    out = _dot(mix.astype(BF16), wo_ref[...])
    zz = DEEPNORM_ALPHA * x + out
    zm = jnp.mean(zz, axis=-1, keepdims=True)
    zv = jnp.mean(jnp.square(zz - zm), axis=-1, keepdims=True)
    y_ref[rows, :] = (zz - zm) * lax.rsqrt(zv + EPS) * lng_ref[...] + lnb_ref[...]


def _output(x, oa, orr, w16, wpa, wpb, wo, lng, lnb, tm):
  m = x.shape[0]
  row = lambda w: pl.BlockSpec((tm, w), lambda i: (i, 0))
  full = lambda a: pl.BlockSpec(a.shape, lambda i: (0, 0), pipeline_mode=pl.Buffered(1))
  wcol = lambda off, width: pl.BlockSpec((D_MODEL, width), lambda i: (0, off // width),
                                         pipeline_mode=pl.Buffered(1))
  in_specs = [row(D_MODEL), row(A_OUT), row(R_V),
              wcol(_OFF[3], A_OUT), wcol(_OFF[7], R_V), wcol(_OFF[8], D_MODEL),
              wcol(_OFF[9], D_MODEL),
              full(wpa), full(wpb), full(wo), full(lng), full(lnb)]
  args = [x, oa, orr, w16, w16, w16, w16, wpa, wpb, wo, lng, lnb]
  return pl.pallas_call(
      _output_kernel,
      grid=(m // tm,),
      in_specs=in_specs,
      out_specs=row(D_MODEL),
      out_shape=jax.ShapeDtypeStruct((m, D_MODEL), F32),
      compiler_params=_params(("parallel",)),
      name="gates_output",
  )(*args)


def _alibi_slopes():
  n = len(A_GROUPS) * A_HEADS
  return jnp.exp2(-8.0 * jnp.arange(1, n + 1, dtype=F32) / n).reshape(len(A_GROUPS), A_HEADS)


def kernel(x_prompt, x_sample, cache_kv_w128, cache_kv_w512, cache_kv_w2048, state_ret,
           w_in, w_pa, w_pb, w_o, gn_g, ln_g, ln_b):
  bp, tp, _ = x_prompt.shape
  bs, ts, _ = x_sample.shape
  slopes = _alibi_slopes()
  w16 = w_in.astype(BF16)
  wpa = w_pa.astype(BF16)
  wpb = w_pb.astype(BF16)
  wo = w_o.astype(BF16)
  gn = gn_g.reshape(1, R_V)
  lng = ln_g.reshape(1, D_MODEL)
  lnb = ln_b.reshape(1, D_MODEL)

  xp2 = x_prompt.reshape(bp * tp, D_MODEL)
  qkvs, kv_p, tails = [], [], []
  for g, (w, _) in enumerate(A_GROUPS):
    qkv, tail = _proj_group(xp2, w16, g, bp, tp)
    qkvs.append(qkv)
    tails.append(tail)
    kv_p.append(tail.reshape(bp, min(w, tp), 2, A_HEADS, A_HEAD_DIM))
  oa_p = _attn_prompt(qkvs, slopes, bp, tp).reshape(bp * tp, A_OUT)

  cos_p, sin_p = _rope_tables(jnp.arange(tp, dtype=jnp.int32))
  or_p, s_p = _retention(
      x_prompt, w16, cos_p, sin_p, _retention_tables(R_CHUNK, R_CHUNK),
      jnp.zeros((bp, R_HEADS, R_KEY_DIM, R_VAL_DIM), F32), gn, RET_ROWS, R_CHUNK, 1)
  y_p = _output(xp2, oa_p, or_p.reshape(bp * tp, R_V), w16, wpa, wpb, wo, lng, lnb, OUT_ROWS)
  xs2 = x_sample.reshape(bs * ts, D_MODEL)
  hs = _matmul(xs2, w16, 0, _OFF[3], F32, bs * ts, A_GW)
  caches = (cache_kv_w128, cache_kv_w512, cache_kv_w2048)
  q_s, news = [], []
  for g in range(3):
    col = lambda part: hs[:, _OFF[part] + g * A_GW:_OFF[part] + (g + 1) * A_GW]
    qg = col(0).reshape(bs, ts * A_HEADS, A_HEAD_DIM)
    q_s.append(jnp.pad(qg, ((0, 0), (0, 128 - ts * A_HEADS), (0, 0))))
    news.append(jnp.concatenate([col(1), col(2)], axis=1).reshape(bs, ts, 8, 128))
  q_s = jnp.stack(q_s, axis=1)
  caches8 = [c.reshape(bs, c.shape[1], 8, 128) for c in caches]
  cpad = SAMPLE_CHUNK_PAD
  xs_pad = jnp.pad(x_sample, ((0, 0), (0, cpad - ts), (0, 0)))
  cos_s, sin_s = _rope_tables(PAST_LEN + jnp.arange(ts, dtype=jnp.int32))
  padtab = lambda a: jnp.pad(a, ((0, cpad - ts), (0, 0)))
  after = [t[0, :8] for t in tails]
  oa_s, u0, u1, u2 = _attn_sample(q_s, caches8, news, slopes, ts, after)
  oa_s = oa_s[:, :ts * A_HEADS].reshape(bs * ts, A_OUT)
  kv_s = [u.reshape(bs, u.shape[1], 2, A_HEADS, A_HEAD_DIM) for u in (u0, u1, u2)]
  or_s, s_s = _retention(
      xs_pad, w16, padtab(cos_s), padtab(sin_s),
      _retention_tables(ts, cpad), state_ret.astype(F32), gn, cpad, cpad, SAMPLE_RET_NB)
  or_s = or_s[:, :ts].reshape(bs * ts, R_V)
  y_s = _output(xs2, oa_s, or_s, w16, wpa, wpb, wo, lng, lnb, SAMPLE_OUT_ROWS)

  return (y_p.reshape(bp, tp, D_MODEL), y_s.reshape(bs, ts, D_MODEL),
          kv_p[0], kv_p[1], kv_p[2], s_p,
          kv_s[0], kv_s[1], kv_s[2], s_s)
```

```python
import functools

import numpy as np
import jax
import jax.numpy as jnp
from jax import lax
from jax.experimental import pallas as pl
from jax.experimental.pallas import tpu as pltpu
from jax.experimental.pallas import tpu_sc as plsc

D_MODEL = 1024
PAST_LEN = 16384
A_GROUPS = ((128, 1), (512, 4), (2048, 16))
A_HEADS = 4
A_HEAD_DIM = 128
A_GW = A_HEADS * A_HEAD_DIM
A_QKV = len(A_GROUPS) * A_GW
A_OUT = A_GW
N_DIST = 128
R_HEADS = 4
R_KEY_DIM = 128
R_VAL_DIM = 256
R_QK = R_HEADS * R_KEY_DIM
R_V = R_HEADS * R_VAL_DIM
R_CHUNK = 256
ROPE_BASE = 10000.0
EPS = 1e-5
NEG_INF = -1e30
DEEPNORM_ALPHA = 2.0 ** 0.25
_SPLIT = (A_QKV, A_QKV, A_QKV, A_OUT, R_QK, R_QK, R_V, R_V, D_MODEL, D_MODEL)
_OFF = tuple(int(v) for v in np.concatenate([[0], np.cumsum(_SPLIT)]))

BF16 = jnp.bfloat16
F32 = jnp.float32
VMEM_LIMIT = 56 * 1024 * 1024
RET_ROWS = 1024
OUT_ROWS = 512
SAMPLE_OUT_ROWS = 256
SAMPLE_CHUNK_PAD = 16
SAMPLE_RET_NB = 8

_NT = (((1,), (1,)), ((), ()))
_TN = (((0,), (0,)), ((), ()))


def _dot(a, b):
  return jnp.dot(a, b, preferred_element_type=F32)


def _dot_nt(a, b):
  return lax.dot_general(a, b, _NT, preferred_element_type=F32)


def _dot_tn(a, b):
  return lax.dot_general(a, b, _TN, preferred_element_type=F32)


def _params(sem):
  return pltpu.CompilerParams(dimension_semantics=sem, vmem_limit_bytes=VMEM_LIMIT)


def _mm_kernel(x_ref, w_ref, o_ref):
  o_ref[...] = _dot(x_ref[...].astype(BF16), w_ref[...].astype(BF16)).astype(o_ref.dtype)


def _cast_weights_once(first_step, w_refs, wbf_refs):
  @pl.when(first_step)
  def _():
    for w_ref, wbf_ref in zip(w_refs, wbf_refs):
      wbf_ref[...] = w_ref[...].astype(BF16)


def _matmul(x, w, col0, ncols, out_dtype, tm, tn):
  m, k = x.shape
  c0 = col0 // tn
  return pl.pallas_call(
      _mm_kernel,
      grid=(m // tm, ncols // tn),
      in_specs=[pl.BlockSpec((tm, k), lambda i, j: (i, 0)),
                pl.BlockSpec((k, tn), lambda i, j: (0, c0 + j))],
      out_specs=pl.BlockSpec((tm, tn), lambda i, j: (i, j)),
      out_shape=jax.ShapeDtypeStruct((m, ncols), out_dtype),
      compiler_params=_params(("parallel", "arbitrary")),
      name="proj_matmul",
  )(x, w)


PROJ_TM = 1024


def _proj_group_kernel(x_ref, wq_ref, wk_ref, wv_ref, o_ref, tail_ref, wbf_ref, *scratch,
                       d, first_tail, tail_rows):
  i = pl.program_id(1)
  w_bf = [wbf_ref.at[part] for part in range(3)]
  _cast_weights_once((pl.program_id(0) == 0) & (i == 0), (wq_ref, wk_ref, wv_ref), w_bf)
  tm = x_ref.shape[0]
  n = tm // d
  x = x_ref[...]
  xb = x.astype(BF16)
  if d == 1:
    perm = xb
  else:
    xs_ref, xp_ref = scratch
    for c in range(D_MODEL // 128):
      xs_ref[c] = x[:, c * 128:(c + 1) * 128]
    for r in range(d):
      for c in range(D_MODEL // 128):
        xp_ref[r * n:(r + 1) * n, c * 128:(c + 1) * 128] = (
            xs_ref[c, pl.ds(r, n, stride=d), :].astype(BF16))
    perm = xp_ref[...]
  nat = []
  for part in range(3):
    res = _dot(perm, w_bf[part][...])
    nat.append(res)
    for r in range(d):
      for h in range(A_HEADS):
        o_ref[r, part * A_HEADS + h] = res[r * n:(r + 1) * n, h * 128:(h + 1) * 128].astype(BF16)

  @pl.when(i >= first_tail)
  def _():
    if d == 1:
      kv = nat[1:]
    else:
      kv = [_dot(xb, w_bf[1][...]), _dot(xb, w_bf[2][...])]
    for c in range(2 * A_HEADS):
      src = kv[c // A_HEADS][tm - tail_rows:, (c % A_HEADS) * 128:(c % A_HEADS + 1) * 128]
      tail_ref[pl.ds(c, tail_rows, stride=2 * A_HEADS), :] = src


def _proj_group(x2, w16, g, batch, seq):
  w, d = A_GROUPS[g]
  w = min(w, seq)
  tm = PROJ_TM
  nblk = seq // tm
  tail_rows = min(tm, w)
  first_tail = nblk - max(w // tm, 1)
  ls = seq // d
  wspec = lambda part: pl.BlockSpec((D_MODEL, A_GW), lambda b, i: (0, 3 * part + g),
                                    pipeline_mode=pl.Buffered(1))
  scratch = [pltpu.VMEM((3, D_MODEL, A_GW), BF16)]
  if d > 1:
    scratch += [pltpu.VMEM((D_MODEL // 128, tm, 128), F32), pltpu.VMEM((tm, D_MODEL), BF16)]
  return pl.pallas_call(
      functools.partial(_proj_group_kernel, d=d, first_tail=first_tail, tail_rows=tail_rows),
      grid=(batch, nblk),
      in_specs=[pl.BlockSpec((tm, D_MODEL), lambda b, i: (b * nblk + i, 0)),
                wspec(0), wspec(1), wspec(2)],
      out_specs=[pl.BlockSpec((None, d, 3 * A_HEADS, tm // d, 128), lambda b, i: (b, 0, 0, i, 0)),
                 pl.BlockSpec((None, tail_rows * 8, 128),
                              lambda b, i: (b, jnp.maximum(i - first_tail, 0), 0))],
      out_shape=[jax.ShapeDtypeStruct((batch, d, 3 * A_HEADS, ls, 128), BF16),
                 jax.ShapeDtypeStruct((batch, w * 8, 128), F32)],
      scratch_shapes=scratch,
      compiler_params=_params(("arbitrary", "arbitrary")),
      name="proj_group",
  )(x2, w16, w16, w16)


ATT_BQ = 128
ATT_SET = 4
LOG2E = 1.4426950408889634


def _attn_prompt_kernel(slopes_ref, *refs, seq):
  qkv = refs[:9]
  o_ref = refs[9]
  oacc = refs[10:13]
  lacc = refs[13:16]
  bias_ref, s_buf, p_buf, l_buf = refs[16:20]
  h = pl.program_id(0)
  scale = A_HEAD_DIM ** -0.5

  @pl.when(pl.program_id(1) == 0)
  def _():
    for g, (_, d) in enumerate(A_GROUPS):
      ls = seq // d
      bq = min(ATT_BQ, ls)
      klen = min(bq + N_DIST, ls)
      slope = slopes_ref[g, h]
      qi = lax.broadcasted_iota(jnp.int32, (bq, klen), 0)
      kj = lax.broadcasted_iota(jnp.int32, (bq, klen), 1)
      for which, off in enumerate((0, N_DIST)):
        delta = qi - kj + off
        valid = (delta >= 0) & (delta <= N_DIST)
        dist = (delta * d).astype(F32)
        bias_ref[2 * g + which, :bq, :klen] = jnp.where(valid, -(slope * LOG2E) * dist, NEG_INF)

  for g, (_, d) in enumerate(A_GROUPS):
    q_ref, k_ref, v_ref = qkv[3 * g:3 * g + 3]
    ls = seq // d
    bq = min(ATT_BQ, ls)
    nqb = ls // bq
    klen = min(bq + N_DIST, ls)

    def coords(idx, nqb=nqb, bq=bq):
      r = idx // nqb
      qb = idx % nqb
      q0 = pl.multiple_of(qb * bq, bq)
      k0 = pl.multiple_of(jnp.maximum(qb * bq - N_DIST, 0), N_DIST)
      return r, qb, q0, k0

    def put_rows(ref, idx, val, d=d, bq=bq):
      r, _, q0, _ = coords(idx)
      if d == 1:
        ref[pl.ds(q0, bq), :] = val
      else:
        ref[pl.ds(r + q0 * d, bq, stride=d), :] = val

    def scores(idx, slot, bq=bq, klen=klen, q_ref=q_ref, k_ref=k_ref):
      r, _, q0, k0 = coords(idx)
      s_buf[slot, :bq, :klen] = _dot_nt(q_ref[r, pl.ds(q0, bq), :], k_ref[r, pl.ds(k0, klen), :])

    def softmax(idx, slot, g=g, bq=bq, klen=klen):
      _, qb, _, _ = coords(idx)
      s = (s_buf[slot, :bq, :klen] * (scale * LOG2E)
           + bias_ref[2 * g + jnp.minimum(qb, 1), :bq, :klen])
      m = jnp.max(s, axis=-1, keepdims=True)
      p = jnp.exp2(s - m)
      l = jnp.sum(p, axis=-1, keepdims=True)
      p_buf[slot, :bq, :klen] = p.astype(BF16)
      l_buf[slot, :bq, :] = jnp.broadcast_to(l, (bq, 128))
      put_rows(lacc[g], idx, jnp.broadcast_to(m + jnp.log2(l), (bq, 128)))

    def values(idx, slot, g=g, bq=bq, klen=klen, v_ref=v_ref):
      r, _, _, k0 = coords(idx)
      o = _dot(p_buf[slot, :bq, :klen], v_ref[r, pl.ds(k0, klen), :]) / l_buf[slot, :bq, :]
      put_rows(oacc[g], idx, o)

    nblk = d * nqb
    ns = ATT_SET

    def run(stage, first, half):
      for t in range(ns):
        stage(first + t, half * ns + t)

    run(scores, 0, 0)
    run(softmax, 0, 0)
    run(scores, ns, 1)

    def steady(j, carry):
      i = 2 * ns * (j + 1)
      run(values, i - 2 * ns, 0)
      run(softmax, i - ns, 1)
      run(scores, i, 0)
      run(values, i - ns, 1)
      run(softmax, i, 0)
      run(scores, i + ns, 1)
      return carry

    lax.fori_loop(0, (nblk - 2 * ns) // (2 * ns), steady, 0)
    run(values, nblk - 2 * ns, 0)
    run(softmax, nblk - ns, 1)
    run(values, nblk - ns, 1)

  mrows = 512

  def merge(i, carry):
    r0 = pl.multiple_of(i * mrows, mrows)
    l0 = lacc[0][pl.ds(r0, mrows), :]
    l1 = lacc[1][pl.ds(r0, mrows), :]
    l2 = lacc[2][pl.ds(r0, mrows), :]
    mx = jnp.maximum(jnp.maximum(l0, l1), l2)
    e0 = jnp.exp2(l0 - mx)
    e1 = jnp.exp2(l1 - mx)
    e2 = jnp.exp2(l2 - mx)
    den = e0 + e1 + e2
    acc = (e0 * oacc[0][pl.ds(r0, mrows), :] + e1 * oacc[1][pl.ds(r0, mrows), :]
           + e2 * oacc[2][pl.ds(r0, mrows), :])
    o_ref[pl.ds(r0, mrows), :] = acc / den
    return carry

  lax.fori_loop(0, seq // mrows, merge, 0)


def _attn_prompt(qkvs, slopes, batch, seq):
  in_specs = [pl.BlockSpec(memory_space=pltpu.SMEM)]
  args = [slopes]
  for g, (_, d) in enumerate(A_GROUPS):
    ls = seq // d
    for part in range(3):
      in_specs.append(pl.BlockSpec(
          (None, d, None, ls, 128),
          lambda h, b, part=part: (b, 0, part * A_HEADS + h, 0, 0)))
      args.append(qkvs[g])
  scratch = ([pltpu.VMEM((seq, 128), F32) for _ in range(6)]
             + [pltpu.VMEM((6, ATT_BQ, ATT_BQ + N_DIST), F32),
                pltpu.VMEM((2 * ATT_SET, ATT_BQ, ATT_BQ + N_DIST), F32),
                pltpu.VMEM((2 * ATT_SET, ATT_BQ, ATT_BQ + N_DIST), BF16),
                pltpu.VMEM((2 * ATT_SET, ATT_BQ, 128), F32)])
  return pl.pallas_call(
      functools.partial(_attn_prompt_kernel, seq=seq),
      grid=(A_HEADS, batch),
      in_specs=in_specs,
      out_specs=pl.BlockSpec((None, seq, 128), lambda h, b: (b, 0, h)),
      out_shape=jax.ShapeDtypeStruct((batch, seq, A_OUT), F32),
      scratch_shapes=scratch,
      compiler_params=_params(("parallel", "arbitrary")),
      name="attn_prompt",
  )(*args)


SC_ROLLED = 2
SC_CHUNK_ROWS = 24
SAMPLE_NB = 1


def _sc_roll(caches, shift, after):
  batch, _, slots, lanes = caches[0].shape
  chunk = SC_CHUNK_ROWS * slots
  rows = [c.shape[1] * slots for c in caches]
  nchunks = [(c.shape[1] - shift) // SC_CHUNK_ROWS for c in caches]
  assert all(n * SC_CHUNK_ROWS == c.shape[1] - shift for n, c in zip(nchunks, caches))
  mesh = plsc.VectorSubcoreMesh(core_axis_name="c", subcore_axis_name="s")
  nsub = mesh.num_subcores
  assert mesh.num_cores * nsub == batch, "one vector subcore per batch entry"
  n = len(caches)

  def body(*refs):
    x_hbm = refs[:n]
    o_hbm = refs[n + len(after):2 * n + len(after)]
    buf = refs[-1]
    entry = lax.axis_index("c") * nsub + lax.axis_index("s")
    for g in range(n):
      base = entry * rows[g]

      @pl.loop(0, nchunks[g])
      def _(i, g=g, base=base):
        pltpu.sync_copy(x_hbm[g].at[pl.ds(base + shift * slots + i * chunk, chunk)], buf)
        pltpu.sync_copy(buf, o_hbm[g].at[pl.ds(base + i * chunk, chunk)])

  flats = [c.reshape(batch * r, lanes) for c, r in zip(caches, rows)]
  outs = pl.kernel(body, out_type=[jax.ShapeDtypeStruct(f.shape, f.dtype) for f in flats],
                   mesh=mesh,
                   scratch_types=[pltpu.VMEM((chunk, lanes), caches[0].dtype)])(*flats, *after)
  return [o.reshape(c.shape) for o, c in zip(outs, caches)]


def _attn_sample_kernel(slopes_ref, q_ref, c0_ref, c1_ref, c2_ref, n0_ref, n1_ref, n2_ref,
                        r0_ref, r1_ref, r2_ref, o_ref, u0_ref, u1_ref, u2_ref, sems, bias_ref,
                        *, tq):
  b = pl.program_id(0)
  cache_refs = (c0_ref, c1_ref, c2_ref)
  new_refs = (n0_ref, n1_ref, n2_ref)
  out_refs = (u0_ref, u1_ref, u2_ref)
  del r0_ref, r1_ref, r2_ref

  nb = q_ref.shape[0]
  copies = []
  for e in range(nb):
    for g, (w, _) in enumerate(A_GROUPS):
      tail = pltpu.make_async_copy(new_refs[g].at[e], out_refs[g].at[b * nb + e, pl.ds(w - tq, tq)],
                                   sems.at[e * len(A_GROUPS) + g])
      tail.start()
      copies.append(tail)

  scale = A_HEAD_DIM ** -0.5
  nreal = (N_DIST + tq) * A_HEADS
  nrow = -(-nreal // 128) * 128

  @pl.when(b == 0)
  def _():
    ri = lax.broadcasted_iota(jnp.int32, (nrow, 128), 0)
    cj = lax.broadcasted_iota(jnp.int32, (nrow, 128), 1)
    slot = ri % A_HEADS
    key = ri // A_HEADS
    tcol = cj // A_HEADS
    hcol = cj % A_HEADS
    col_ok = cj < tq * A_HEADS
    for g, (_, d) in enumerate(A_GROUPS):
      qt = tcol // d
      nidx = key - N_DIST
      in_cache = key < N_DIST
      valid_c = in_cache & (key >= qt)
      dist_c = d * (N_DIST + qt - key)
      valid_n = (~in_cache) & (nidx <= tcol) & ((tcol - nidx) % d == 0)
      dist_n = tcol - nidx
      valid = (valid_c | valid_n) & (slot == hcol) & col_ok & (ri < nreal)
      dist = jnp.where(in_cache, dist_c, dist_n).astype(F32)
      slope = jnp.zeros((nrow, 128), F32)
      for hh in range(A_HEADS):
        slope = jnp.where(hcol == hh, slopes_ref[g, hh], slope)
      bias_ref[g] = jnp.where(valid, -slope * dist, NEG_INF)

  qrow_t = lax.broadcasted_iota(jnp.int32, (128, 1), 0) // A_HEADS
  xpad = jnp.zeros((nrow - nreal, 128), F32)
  for e in range(nb):
    _attend_entry(e, q_ref, cache_refs, new_refs, bias_ref, o_ref, qrow_t, xpad, scale, tq)

  for c in copies:
    c.wait()


def _attend_entry(e, q_ref, cache_refs, new_refs, bias_ref, o_ref, qrow_t, xpad, scale, tq):
  lse_g = []
  o_g = []
  nh = A_HEADS
  for g, (w, d) in enumerate(A_GROUPS):
    qg = q_ref[e, g]
    nres = min(d, tq)
    xk, xv, qs = [], [], []
    for rho in range(nres):
      parts = []
      for lo in (0, nh):
        if g == SC_ROLLED:
          xc = cache_refs[g][e, :, rho, lo:lo + nh, :]
        elif d == 1:
          xc = cache_refs[g][e, :, lo:lo + nh, :]
        else:
          xc = cache_refs[g][e, pl.ds(rho, N_DIST, stride=d), lo:lo + nh, :]
        xn = new_refs[g][e, :, lo:lo + nh, :]
        parts.append(jnp.concatenate(
            [xc.reshape(N_DIST * nh, 128), xn.reshape(tq * nh, 128), xpad], axis=0).astype(BF16))
      xk.append(parts[0])
      xv.append(parts[1])
      qs.append(qg if d == 1 else jnp.where(qrow_t % d == rho, qg, 0.0))
    cat = lambda xs: xs[0] if nres == 1 else jnp.concatenate(xs, axis=1)
    s = _dot_nt(cat(xk), cat(qs).astype(BF16)) * scale + bias_ref[g]
    m = jnp.max(s, axis=0, keepdims=True)
    p = jnp.exp(s - m)
    l = jnp.sum(p, axis=0, keepdims=True)
    pvt = (p / l).T.astype(BF16)
    full = _dot(pvt, cat(xv))
    o = full[:, :128]
    for rho in range(1, nres):
      o = jnp.where(qrow_t % d == rho, full[:, rho * 128:(rho + 1) * 128], o)
    o_g.append(o)
    lse_g.append(jnp.broadcast_to(m + jnp.log(l), (128, 128)).T)
  mx = jnp.maximum(jnp.maximum(lse_g[0], lse_g[1]), lse_g[2])
  es = [jnp.exp(v - mx) for v in lse_g]
  den = es[0] + es[1] + es[2]
  o_ref[e] = (es[0] * o_g[0] + es[1] * o_g[1] + es[2] * o_g[2]) / den


def _attn_sample(q, caches, news, slopes, tq, after):
  batch = q.shape[0]
  nb = SAMPLE_NB
  in_specs = [pl.BlockSpec(memory_space=pltpu.SMEM),
              pl.BlockSpec((nb, 3, 128, 128), lambda b: (b, 0, 0, 0))]
  args = [slopes, q]
  for g, (w, d) in enumerate(A_GROUPS):
    if g == SC_ROLLED:
      in_specs.append(pl.BlockSpec((nb, N_DIST, min(d, tq), 8, 128), lambda b: (b, 0, 0, 0, 0)))
      args.append(caches[g].reshape(batch, N_DIST, d, 8, 128))
    else:
      in_specs.append(pl.BlockSpec((nb, w, 8, 128), lambda b: (b, 0, 0, 0)))
      args.append(caches[g])
  for g in range(3):
    in_specs.append(pl.BlockSpec((nb, tq, 8, 128), lambda b: (b, 0, 0, 0)))
    args.append(news[g])
  rolled_in = len(args)
  for r in _sc_roll(caches, tq, after):
    in_specs.append(pl.BlockSpec(memory_space=pl.ANY))
    args.append(r)
  out_shape = [jax.ShapeDtypeStruct((batch, 128, 128), F32)]
  out_specs = [pl.BlockSpec((nb, 128, 128), lambda b: (b, 0, 0))]
  for g, (w, _) in enumerate(A_GROUPS):
    out_shape.append(jax.ShapeDtypeStruct((batch, w, 8, 128), F32))
    out_specs.append(pl.BlockSpec(memory_space=pl.ANY))
  return pl.pallas_call(
      functools.partial(_attn_sample_kernel, tq=tq),
      grid=(batch // nb,),
      in_specs=in_specs,
      out_specs=out_specs,
      out_shape=out_shape,
      input_output_aliases={rolled_in + g: 1 + g for g in range(len(A_GROUPS))},
      scratch_shapes=[pltpu.SemaphoreType.DMA((nb * len(A_GROUPS),)),
                      pltpu.VMEM((len(A_GROUPS), -(-(N_DIST + tq) * A_HEADS // 128) * 128, 128),
                                 F32)],
      compiler_params=_params(("arbitrary",)),
      name="attn_sample",
  )(*args)


def _retention_kernel(x_ref, wqk_ref, wv_ref, cos_ref, sin_ref, dmat_ref, qdec_ref, kdec_ref,
                      cdec_ref, s0_ref, gn_ref, o_ref, sout_ref, state, qk_ref, v_ref,
                      wqk_bf, wv_bf, *, chunk):
  j = pl.program_id(1)
  nb, tc, _ = x_ref.shape
  _cast_weights_once((pl.program_id(0) == 0) & (j == 0), (wqk_ref, wv_ref), (wqk_bf, wv_bf))

  @pl.when(j == 0)
  def _():
    state[...] = s0_ref[...]

  xb = x_ref[...].reshape(nb * tc, D_MODEL).astype(BF16)
  qk_ref[...] = _dot(xb, wqk_bf[...])
  v_ref[...] = _dot(xb, wv_bf[...]).astype(BF16)
  kscale = R_KEY_DIM ** -0.5
  for bi in range(nb):
    for ci in range(tc // chunk):
      rows = pl.ds(bi * tc + ci * chunk, chunk)
      seq_rows = pl.ds(ci * chunk, chunk)
      cosv = cos_ref[seq_rows, :]
      sinv = sin_ref[seq_rows, :]
      for h in range(R_HEADS):
        qh = qk_ref[rows, h * 128:(h + 1) * 128]
        kh = qk_ref[rows, R_QK + h * 128:R_QK + (h + 1) * 128]
        qrot = qh * cosv + pltpu.roll(qh, 64, 1) * sinv
        krot = (kh * cosv + pltpu.roll(kh, 64, 1) * sinv) * kscale
        qb = qrot.astype(BF16)
        vh = v_ref[rows, h * 256:(h + 1) * 256]
        att = _dot_nt(qb, krot.astype(BF16)) * dmat_ref[h]
        sh = state[bi, h]
        o = _dot(att.astype(BF16), vh) + _dot(qb, sh.astype(BF16)) * qdec_ref[h]
        state[bi, h] = cdec_ref[h] * sh + _dot_tn((krot * kdec_ref[h]).astype(BF16), vh)
        mu = jnp.mean(o, axis=-1, keepdims=True)
        var = jnp.mean(jnp.square(o - mu), axis=-1, keepdims=True)
        on = (o - mu) * lax.rsqrt(var + EPS) * gn_ref[:, h * 256:(h + 1) * 256]
        o_ref[bi, seq_rows, h * 256:(h + 1) * 256] = on

  @pl.when(j == pl.num_programs(1) - 1)
  def _():
    sout_ref[...] = state[...]


def _retention(x, w16, cos, sin, tables, s0, gn, rows_per_step, chunk, nb):
  batch, t, _ = x.shape
  dmat, qdec, kdec, cdec = tables
  tc = rows_per_step
  const3 = lambda b, j: (0, 0, 0)
  wslab = lambda off: pl.BlockSpec((D_MODEL, R_V), lambda b, j: (0, off // R_V),
                                   pipeline_mode=pl.Buffered(1))
  state_spec = pl.BlockSpec((nb, R_HEADS, R_KEY_DIM, R_VAL_DIM), lambda b, j: (b, 0, 0, 0))
  return pl.pallas_call(
      functools.partial(_retention_kernel, chunk=chunk),
      grid=(batch // nb, t // tc),
      in_specs=[pl.BlockSpec((nb, tc, D_MODEL), lambda b, j: (b, j, 0)),
                wslab(_OFF[4]), wslab(_OFF[6]),
                pl.BlockSpec((tc, 128), lambda b, j: (j, 0)),
                pl.BlockSpec((tc, 128), lambda b, j: (j, 0)),
                pl.BlockSpec(dmat.shape, const3),
                pl.BlockSpec(qdec.shape, const3),
                pl.BlockSpec(kdec.shape, const3),
                pl.BlockSpec(cdec.shape, const3),
                state_spec,
                pl.BlockSpec((1, R_V), lambda b, j: (0, 0))],
      out_specs=[pl.BlockSpec((nb, tc, R_V), lambda b, j: (b, j, 0)), state_spec],
      out_shape=[jax.ShapeDtypeStruct((batch, t, R_V), F32),
                 jax.ShapeDtypeStruct((batch, R_HEADS, R_KEY_DIM, R_VAL_DIM), F32)],
      scratch_shapes=[pltpu.VMEM((nb, R_HEADS, R_KEY_DIM, R_VAL_DIM), F32),
                      pltpu.VMEM((nb * tc, 2 * R_QK), F32), pltpu.VMEM((nb * tc, R_V), BF16),
                      pltpu.VMEM((D_MODEL, R_V), BF16), pltpu.VMEM((D_MODEL, R_V), BF16)],
      compiler_params=_params(("arbitrary", "arbitrary")),
      name="retention",
  )(x, w16, w16, cos, sin, dmat, qdec, kdec, cdec, s0, gn)


def _retention_tables(c_true, c_pad):
  lg = jnp.log1p(-jnp.exp2(-5.0 - jnp.arange(R_HEADS, dtype=F32)))
  i = jnp.arange(c_pad, dtype=F32)
  live = i < c_true
  diff = i[:, None] - i[None, :]
  causal = (diff >= 0) & live[:, None] & live[None, :]
  dmat = jnp.where(causal[None], jnp.exp(jnp.where(causal, diff, 0.0)[None] * lg[:, None, None]), 0.0)
  qdec = jnp.where(live[None, :], jnp.exp((i[None, :] + 1.0) * lg[:, None]), 0.0)
  kdec = jnp.where(live[None, :], jnp.exp((c_true - 1.0 - i)[None, :] * lg[:, None]), 0.0)
  cdec = jnp.exp(c_true * lg)
  qdec = jnp.broadcast_to(qdec[:, :, None], (R_HEADS, c_pad, R_VAL_DIM))
  kdec = jnp.broadcast_to(kdec[:, :, None], (R_HEADS, c_pad, R_KEY_DIM))
  cdec = jnp.broadcast_to(cdec[:, None, None], (R_HEADS, R_KEY_DIM, R_VAL_DIM))
  return dmat, qdec, kdec, cdec


def _rope_tables(pos):
  half = R_KEY_DIM // 2
  inv = 1.0 / (ROPE_BASE ** jnp.linspace(0.0, 1.0, half, dtype=F32))
  ang = pos.astype(F32)[:, None] * inv[None, :]
  cos = jnp.cos(ang)
  sin = jnp.sin(ang)
  return jnp.concatenate([cos, cos], axis=-1), jnp.concatenate([-sin, sin], axis=-1)


def _output_kernel(x_ref, oa_ref, or_ref, wza32_ref, wzr32_ref, wga32_ref, wgb32_ref, wpa_ref,
                   wpb_ref, wo_ref, lng_ref, lnb_ref, y_ref, wza_ref, wzr_ref, wga_ref, wgb_ref):
  _cast_weights_once(pl.program_id(0) == 0, (wza32_ref, wzr32_ref, wga32_ref, wgb32_ref),
                     (wza_ref, wzr_ref, wga_ref, wgb_ref))
  tm = x_ref.shape[0]
  nsub = 2 if tm % 32 == 0 else 1
  sub = tm // nsub
  for si in range(nsub):
    rows = pl.ds(si * sub, sub)
    x = x_ref[rows, :]
    xb = x.astype(BF16)
    za = _dot(xb, wza_ref[...])
    ya = _dot((jax.nn.silu(za) * oa_ref[rows, :]).astype(BF16), wpa_ref[...])
    zr = _dot(xb, wzr_ref[...])
    yb = _dot((jax.nn.silu(zr) * or_ref[rows, :]).astype(BF16), wpb_ref[...])
    ga = _dot(xb, wga_ref[...])
    gb = _dot(xb, wgb_ref[...])
    mix = jax.nn.sigmoid(ga) * ya + jax.nn.sigmoid(gb) * yb
    out = _dot(mix.astype(BF16), wo_ref[...])
    zz = DEEPNORM_ALPHA * x + out
    zm = jnp.mean(zz, axis=-1, keepdims=True)
    zv = jnp.mean(jnp.square(zz - zm), axis=-1, keepdims=True)
    y_ref[rows, :] = (zz - zm) * lax.rsqrt(zv + EPS) * lng_ref[...] + lnb_ref[...]


def _output(x, oa, orr, w16, wpa, wpb, wo, lng, lnb, tm):
  m = x.shape[0]
  row = lambda w: pl.BlockSpec((tm, w), lambda i: (i, 0))
  full = lambda a: pl.BlockSpec(a.shape, lambda i: (0, 0), pipeline_mode=pl.Buffered(1))
  wcol = lambda off, width: pl.BlockSpec((D_MODEL, width), lambda i: (0, off // width),
                                         pipeline_mode=pl.Buffered(1))
  in_specs = [row(D_MODEL), row(A_OUT), row(R_V),
              wcol(_OFF[3], A_OUT), wcol(_OFF[7], R_V), wcol(_OFF[8], D_MODEL),
              wcol(_OFF[9], D_MODEL),
              full(wpa), full(wpb), full(wo), full(lng), full(lnb)]
  args = [x, oa, orr, w16, w16, w16, w16, wpa, wpb, wo, lng, lnb]
  return pl.pallas_call(
      _output_kernel,
      grid=(m // tm,),
      in_specs=in_specs,
      out_specs=row(D_MODEL),
      out_shape=jax.ShapeDtypeStruct((m, D_MODEL), F32),
      scratch_shapes=[pltpu.VMEM((D_MODEL, A_OUT), BF16), pltpu.VMEM((D_MODEL, R_V), BF16),
                      pltpu.VMEM((D_MODEL, D_MODEL), BF16), pltpu.VMEM((D_MODEL, D_MODEL), BF16)],
      compiler_params=_params(("arbitrary",)),
      name="gates_output",
  )(*args)


def _alibi_slopes():
  n = len(A_GROUPS) * A_HEADS
  return jnp.exp2(-8.0 * jnp.arange(1, n + 1, dtype=F32) / n).reshape(len(A_GROUPS), A_HEADS)


def kernel(x_prompt, x_sample, cache_kv_w128, cache_kv_w512, cache_kv_w2048, state_ret,
           w_in, w_pa, w_pb, w_o, gn_g, ln_g, ln_b):
  bp, tp, _ = x_prompt.shape
  bs, ts, _ = x_sample.shape
  slopes = _alibi_slopes()
  w16 = w_in
  wpa = w_pa.astype(BF16)
  wpb = w_pb.astype(BF16)
  wo = w_o.astype(BF16)
  gn = gn_g.reshape(1, R_V)
  lng = ln_g.reshape(1, D_MODEL)
  lnb = ln_b.reshape(1, D_MODEL)

  xp2 = x_prompt.reshape(bp * tp, D_MODEL)
  qkvs, kv_p, tails = [], [], []
  for g, (w, _) in enumerate(A_GROUPS):
    qkv, tail = _proj_group(xp2, w16, g, bp, tp)
    qkvs.append(qkv)
    tails.append(tail)
    kv_p.append(tail.reshape(bp, min(w, tp), 2, A_HEADS, A_HEAD_DIM))
  oa_p = _attn_prompt(qkvs, slopes, bp, tp).reshape(bp * tp, A_OUT)

  cos_p, sin_p = _rope_tables(jnp.arange(tp, dtype=jnp.int32))
  or_p, s_p = _retention(
      x_prompt, w16, cos_p, sin_p, _retention_tables(R_CHUNK, R_CHUNK),
      jnp.zeros((bp, R_HEADS, R_KEY_DIM, R_VAL_DIM), F32), gn, RET_ROWS, R_CHUNK, 1)
  y_p = _output(xp2, oa_p, or_p.reshape(bp * tp, R_V), w16, wpa, wpb, wo, lng, lnb, OUT_ROWS)
  xs2 = x_sample.reshape(bs * ts, D_MODEL)
  hs = _matmul(xs2, w16, 0, _OFF[3], F32, bs * ts, A_QKV)
  caches = (cache_kv_w128, cache_kv_w512, cache_kv_w2048)
  q_s, news = [], []
  for g in range(3):
    col = lambda part: hs[:, _OFF[part] + g * A_GW:_OFF[part] + (g + 1) * A_GW]
    qg = col(0).reshape(bs, ts * A_HEADS, A_HEAD_DIM)
    q_s.append(jnp.pad(qg, ((0, 0), (0, 128 - ts * A_HEADS), (0, 0))))
    news.append(jnp.concatenate([col(1), col(2)], axis=1).reshape(bs, ts, 8, 128))
  q_s = jnp.stack(q_s, axis=1)
  caches8 = [c.reshape(bs, c.shape[1], 8, 128) for c in caches]
  cpad = SAMPLE_CHUNK_PAD
  xs_pad = jnp.pad(x_sample, ((0, 0), (0, cpad - ts), (0, 0)))
  cos_s, sin_s = _rope_tables(PAST_LEN + jnp.arange(ts, dtype=jnp.int32))
  padtab = lambda a: jnp.pad(a, ((0, cpad - ts), (0, 0)))
  after = [t[0, :8] for t in tails]
  oa_s, u0, u1, u2 = _attn_sample(q_s, caches8, news, slopes, ts, after)
  oa_s = oa_s[:, :ts * A_HEADS].reshape(bs * ts, A_OUT)
  kv_s = [u.reshape(bs, u.shape[1], 2, A_HEADS, A_HEAD_DIM) for u in (u0, u1, u2)]
  or_s, s_s = _retention(
      xs_pad, w16, padtab(cos_s), padtab(sin_s),
      _retention_tables(ts, cpad), state_ret.astype(F32), gn, cpad, cpad, SAMPLE_RET_NB)
  or_s = or_s[:, :ts].reshape(bs * ts, R_V)
  y_s = _output(xs2, oa_s, or_s, w16, wpa, wpb, wo, lng, lnb, SAMPLE_OUT_ROWS)

  return (y_p.reshape(bp, tp, D_MODEL), y_s.reshape(bs, ts, D_MODEL),
          kv_p[0], kv_p[1], kv_p[2], s_p,
          kv_s[0], kv_s[1], kv_s[2], s_s)
```

```python
import functools

import numpy as np
import jax
import jax.numpy as jnp
from jax import lax
from jax.experimental import pallas as pl
from jax.experimental.pallas import tpu as pltpu
from jax.experimental.pallas import tpu_sc as plsc

D_MODEL = 1024
PAST_LEN = 16384
A_GROUPS = ((128, 1), (512, 4), (2048, 16))
A_HEADS = 4
A_HEAD_DIM = 128
A_GW = A_HEADS * A_HEAD_DIM
A_QKV = len(A_GROUPS) * A_GW
A_OUT = A_GW
N_DIST = 128
R_HEADS = 4
R_KEY_DIM = 128
R_VAL_DIM = 256
R_QK = R_HEADS * R_KEY_DIM
R_V = R_HEADS * R_VAL_DIM
R_CHUNK = 256
ROPE_BASE = 10000.0
EPS = 1e-5
NEG_INF = -1e30
DEEPNORM_ALPHA = 2.0 ** 0.25
_SPLIT = (A_QKV, A_QKV, A_QKV, A_OUT, R_QK, R_QK, R_V, R_V, D_MODEL, D_MODEL)
_OFF = tuple(int(v) for v in np.concatenate([[0], np.cumsum(_SPLIT)]))

BF16 = jnp.bfloat16
F32 = jnp.float32
VMEM_LIMIT = 56 * 1024 * 1024
RET_ROWS = 1024
OUT_ROWS = 512
SAMPLE_OUT_ROWS = 256
SAMPLE_CHUNK_PAD = 16
SAMPLE_RET_NB = 8

_NT = (((1,), (1,)), ((), ()))
_TN = (((0,), (0,)), ((), ()))


def _dot(a, b):
  return jnp.dot(a, b, preferred_element_type=F32)


def _dot_nt(a, b):
  return lax.dot_general(a, b, _NT, preferred_element_type=F32)


def _dot_tn(a, b):
  return lax.dot_general(a, b, _TN, preferred_element_type=F32)


def _params(sem):
  return pltpu.CompilerParams(dimension_semantics=sem, vmem_limit_bytes=VMEM_LIMIT)


def _mm_kernel(x_ref, w_ref, o_ref):
  o_ref[...] = _dot(x_ref[...].astype(BF16), w_ref[...]).astype(o_ref.dtype)


def _matmul(x, w, col0, ncols, out_dtype, tm, tn):
  m, k = x.shape
  c0 = col0 // tn
  return pl.pallas_call(
      _mm_kernel,
      grid=(m // tm, ncols // tn),
      in_specs=[pl.BlockSpec((tm, k), lambda i, j: (i, 0)),
                pl.BlockSpec((k, tn), lambda i, j: (0, c0 + j))],
      out_specs=pl.BlockSpec((tm, tn), lambda i, j: (i, j)),
      out_shape=jax.ShapeDtypeStruct((m, ncols), out_dtype),
      compiler_params=_params(("parallel", "arbitrary")),
      name="proj_matmul",
  )(x, w)


PROJ_TM = 1024


def _proj_group_kernel(x_ref, wq_ref, wk_ref, wv_ref, o_ref, tail_ref, *scratch,
                       d, first_tail, tail_rows):
  i = pl.program_id(1)
  tm = x_ref.shape[0]
  n = tm // d
  x = x_ref[...]
  xb = x.astype(BF16)
  if d == 1:
    perm = xb
  else:
    xs_ref, xp_ref = scratch
    for c in range(D_MODEL // 128):
      xs_ref[c] = x[:, c * 128:(c + 1) * 128]
    for r in range(d):
      for c in range(D_MODEL // 128):
        xp_ref[r * n:(r + 1) * n, c * 128:(c + 1) * 128] = (
            xs_ref[c, pl.ds(r, n, stride=d), :].astype(BF16))
    perm = xp_ref[...]
  nat = []
  for part, w_ref in enumerate((wq_ref, wk_ref, wv_ref)):
    res = _dot(perm, w_ref[...])
    nat.append(res)
    for r in range(d):
      for h in range(A_HEADS):
        o_ref[r, part * A_HEADS + h] = res[r * n:(r + 1) * n, h * 128:(h + 1) * 128].astype(BF16)

  @pl.when(i >= first_tail)
  def _():
    if d == 1:
      kv = nat[1:]
    else:
      kv = [_dot(xb, wk_ref[...]), _dot(xb, wv_ref[...])]
    for c in range(2 * A_HEADS):
      src = kv[c // A_HEADS][tm - tail_rows:, (c % A_HEADS) * 128:(c % A_HEADS + 1) * 128]
      tail_ref[pl.ds(c, tail_rows, stride=2 * A_HEADS), :] = src


def _proj_group(x2, w16, g, batch, seq):
  w, d = A_GROUPS[g]
  w = min(w, seq)
  tm = PROJ_TM
  nblk = seq // tm
  tail_rows = min(tm, w)
  first_tail = nblk - max(w // tm, 1)
  ls = seq // d
  wspec = lambda part: pl.BlockSpec((D_MODEL, A_GW), lambda b, i: (0, 3 * part + g))
  scratch = []
  if d > 1:
    scratch = [pltpu.VMEM((D_MODEL // 128, tm, 128), F32), pltpu.VMEM((tm, D_MODEL), BF16)]
  return pl.pallas_call(
      functools.partial(_proj_group_kernel, d=d, first_tail=first_tail, tail_rows=tail_rows),
      grid=(batch, nblk),
      in_specs=[pl.BlockSpec((tm, D_MODEL), lambda b, i: (b * nblk + i, 0)),
                wspec(0), wspec(1), wspec(2)],
      out_specs=[pl.BlockSpec((None, d, 3 * A_HEADS, tm // d, 128), lambda b, i: (b, 0, 0, i, 0)),
                 pl.BlockSpec((None, tail_rows * 8, 128),
                              lambda b, i: (b, jnp.maximum(i - first_tail, 0), 0))],
      out_shape=[jax.ShapeDtypeStruct((batch, d, 3 * A_HEADS, ls, 128), BF16),
                 jax.ShapeDtypeStruct((batch, w * 8, 128), F32)],
      scratch_shapes=scratch,
      compiler_params=_params(("parallel", "arbitrary")),
      name="proj_group",
  )(x2, w16, w16, w16)


ATT_BQ = 128
ATT_SET = 4
LOG2E = 1.4426950408889634


def _attn_prompt_kernel(slopes_ref, *refs, seq):
  qkv = refs[:9]
  o_ref = refs[9]
  oacc = refs[10:13]
  lacc = refs[13:16]
  bias_ref, s_buf, p_buf, l_buf = refs[16:20]
  h = pl.program_id(0)
  scale = A_HEAD_DIM ** -0.5

  @pl.when(pl.program_id(1) == 0)
  def _():
    for g, (_, d) in enumerate(A_GROUPS):
      ls = seq // d
      bq = min(ATT_BQ, ls)
      klen = min(bq + N_DIST, ls)
      slope = slopes_ref[g, h]
      qi = lax.broadcasted_iota(jnp.int32, (bq, klen), 0)
      kj = lax.broadcasted_iota(jnp.int32, (bq, klen), 1)
      for which, off in enumerate((0, N_DIST)):
        delta = qi - kj + off
        valid = (delta >= 0) & (delta <= N_DIST)
        dist = (delta * d).astype(F32)
        bias_ref[2 * g + which, :bq, :klen] = jnp.where(valid, -(slope * LOG2E) * dist, NEG_INF)

  for g, (_, d) in enumerate(A_GROUPS):
    q_ref, k_ref, v_ref = qkv[3 * g:3 * g + 3]
    ls = seq // d
    bq = min(ATT_BQ, ls)
    nqb = ls // bq
    klen = min(bq + N_DIST, ls)

    def coords(idx, nqb=nqb, bq=bq):
      r = idx // nqb
      qb = idx % nqb
      q0 = pl.multiple_of(qb * bq, bq)
      k0 = pl.multiple_of(jnp.maximum(qb * bq - N_DIST, 0), N_DIST)
      return r, qb, q0, k0

    def put_rows(ref, idx, val, d=d, bq=bq):
      r, _, q0, _ = coords(idx)
      if d == 1:
        ref[pl.ds(q0, bq), :] = val
      else:
        ref[pl.ds(r + q0 * d, bq, stride=d), :] = val

    def scores(idx, slot, bq=bq, klen=klen, q_ref=q_ref, k_ref=k_ref):
      r, _, q0, k0 = coords(idx)
      s_buf[slot, :bq, :klen] = _dot_nt(q_ref[r, pl.ds(q0, bq), :], k_ref[r, pl.ds(k0, klen), :])

    def softmax(idx, slot, g=g, bq=bq, klen=klen):
      _, qb, _, _ = coords(idx)
      s = (s_buf[slot, :bq, :klen] * (scale * LOG2E)
           + bias_ref[2 * g + jnp.minimum(qb, 1), :bq, :klen])
      m = jnp.max(s, axis=-1, keepdims=True)
      p = jnp.exp2(s - m)
      l = jnp.sum(p, axis=-1, keepdims=True)
      p_buf[slot, :bq, :klen] = p.astype(BF16)
      l_buf[slot, :bq, :] = jnp.broadcast_to(l, (bq, 128))
      put_rows(lacc[g], idx, jnp.broadcast_to(m + jnp.log2(l), (bq, 128)))

    def values(idx, slot, g=g, bq=bq, klen=klen, v_ref=v_ref):
      r, _, _, k0 = coords(idx)
      o = _dot(p_buf[slot, :bq, :klen], v_ref[r, pl.ds(k0, klen), :]) / l_buf[slot, :bq, :]
      put_rows(oacc[g], idx, o)

    nblk = d * nqb
    ns = ATT_SET

    def run(stage, first, half):
      for t in range(ns):
        stage(first + t, half * ns + t)

    run(scores, 0, 0)
    run(softmax, 0, 0)
    run(scores, ns, 1)

    def steady(j, carry):
      i = 2 * ns * (j + 1)
      run(values, i - 2 * ns, 0)
      run(softmax, i - ns, 1)
      run(scores, i, 0)
      run(values, i - ns, 1)
      run(softmax, i, 0)
      run(scores, i + ns, 1)
      return carry

    lax.fori_loop(0, (nblk - 2 * ns) // (2 * ns), steady, 0)
    run(values, nblk - 2 * ns, 0)
    run(softmax, nblk - ns, 1)
    run(values, nblk - ns, 1)

  mrows = 512

  def merge(i, carry):
    r0 = pl.multiple_of(i * mrows, mrows)
    l0 = lacc[0][pl.ds(r0, mrows), :]
    l1 = lacc[1][pl.ds(r0, mrows), :]
    l2 = lacc[2][pl.ds(r0, mrows), :]
    mx = jnp.maximum(jnp.maximum(l0, l1), l2)
    e0 = jnp.exp2(l0 - mx)
    e1 = jnp.exp2(l1 - mx)
    e2 = jnp.exp2(l2 - mx)
    den = e0 + e1 + e2
    acc = (e0 * oacc[0][pl.ds(r0, mrows), :] + e1 * oacc[1][pl.ds(r0, mrows), :]
           + e2 * oacc[2][pl.ds(r0, mrows), :])
    o_ref[pl.ds(r0, mrows), :] = acc / den
    return carry

  lax.fori_loop(0, seq // mrows, merge, 0)


def _attn_prompt(qkvs, slopes, batch, seq):
  in_specs = [pl.BlockSpec(memory_space=pltpu.SMEM)]
  args = [slopes]
  for g, (_, d) in enumerate(A_GROUPS):
    ls = seq // d
    for part in range(3):
      in_specs.append(pl.BlockSpec(
          (None, d, None, ls, 128),
          lambda h, b, part=part: (b, 0, part * A_HEADS + h, 0, 0)))
      args.append(qkvs[g])
  scratch = ([pltpu.VMEM((seq, 128), F32) for _ in range(6)]
             + [pltpu.VMEM((6, ATT_BQ, ATT_BQ + N_DIST), F32),
                pltpu.VMEM((2 * ATT_SET, ATT_BQ, ATT_BQ + N_DIST), F32),
                pltpu.VMEM((2 * ATT_SET, ATT_BQ, ATT_BQ + N_DIST), BF16),
                pltpu.VMEM((2 * ATT_SET, ATT_BQ, 128), F32)])
  return pl.pallas_call(
      functools.partial(_attn_prompt_kernel, seq=seq),
      grid=(A_HEADS, batch),
      in_specs=in_specs,
      out_specs=pl.BlockSpec((None, seq, 128), lambda h, b: (b, 0, h)),
      out_shape=jax.ShapeDtypeStruct((batch, seq, A_OUT), F32),
      scratch_shapes=scratch,
      compiler_params=_params(("parallel", "arbitrary")),
      name="attn_prompt",
  )(*args)


SC_ROLLED = 2
SC_TILE_BYTES = 480 * 1024
SAMPLE_NB = 1


def _sc_roll(caches, shift, after):
  batch, _, slots, lanes = caches[0].shape
  row_bytes = slots * lanes * caches[0].dtype.itemsize
  rows = [c.shape[1] * slots for c in caches]
  crows = [max(c for c in range(1, SC_TILE_BYTES // row_bytes + 1) if (k.shape[1] - shift) % c == 0)
           for k in caches]
  nchunks = [(k.shape[1] - shift) // c for k, c in zip(caches, crows)]
  chunks = [c * slots for c in crows]
  mesh = plsc.VectorSubcoreMesh(core_axis_name="c", subcore_axis_name="s")
  nsub = mesh.num_subcores
  assert mesh.num_cores * nsub == batch, "one vector subcore per batch entry"
  n = len(caches)

  def body(*refs):
    x_hbm = refs[:n]
    o_hbm = refs[n + len(after):2 * n + len(after)]
    buf = refs[-1]
    entry = lax.axis_index("c") * nsub + lax.axis_index("s")
    for g in range(n):
      base = entry * rows[g]

      chunk = chunks[g]
      stage = buf.at[pl.ds(0, chunk)]

      @pl.loop(0, nchunks[g])
      def _(i, g=g, base=base, chunk=chunk, stage=stage):
        pltpu.sync_copy(x_hbm[g].at[pl.ds(base + shift * slots + i * chunk, chunk)], stage)
        pltpu.sync_copy(stage, o_hbm[g].at[pl.ds(base + i * chunk, chunk)])

  flats = [c.reshape(batch * r, lanes) for c, r in zip(caches, rows)]
  outs = pl.kernel(body, out_type=[jax.ShapeDtypeStruct(f.shape, f.dtype) for f in flats],
                   mesh=mesh,
                   scratch_types=[pltpu.VMEM((max(chunks), lanes), caches[0].dtype)])(*flats, *after)
  return [o.reshape(c.shape) for o, c in zip(outs, caches)]


def _attn_sample_kernel(slopes_ref, q_ref, c0_ref, c1_ref, c2_ref, n0_ref, n1_ref, n2_ref,
                        r0_ref, r1_ref, r2_ref, o_ref, u0_ref, u1_ref, u2_ref, sems, bias_ref,
                        *, tq):
  b = pl.program_id(0)
  cache_refs = (c0_ref, c1_ref, c2_ref)
  new_refs = (n0_ref, n1_ref, n2_ref)
  out_refs = (u0_ref, u1_ref, u2_ref)
  del r0_ref, r1_ref, r2_ref

  nb = q_ref.shape[0]
  copies = []
  for e in range(nb):
    for g, (w, _) in enumerate(A_GROUPS):
      tail = pltpu.make_async_copy(new_refs[g].at[e], out_refs[g].at[b * nb + e, pl.ds(w - tq, tq)],
                                   sems.at[e * len(A_GROUPS) + g])
      tail.start()
      copies.append(tail)

  scale = A_HEAD_DIM ** -0.5
  nreal = (N_DIST + tq) * A_HEADS
  nrow = -(-nreal // 128) * 128

  @pl.when(b == 0)
  def _():
    ri = lax.broadcasted_iota(jnp.int32, (nrow, 128), 0)
    cj = lax.broadcasted_iota(jnp.int32, (nrow, 128), 1)
    slot = ri % A_HEADS
    key = ri // A_HEADS
    tcol = cj // A_HEADS
    hcol = cj % A_HEADS
    col_ok = cj < tq * A_HEADS
    for g, (_, d) in enumerate(A_GROUPS):
      qt = tcol // d
      nidx = key - N_DIST
      in_cache = key < N_DIST
      valid_c = in_cache & (key >= qt)
      dist_c = d * (N_DIST + qt - key)
      valid_n = (~in_cache) & (nidx <= tcol) & ((tcol - nidx) % d == 0)
      dist_n = tcol - nidx
      valid = (valid_c | valid_n) & (slot == hcol) & col_ok & (ri < nreal)
      dist = jnp.where(in_cache, dist_c, dist_n).astype(F32)
      slope = jnp.zeros((nrow, 128), F32)
      for hh in range(A_HEADS):
        slope = jnp.where(hcol == hh, slopes_ref[g, hh], slope)
      bias_ref[g] = jnp.where(valid, -slope * dist, NEG_INF)

  qrow_t = lax.broadcasted_iota(jnp.int32, (128, 1), 0) // A_HEADS
  xpad = jnp.zeros((nrow - nreal, 128), F32)
  for e in range(nb):
    _attend_entry(e, q_ref, cache_refs, new_refs, bias_ref, o_ref, qrow_t, xpad, scale, tq)

  for c in copies:
    c.wait()


def _attend_entry(e, q_ref, cache_refs, new_refs, bias_ref, o_ref, qrow_t, xpad, scale, tq):
  lse_g = []
  o_g = []
  nh = A_HEADS
  for g, (w, d) in enumerate(A_GROUPS):
    qg = q_ref[e, g]
    nres = min(d, tq)
    xk, xv, qs = [], [], []
    for rho in range(nres):
      parts = []
      for lo in (0, nh):
        if g == SC_ROLLED:
          xc = cache_refs[g][e, :, rho, lo:lo + nh, :]
        elif d == 1:
          xc = cache_refs[g][e, :, lo:lo + nh, :]
        else:
          xc = cache_refs[g][e, pl.ds(rho, N_DIST, stride=d), lo:lo + nh, :]
        xn = new_refs[g][e, :, lo:lo + nh, :]
        parts.append(jnp.concatenate(
            [xc.reshape(N_DIST * nh, 128), xn.reshape(tq * nh, 128), xpad], axis=0).astype(BF16))
      xk.append(parts[0])
      xv.append(parts[1])
      qs.append(qg if d == 1 else jnp.where(qrow_t % d == rho, qg, 0.0))
    cat = lambda xs: xs[0] if nres == 1 else jnp.concatenate(xs, axis=1)
    s = _dot_nt(cat(xk), cat(qs).astype(BF16)) * scale + bias_ref[g]
    m = jnp.max(s, axis=0, keepdims=True)
    p = jnp.exp(s - m)
    l = jnp.sum(p, axis=0, keepdims=True)
    pvt = (p / l).T.astype(BF16)
    full = _dot(pvt, cat(xv))
    o = full[:, :128]
    for rho in range(1, nres):
      o = jnp.where(qrow_t % d == rho, full[:, rho * 128:(rho + 1) * 128], o)
    o_g.append(o)
    lse_g.append(jnp.broadcast_to(m + jnp.log(l), (128, 128)).T)
  mx = jnp.maximum(jnp.maximum(lse_g[0], lse_g[1]), lse_g[2])
  es = [jnp.exp(v - mx) for v in lse_g]
  den = es[0] + es[1] + es[2]
  o_ref[e] = (es[0] * o_g[0] + es[1] * o_g[1] + es[2] * o_g[2]) / den


def _attn_sample(q, caches, news, slopes, tq, after):
  batch = q.shape[0]
  nb = SAMPLE_NB
  in_specs = [pl.BlockSpec(memory_space=pltpu.SMEM),
              pl.BlockSpec((nb, 3, 128, 128), lambda b: (b, 0, 0, 0))]
  args = [slopes, q]
  for g, (w, d) in enumerate(A_GROUPS):
    if g == SC_ROLLED:
      in_specs.append(pl.BlockSpec((nb, N_DIST, min(d, tq), 8, 128), lambda b: (b, 0, 0, 0, 0)))
      args.append(caches[g].reshape(batch, N_DIST, d, 8, 128))
    else:
      in_specs.append(pl.BlockSpec((nb, w, 8, 128), lambda b: (b, 0, 0, 0)))
      args.append(caches[g])
  for g in range(3):
    in_specs.append(pl.BlockSpec((nb, tq, 8, 128), lambda b: (b, 0, 0, 0)))
    args.append(news[g])
  rolled_in = len(args)
  for r in _sc_roll(caches, tq, after):
    in_specs.append(pl.BlockSpec(memory_space=pl.ANY))
    args.append(r)
  out_shape = [jax.ShapeDtypeStruct((batch, 128, 128), F32)]
  out_specs = [pl.BlockSpec((nb, 128, 128), lambda b: (b, 0, 0))]
  for g, (w, _) in enumerate(A_GROUPS):
    out_shape.append(jax.ShapeDtypeStruct((batch, w, 8, 128), F32))
    out_specs.append(pl.BlockSpec(memory_space=pl.ANY))
  return pl.pallas_call(
      functools.partial(_attn_sample_kernel, tq=tq),
      grid=(batch // nb,),
      in_specs=in_specs,
      out_specs=out_specs,
      out_shape=out_shape,
      input_output_aliases={rolled_in + g: 1 + g for g in range(len(A_GROUPS))},
      scratch_shapes=[pltpu.SemaphoreType.DMA((nb * len(A_GROUPS),)),
                      pltpu.VMEM((len(A_GROUPS), -(-(N_DIST + tq) * A_HEADS // 128) * 128, 128),
                                 F32)],
      compiler_params=_params(("arbitrary",)),
      name="attn_sample",
  )(*args)


def _retention_kernel(x_ref, wqk_ref, wv_ref, cos_ref, sin_ref, dmat_ref, qdec_ref, kdec_ref,
                      cdec_ref, s0_ref, gn_ref, o_ref, sout_ref, state, qk_ref, v_ref, *, chunk):
  j = pl.program_id(1)
  nb, tc, _ = x_ref.shape

  @pl.when(j == 0)
  def _():
    state[...] = s0_ref[...]

  xb = x_ref[...].reshape(nb * tc, D_MODEL).astype(BF16)
  qk_ref[...] = _dot(xb, wqk_ref[...])
  v_ref[...] = _dot(xb, wv_ref[...]).astype(BF16)
  kscale = R_KEY_DIM ** -0.5
  for bi in range(nb):
    for ci in range(tc // chunk):
      rows = pl.ds(bi * tc + ci * chunk, chunk)
      seq_rows = pl.ds(ci * chunk, chunk)
      cosv = cos_ref[seq_rows, :]
      sinv = sin_ref[seq_rows, :]
      for h in range(R_HEADS):
        qh = qk_ref[rows, h * 128:(h + 1) * 128]
        kh = qk_ref[rows, R_QK + h * 128:R_QK + (h + 1) * 128]
        qrot = qh * cosv + pltpu.roll(qh, 64, 1) * sinv
        krot = (kh * cosv + pltpu.roll(kh, 64, 1) * sinv) * kscale
        qb = qrot.astype(BF16)
        vh = v_ref[rows, h * 256:(h + 1) * 256]
        att = _dot_nt(qb, krot.astype(BF16)) * dmat_ref[h]
        sh = state[bi, h]
        o = _dot(att.astype(BF16), vh) + _dot(qb, sh.astype(BF16)) * qdec_ref[h]
        state[bi, h] = cdec_ref[h] * sh + _dot_tn((krot * kdec_ref[h]).astype(BF16), vh)
        mu = jnp.mean(o, axis=-1, keepdims=True)
        var = jnp.mean(jnp.square(o - mu), axis=-1, keepdims=True)
        on = (o - mu) * lax.rsqrt(var + EPS) * gn_ref[:, h * 256:(h + 1) * 256]
        o_ref[bi, seq_rows, h * 256:(h + 1) * 256] = on

  @pl.when(j == pl.num_programs(1) - 1)
  def _():
    sout_ref[...] = state[...]


def _retention(x, w16, cos, sin, tables, s0, gn, rows_per_step, chunk, nb):
  batch, t, _ = x.shape
  dmat, qdec, kdec, cdec = tables
  tc = rows_per_step
  const3 = lambda b, j: (0, 0, 0)
  wslab = lambda off: pl.BlockSpec((D_MODEL, R_V), lambda b, j: (0, off // R_V),
                                   pipeline_mode=pl.Buffered(1))
  state_spec = pl.BlockSpec((nb, R_HEADS, R_KEY_DIM, R_VAL_DIM), lambda b, j: (b, 0, 0, 0))
  return pl.pallas_call(
      functools.partial(_retention_kernel, chunk=chunk),
      grid=(batch // nb, t // tc),
      in_specs=[pl.BlockSpec((nb, tc, D_MODEL), lambda b, j: (b, j, 0)),
                wslab(_OFF[4]), wslab(_OFF[6]),
                pl.BlockSpec((tc, 128), lambda b, j: (j, 0)),
                pl.BlockSpec((tc, 128), lambda b, j: (j, 0)),
                pl.BlockSpec(dmat.shape, const3),
                pl.BlockSpec(qdec.shape, const3),
                pl.BlockSpec(kdec.shape, const3),
                pl.BlockSpec(cdec.shape, const3),
                state_spec,
                pl.BlockSpec((1, R_V), lambda b, j: (0, 0))],
      out_specs=[pl.BlockSpec((nb, tc, R_V), lambda b, j: (b, j, 0)), state_spec],
      out_shape=[jax.ShapeDtypeStruct((batch, t, R_V), F32),
                 jax.ShapeDtypeStruct((batch, R_HEADS, R_KEY_DIM, R_VAL_DIM), F32)],
      scratch_shapes=[pltpu.VMEM((nb, R_HEADS, R_KEY_DIM, R_VAL_DIM), F32),
                      pltpu.VMEM((nb * tc, 2 * R_QK), F32), pltpu.VMEM((nb * tc, R_V), BF16)],
      compiler_params=_params(("parallel", "arbitrary")),
      name="retention",
  )(x, w16, w16, cos, sin, dmat, qdec, kdec, cdec, s0, gn)


def _retention_tables(c_true, c_pad):
  lg = jnp.log1p(-jnp.exp2(-5.0 - jnp.arange(R_HEADS, dtype=F32)))
  i = jnp.arange(c_pad, dtype=F32)
  live = i < c_true
  diff = i[:, None] - i[None, :]
  causal = (diff >= 0) & live[:, None] & live[None, :]
  dmat = jnp.where(causal[None], jnp.exp(jnp.where(causal, diff, 0.0)[None] * lg[:, None, None]), 0.0)
  qdec = jnp.where(live[None, :], jnp.exp((i[None, :] + 1.0) * lg[:, None]), 0.0)
  kdec = jnp.where(live[None, :], jnp.exp((c_true - 1.0 - i)[None, :] * lg[:, None]), 0.0)
  cdec = jnp.exp(c_true * lg)
  qdec = jnp.broadcast_to(qdec[:, :, None], (R_HEADS, c_pad, R_VAL_DIM))
  kdec = jnp.broadcast_to(kdec[:, :, None], (R_HEADS, c_pad, R_KEY_DIM))
  cdec = jnp.broadcast_to(cdec[:, None, None], (R_HEADS, R_KEY_DIM, R_VAL_DIM))
  return dmat, qdec, kdec, cdec


def _rope_tables(pos):
  half = R_KEY_DIM // 2
  inv = 1.0 / (ROPE_BASE ** jnp.linspace(0.0, 1.0, half, dtype=F32))
  ang = pos.astype(F32)[:, None] * inv[None, :]
  cos = jnp.cos(ang)
  sin = jnp.sin(ang)
  return jnp.concatenate([cos, cos], axis=-1), jnp.concatenate([-sin, sin], axis=-1)


def _output_kernel(x_ref, oa_ref, or_ref, wza_ref, wzr_ref, wga_ref, wgb_ref, wpa_ref, wpb_ref,
                   wo_ref, lng_ref, lnb_ref, y_ref):
  tm = x_ref.shape[0]
  nsub = 2 if tm % 32 == 0 else 1
  sub = tm // nsub
  for si in range(nsub):
    rows = pl.ds(si * sub, sub)
    x = x_ref[rows, :]
    xb = x.astype(BF16)
    za = _dot(xb, wza_ref[...])
    ya = _dot((jax.nn.silu(za) * oa_ref[rows, :]).astype(BF16), wpa_ref[...])
    zr = _dot(xb, wzr_ref[...])
    yb = _dot((jax.nn.silu(zr) * or_ref[rows, :]).astype(BF16), wpb_ref[...])
    ga = _dot(xb, wga_ref[...])
    gb = _dot(xb, wgb_ref[...])
    mix = jax.nn.sigmoid(ga) * ya + jax.nn.sigmoid(gb) * yb
    out = _dot(mix.astype(BF16), wo_ref[...])
    zz = DEEPNORM_ALPHA * x + out
    zm = jnp.mean(zz, axis=-1, keepdims=True)
    zv = jnp.mean(jnp.square(zz - zm), axis=-1, keepdims=True)
    y_ref[rows, :] = (zz - zm) * lax.rsqrt(zv + EPS) * lng_ref[...] + lnb_ref[...]


def _output(x, oa, orr, w16, wpa, wpb, wo, lng, lnb, tm):
  m = x.shape[0]
  row = lambda w: pl.BlockSpec((tm, w), lambda i: (i, 0))
  full = lambda a: pl.BlockSpec(a.shape, lambda i: (0, 0), pipeline_mode=pl.Buffered(1))
  wcol = lambda off, width: pl.BlockSpec((D_MODEL, width), lambda i: (0, off // width),
                                         pipeline_mode=pl.Buffered(1))
  in_specs = [row(D_MODEL), row(A_OUT), row(R_V),
              wcol(_OFF[3], A_OUT), wcol(_OFF[7], R_V), wcol(_OFF[8], D_MODEL),
              wcol(_OFF[9], D_MODEL),
              full(wpa), full(wpb), full(wo), full(lng), full(lnb)]
  args = [x, oa, orr, w16, w16, w16, w16, wpa, wpb, wo, lng, lnb]
  return pl.pallas_call(
      _output_kernel,
      grid=(m // tm,),
      in_specs=in_specs,
      out_specs=row(D_MODEL),
      out_shape=jax.ShapeDtypeStruct((m, D_MODEL), F32),
      compiler_params=_params(("parallel",)),
      name="gates_output",
  )(*args)


def _alibi_slopes():
  n = len(A_GROUPS) * A_HEADS
  return jnp.exp2(-8.0 * jnp.arange(1, n + 1, dtype=F32) / n).reshape(len(A_GROUPS), A_HEADS)


def kernel(x_prompt, x_sample, cache_kv_w128, cache_kv_w512, cache_kv_w2048, state_ret,
           w_in, w_pa, w_pb, w_o, gn_g, ln_g, ln_b):
  bp, tp, _ = x_prompt.shape
  bs, ts, _ = x_sample.shape
  slopes = _alibi_slopes()
  w16 = w_in.astype(BF16)
  wpa = w_pa.astype(BF16)
  wpb = w_pb.astype(BF16)
  wo = w_o.astype(BF16)
  gn = gn_g.reshape(1, R_V)
  lng = ln_g.reshape(1, D_MODEL)
  lnb = ln_b.reshape(1, D_MODEL)

  xp2 = x_prompt.reshape(bp * tp, D_MODEL)
  qkvs, kv_p = [], []
  for g, (w, _) in enumerate(A_GROUPS):
    qkv, tail = _proj_group(xp2, w16, g, bp, tp)
    qkvs.append(qkv)
    kv_p.append(tail.reshape(bp, min(w, tp), 2, A_HEADS, A_HEAD_DIM))
  oa_p = _attn_prompt(qkvs, slopes, bp, tp).reshape(bp * tp, A_OUT)

  cos_p, sin_p = _rope_tables(jnp.arange(tp, dtype=jnp.int32))
  or_p, s_p = _retention(
      x_prompt, w16, cos_p, sin_p, _retention_tables(R_CHUNK, R_CHUNK),
      jnp.zeros((bp, R_HEADS, R_KEY_DIM, R_VAL_DIM), F32), gn, RET_ROWS, R_CHUNK, 1)
  y_p = _output(xp2, oa_p, or_p.reshape(bp * tp, R_V), w16, wpa, wpb, wo, lng, lnb, OUT_ROWS)
  xs2 = x_sample.reshape(bs * ts, D_MODEL)
  hs = _matmul(xs2, w16, 0, _OFF[3], F32, bs * ts, A_GW)
  caches = (cache_kv_w128, cache_kv_w512, cache_kv_w2048)
  q_s, news = [], []
  for g in range(3):
    col = lambda part: hs[:, _OFF[part] + g * A_GW:_OFF[part] + (g + 1) * A_GW]
    qg = col(0).reshape(bs, ts * A_HEADS, A_HEAD_DIM)
    q_s.append(jnp.pad(qg, ((0, 0), (0, 128 - ts * A_HEADS), (0, 0))))
    news.append(jnp.concatenate([col(1), col(2)], axis=1).reshape(bs, ts, 8, 128))
  q_s = jnp.stack(q_s, axis=1)
  caches8 = [c.reshape(bs, c.shape[1], 8, 128) for c in caches]
  cpad = SAMPLE_CHUNK_PAD
  xs_pad = jnp.pad(x_sample, ((0, 0), (0, cpad - ts), (0, 0)))
  cos_s, sin_s = _rope_tables(PAST_LEN + jnp.arange(ts, dtype=jnp.int32))
  padtab = lambda a: jnp.pad(a, ((0, cpad - ts), (0, 0)))
  after = [oa_p[:8, :128]]
  oa_s, u0, u1, u2 = _attn_sample(q_s, caches8, news, slopes, ts, after)
  oa_s = oa_s[:, :ts * A_HEADS].reshape(bs * ts, A_OUT)
  kv_s = [u.reshape(bs, u.shape[1], 2, A_HEADS, A_HEAD_DIM) for u in (u0, u1, u2)]
  or_s, s_s = _retention(
      xs_pad, w16, padtab(cos_s), padtab(sin_s),
      _retention_tables(ts, cpad), state_ret.astype(F32), gn, cpad, cpad, SAMPLE_RET_NB)
  or_s = or_s[:, :ts].reshape(bs * ts, R_V)
  y_s = _output(xs2, oa_s, or_s, w16, wpa, wpb, wo, lng, lnb, SAMPLE_OUT_ROWS)

  return (y_p.reshape(bp, tp, D_MODEL), y_s.reshape(bs, ts, D_MODEL),
          kv_p[0], kv_p[1], kv_p[2], s_p,
          kv_s[0], kv_s[1], kv_s[2], s_s)
```

```python
import functools

import numpy as np
import jax
import jax.numpy as jnp
from jax import lax
from jax.experimental import pallas as pl
from jax.experimental.pallas import tpu as pltpu
from jax.experimental.pallas import tpu_sc as plsc

D_MODEL = 1024
PAST_LEN = 16384
A_GROUPS = ((128, 1), (512, 4), (2048, 16))
A_HEADS = 4
A_HEAD_DIM = 128
A_GW = A_HEADS * A_HEAD_DIM
A_QKV = len(A_GROUPS) * A_GW
A_OUT = A_GW
N_DIST = 128
R_HEADS = 4
R_KEY_DIM = 128
R_VAL_DIM = 256
R_QK = R_HEADS * R_KEY_DIM
R_V = R_HEADS * R_VAL_DIM
R_CHUNK = 256
ROPE_BASE = 10000.0
EPS = 1e-5
NEG_INF = -1e30
DEEPNORM_ALPHA = 2.0 ** 0.25
_SPLIT = (A_QKV, A_QKV, A_QKV, A_OUT, R_QK, R_QK, R_V, R_V, D_MODEL, D_MODEL)
_OFF = tuple(int(v) for v in np.concatenate([[0], np.cumsum(_SPLIT)]))

BF16 = jnp.bfloat16
F32 = jnp.float32
VMEM_LIMIT = 56 * 1024 * 1024
RET_ROWS = 1024
OUT_ROWS = 512
SAMPLE_OUT_ROWS = 256
SAMPLE_CHUNK_PAD = 16
SAMPLE_RET_NB = 8

_NT = (((1,), (1,)), ((), ()))
_TN = (((0,), (0,)), ((), ()))


def _dot(a, b):
  return jnp.dot(a, b, preferred_element_type=F32)


def _dot_nt(a, b):
  return lax.dot_general(a, b, _NT, preferred_element_type=F32)


def _dot_tn(a, b):
  return lax.dot_general(a, b, _TN, preferred_element_type=F32)


def _params(sem):
  return pltpu.CompilerParams(dimension_semantics=sem, vmem_limit_bytes=VMEM_LIMIT)


PROJ_TM = 1024


def _proj_group_kernel(x_ref, xs_ref, wq_ref, wk_ref, wv_ref, o_ref, tail_ref, qs_ref, ns_ref,
                       *scratch, d, first_tail, tail_rows):
  i = pl.program_id(1)

  @pl.when((pl.program_id(0) == 0) & (i == 0))
  def _():
    xsb = xs_ref[...].astype(BF16)
    ns_rows = xs_ref.shape[0]
    sq = _dot(xsb, wq_ref[...])
    skv = [_dot(xsb, wk_ref[...]), _dot(xsb, wv_ref[...])]
    for h in range(A_HEADS):
      qs_ref[pl.ds(h, ns_rows, stride=A_HEADS), :] = sq[:, h * 128:(h + 1) * 128]
    for c in range(2 * A_HEADS):
      ns_ref[pl.ds(c, ns_rows, stride=2 * A_HEADS), :] = (
          skv[c // A_HEADS][:, (c % A_HEADS) * 128:(c % A_HEADS + 1) * 128])

  tm = x_ref.shape[0]
  n = tm // d
  x = x_ref[...]
  xb = x.astype(BF16)
  if d == 1:
    perm = xb
  else:
    xs_ref, xp_ref = scratch
    for c in range(D_MODEL // 128):
      xs_ref[c] = x[:, c * 128:(c + 1) * 128]
    for r in range(d):
      for c in range(D_MODEL // 128):
        xp_ref[r * n:(r + 1) * n, c * 128:(c + 1) * 128] = (
            xs_ref[c, pl.ds(r, n, stride=d), :].astype(BF16))
    perm = xp_ref[...]
  nat = []
  for part, w_ref in enumerate((wq_ref, wk_ref, wv_ref)):
    res = _dot(perm, w_ref[...])
    nat.append(res)
    for r in range(d):
      for h in range(A_HEADS):
        o_ref[r, part * A_HEADS + h] = res[r * n:(r + 1) * n, h * 128:(h + 1) * 128].astype(BF16)

  @pl.when(i >= first_tail)
  def _():
    if d == 1:
      kv = nat[1:]
    else:
      kv = [_dot(xb, wk_ref[...]), _dot(xb, wv_ref[...])]
    for c in range(2 * A_HEADS):
      src = kv[c // A_HEADS][tm - tail_rows:, (c % A_HEADS) * 128:(c % A_HEADS + 1) * 128]
      tail_ref[pl.ds(c, tail_rows, stride=2 * A_HEADS), :] = src


def _proj_group(x2, xs2, w16, g, batch, seq):
  ns = xs2.shape[0]
  w, d = A_GROUPS[g]
  w = min(w, seq)
  tm = PROJ_TM
  nblk = seq // tm
  tail_rows = min(tm, w)
  first_tail = nblk - max(w // tm, 1)
  ls = seq // d
  wspec = lambda part: pl.BlockSpec((D_MODEL, A_GW), lambda b, i: (0, 3 * part + g))
  scratch = []
  if d > 1:
    scratch = [pltpu.VMEM((D_MODEL // 128, tm, 128), F32), pltpu.VMEM((tm, D_MODEL), BF16)]
  return pl.pallas_call(
      functools.partial(_proj_group_kernel, d=d, first_tail=first_tail, tail_rows=tail_rows),
      grid=(batch, nblk),
      in_specs=[pl.BlockSpec((tm, D_MODEL), lambda b, i: (b * nblk + i, 0)),
                pl.BlockSpec((ns, D_MODEL), lambda b, i: (0, 0)),
                wspec(0), wspec(1), wspec(2)],
      out_specs=[pl.BlockSpec((None, d, 3 * A_HEADS, tm // d, 128), lambda b, i: (b, 0, 0, i, 0)),
                 pl.BlockSpec((None, tail_rows * 8, 128),
                              lambda b, i: (b, jnp.maximum(i - first_tail, 0), 0)),
                 pl.BlockSpec((ns * A_HEADS, 128), lambda b, i: (0, 0)),
                 pl.BlockSpec((ns * 2 * A_HEADS, 128), lambda b, i: (0, 0))],
      out_shape=[jax.ShapeDtypeStruct((batch, d, 3 * A_HEADS, ls, 128), BF16),
                 jax.ShapeDtypeStruct((batch, w * 8, 128), F32),
                 jax.ShapeDtypeStruct((ns * A_HEADS, 128), F32),
                 jax.ShapeDtypeStruct((ns * 2 * A_HEADS, 128), F32)],
      scratch_shapes=scratch,
      compiler_params=_params(("arbitrary", "arbitrary")),
      name="proj_group",
  )(x2, xs2, w16, w16, w16)


ATT_BQ = 128
ATT_SET = 4
LOG2E = 1.4426950408889634


def _attn_prompt_kernel(slopes_ref, *refs, seq):
  qkv = refs[:9]
  o_ref = refs[9]
  oacc = refs[10:13]
  lacc = refs[13:16]
  bias_ref, s_buf, p_buf, l_buf = refs[16:20]
  h = pl.program_id(0)
  scale = A_HEAD_DIM ** -0.5

  @pl.when(pl.program_id(1) == 0)
  def _():
    for g, (_, d) in enumerate(A_GROUPS):
      ls = seq // d
      bq = min(ATT_BQ, ls)
      klen = min(bq + N_DIST, ls)
      slope = slopes_ref[g, h]
      qi = lax.broadcasted_iota(jnp.int32, (bq, klen), 0)
      kj = lax.broadcasted_iota(jnp.int32, (bq, klen), 1)
      for which, off in enumerate((0, N_DIST)):
        delta = qi - kj + off
        valid = (delta >= 0) & (delta <= N_DIST)
        dist = (delta * d).astype(F32)
        bias_ref[2 * g + which, :bq, :klen] = jnp.where(valid, -(slope * LOG2E) * dist, NEG_INF)

  for g, (_, d) in enumerate(A_GROUPS):
    q_ref, k_ref, v_ref = qkv[3 * g:3 * g + 3]
    ls = seq // d
    bq = min(ATT_BQ, ls)
    nqb = ls // bq
    klen = min(bq + N_DIST, ls)

    def coords(idx, nqb=nqb, bq=bq):
      r = idx // nqb
      qb = idx % nqb
      q0 = pl.multiple_of(qb * bq, bq)
      k0 = pl.multiple_of(jnp.maximum(qb * bq - N_DIST, 0), N_DIST)
      return r, qb, q0, k0

    def put_rows(ref, idx, val, d=d, bq=bq):
      r, _, q0, _ = coords(idx)
      if d == 1:
        ref[pl.ds(q0, bq), :] = val
      else:
        ref[pl.ds(r + q0 * d, bq, stride=d), :] = val

    def scores(idx, slot, bq=bq, klen=klen, q_ref=q_ref, k_ref=k_ref):
      r, _, q0, k0 = coords(idx)
      s_buf[slot, :bq, :klen] = _dot_nt(q_ref[r, pl.ds(q0, bq), :], k_ref[r, pl.ds(k0, klen), :])

    def softmax(idx, slot, g=g, bq=bq, klen=klen):
      _, qb, _, _ = coords(idx)
      s = (s_buf[slot, :bq, :klen] * (scale * LOG2E)
           + bias_ref[2 * g + jnp.minimum(qb, 1), :bq, :klen])
      m = jnp.max(s, axis=-1, keepdims=True)
      p = jnp.exp2(s - m)
      l = jnp.sum(p, axis=-1, keepdims=True)
      p_buf[slot, :bq, :klen] = p.astype(BF16)
      l_buf[slot, :bq, :] = jnp.broadcast_to(l, (bq, 128))
      put_rows(lacc[g], idx, jnp.broadcast_to(m + jnp.log2(l), (bq, 128)))

    def values(idx, slot, g=g, bq=bq, klen=klen, v_ref=v_ref):
      r, _, _, k0 = coords(idx)
      o = _dot(p_buf[slot, :bq, :klen], v_ref[r, pl.ds(k0, klen), :]) / l_buf[slot, :bq, :]
      put_rows(oacc[g], idx, o)

    nblk = d * nqb
    ns = ATT_SET

    def run(stage, first, half):
      for t in range(ns):
        stage(first + t, half * ns + t)

    run(scores, 0, 0)
    run(softmax, 0, 0)
    run(scores, ns, 1)

    def steady(j, carry):
      i = 2 * ns * (j + 1)
      run(values, i - 2 * ns, 0)
      run(softmax, i - ns, 1)
      run(scores, i, 0)
      run(values, i - ns, 1)
      run(softmax, i, 0)
      run(scores, i + ns, 1)
      return carry

    lax.fori_loop(0, (nblk - 2 * ns) // (2 * ns), steady, 0)
    run(values, nblk - 2 * ns, 0)
    run(softmax, nblk - ns, 1)
    run(values, nblk - ns, 1)

  mrows = 512

  def merge(i, carry):
    r0 = pl.multiple_of(i * mrows, mrows)
    l0 = lacc[0][pl.ds(r0, mrows), :]
    l1 = lacc[1][pl.ds(r0, mrows), :]
    l2 = lacc[2][pl.ds(r0, mrows), :]
    mx = jnp.maximum(jnp.maximum(l0, l1), l2)
    e0 = jnp.exp2(l0 - mx)
    e1 = jnp.exp2(l1 - mx)
    e2 = jnp.exp2(l2 - mx)
    den = e0 + e1 + e2
    acc = (e0 * oacc[0][pl.ds(r0, mrows), :] + e1 * oacc[1][pl.ds(r0, mrows), :]
           + e2 * oacc[2][pl.ds(r0, mrows), :])
    o_ref[pl.ds(r0, mrows), :] = acc / den
    return carry

  lax.fori_loop(0, seq // mrows, merge, 0)


def _attn_prompt(qkvs, slopes, batch, seq):
  in_specs = [pl.BlockSpec(memory_space=pltpu.SMEM)]
  args = [slopes]
  for g, (_, d) in enumerate(A_GROUPS):
    ls = seq // d
    for part in range(3):
      in_specs.append(pl.BlockSpec(
          (None, d, None, ls, 128),
          lambda h, b, part=part: (b, 0, part * A_HEADS + h, 0, 0)))
      args.append(qkvs[g])
  scratch = ([pltpu.VMEM((seq, 128), F32) for _ in range(6)]
             + [pltpu.VMEM((6, ATT_BQ, ATT_BQ + N_DIST), F32),
                pltpu.VMEM((2 * ATT_SET, ATT_BQ, ATT_BQ + N_DIST), F32),
                pltpu.VMEM((2 * ATT_SET, ATT_BQ, ATT_BQ + N_DIST), BF16),
                pltpu.VMEM((2 * ATT_SET, ATT_BQ, 128), F32)])
  return pl.pallas_call(
      functools.partial(_attn_prompt_kernel, seq=seq),
      grid=(A_HEADS, batch),
      in_specs=in_specs,
      out_specs=pl.BlockSpec((None, seq, 128), lambda h, b: (b, 0, h)),
      out_shape=jax.ShapeDtypeStruct((batch, seq, A_OUT), F32),
      scratch_shapes=scratch,
      compiler_params=_params(("parallel", "arbitrary")),
      name="attn_prompt",
  )(*args)


SC_ROLLED = 2
SC_CHUNK_ROWS = 8
SAMPLE_NB = 1


def _sc_roll(caches, shift, after):
  batch, _, slots, lanes = caches[0].shape
  rows = [c.shape[1] * slots for c in caches]
  nchunks = [(k.shape[1] - shift) // SC_CHUNK_ROWS for k in caches]
  assert all(n * SC_CHUNK_ROWS == k.shape[1] - shift for n, k in zip(nchunks, caches))
  chunks = [SC_CHUNK_ROWS * slots] * len(caches)
  mesh = plsc.VectorSubcoreMesh(core_axis_name="c", subcore_axis_name="s")
  nsub = mesh.num_subcores
  assert mesh.num_cores * nsub == batch, "one vector subcore per batch entry"
  n = len(caches)

  def body(*refs):
    x_hbm = refs[:n]
    o_hbm = refs[n + len(after):2 * n + len(after)]
    buf = refs[-1]
    entry = lax.axis_index("c") * nsub + lax.axis_index("s")
    for g in range(n):
      base = entry * rows[g]

      chunk = chunks[g]
      stage = buf.at[pl.ds(0, chunk)]

      @pl.loop(0, nchunks[g])
      def _(i, g=g, base=base, chunk=chunk, stage=stage):
        pltpu.sync_copy(x_hbm[g].at[pl.ds(base + shift * slots + i * chunk, chunk)], stage)
        pltpu.sync_copy(stage, o_hbm[g].at[pl.ds(base + i * chunk, chunk)])

  flats = [c.reshape(batch * r, lanes) for c, r in zip(caches, rows)]
  outs = pl.kernel(body, out_type=[jax.ShapeDtypeStruct(f.shape, f.dtype) for f in flats],
                   mesh=mesh,
                   scratch_types=[pltpu.VMEM((max(chunks), lanes), caches[0].dtype)])(*flats, *after)
  return [o.reshape(c.shape) for o, c in zip(outs, caches)]


def _attn_sample_kernel(slopes_ref, q0_ref, q1_ref, q2_ref, c0_ref, c1_ref, c2_ref,
                        n0_ref, n1_ref, n2_ref,
                        r0_ref, r1_ref, r2_ref, o_ref, u0_ref, u1_ref, u2_ref, sems, bias_ref,
                        *, tq):
  b = pl.program_id(0)
  cache_refs = (c0_ref, c1_ref, c2_ref)
  new_refs = (n0_ref, n1_ref, n2_ref)
  out_refs = (u0_ref, u1_ref, u2_ref)
  del r0_ref, r1_ref, r2_ref

  q_refs = (q0_ref, q1_ref, q2_ref)
  nb = q0_ref.shape[0]
  copies = []
  for e in range(nb):
    for g, (w, _) in enumerate(A_GROUPS):
      tail = pltpu.make_async_copy(new_refs[g].at[e], out_refs[g].at[b * nb + e, pl.ds(w - tq, tq)],
                                   sems.at[e * len(A_GROUPS) + g])
      tail.start()
      copies.append(tail)

  scale = A_HEAD_DIM ** -0.5
  nreal = (N_DIST + tq) * A_HEADS
  nrow = -(-nreal // 128) * 128

  @pl.when(b == 0)
  def _():
    ri = lax.broadcasted_iota(jnp.int32, (nrow, 128), 0)
    cj = lax.broadcasted_iota(jnp.int32, (nrow, 128), 1)
    slot = ri % A_HEADS
    key = ri // A_HEADS
    tcol = cj // A_HEADS
    hcol = cj % A_HEADS
    col_ok = cj < tq * A_HEADS
    for g, (_, d) in enumerate(A_GROUPS):
      qt = tcol // d
      nidx = key - N_DIST
      in_cache = key < N_DIST
      valid_c = in_cache & (key >= qt)
      dist_c = d * (N_DIST + qt - key)
      valid_n = (~in_cache) & (nidx <= tcol) & ((tcol - nidx) % d == 0)
      dist_n = tcol - nidx
      valid = (valid_c | valid_n) & (slot == hcol) & col_ok & (ri < nreal)
      dist = jnp.where(in_cache, dist_c, dist_n).astype(F32)
      slope = jnp.zeros((nrow, 128), F32)
      for hh in range(A_HEADS):
        slope = jnp.where(hcol == hh, slopes_ref[g, hh], slope)
      bias_ref[g] = jnp.where(valid, -slope * dist, NEG_INF)

  qrow_t = lax.broadcasted_iota(jnp.int32, (128, 1), 0) // A_HEADS
  xpad = jnp.zeros((nrow - nreal, 128), F32)
  for e in range(nb):
    _attend_entry(e, q_refs, cache_refs, new_refs, bias_ref, o_ref, qrow_t, xpad, scale, tq)

  for c in copies:
    c.wait()


def _attend_entry(e, q_refs, cache_refs, new_refs, bias_ref, o_ref, qrow_t, xpad, scale, tq):
  lse_g = []
  o_g = []
  nh = A_HEADS
  for g, (w, d) in enumerate(A_GROUPS):
    qg = jnp.concatenate([q_refs[g][e], jnp.zeros((128 - tq * A_HEADS, 128), F32)], axis=0)
    nres = min(d, tq)
    xk, xv, qs = [], [], []
    for rho in range(nres):
      parts = []
      for lo in (0, nh):
        if g == SC_ROLLED:
          xc = cache_refs[g][e, :, rho, lo:lo + nh, :]
        elif d == 1:
          xc = cache_refs[g][e, :, lo:lo + nh, :]
        else:
          xc = cache_refs[g][e, pl.ds(rho, N_DIST, stride=d), lo:lo + nh, :]
        xn = new_refs[g][e, :, lo:lo + nh, :]
        parts.append(jnp.concatenate(
            [xc.reshape(N_DIST * nh, 128), xn.reshape(tq * nh, 128), xpad], axis=0).astype(BF16))
      xk.append(parts[0])
      xv.append(parts[1])
      qs.append(qg if d == 1 else jnp.where(qrow_t % d == rho, qg, 0.0))
    cat = lambda xs: xs[0] if nres == 1 else jnp.concatenate(xs, axis=1)
    s = _dot_nt(cat(xk), cat(qs).astype(BF16)) * scale + bias_ref[g]
    m = jnp.max(s, axis=0, keepdims=True)
    p = jnp.exp(s - m)
    l = jnp.sum(p, axis=0, keepdims=True)
    pvt = (p / l).T.astype(BF16)
    full = _dot(pvt, cat(xv))
    o = full[:, :128]
    for rho in range(1, nres):
      o = jnp.where(qrow_t % d == rho, full[:, rho * 128:(rho + 1) * 128], o)
    o_g.append(o)
    lse_g.append(jnp.broadcast_to(m + jnp.log(l), (128, 128)).T)
  mx = jnp.maximum(jnp.maximum(lse_g[0], lse_g[1]), lse_g[2])
  es = [jnp.exp(v - mx) for v in lse_g]
  den = es[0] + es[1] + es[2]
  o_ref[e] = (es[0] * o_g[0] + es[1] * o_g[1] + es[2] * o_g[2]) / den


def _attn_sample(qs, caches, news, slopes, tq, after):
  batch = qs[0].shape[0]
  nb = SAMPLE_NB
  in_specs = [pl.BlockSpec(memory_space=pltpu.SMEM)]
  args = [slopes]
  for q in qs:
    in_specs.append(pl.BlockSpec((nb, tq * A_HEADS, 128), lambda b: (b, 0, 0)))
    args.append(q)
  for g, (w, d) in enumerate(A_GROUPS):
    if g == SC_ROLLED:
      in_specs.append(pl.BlockSpec((nb, N_DIST, min(d, tq), 8, 128), lambda b: (b, 0, 0, 0, 0)))
      args.append(caches[g].reshape(batch, N_DIST, d, 8, 128))
    else:
      in_specs.append(pl.BlockSpec((nb, w, 8, 128), lambda b: (b, 0, 0, 0)))
      args.append(caches[g])
  for g in range(3):
    in_specs.append(pl.BlockSpec((nb, tq, 8, 128), lambda b: (b, 0, 0, 0)))
    args.append(news[g])
  rolled_in = len(args)
  for r in _sc_roll(caches, tq, after):
    in_specs.append(pl.BlockSpec(memory_space=pl.ANY))
    args.append(r)
  out_shape = [jax.ShapeDtypeStruct((batch, 128, 128), F32)]
  out_specs = [pl.BlockSpec((nb, 128, 128), lambda b: (b, 0, 0))]
  for g, (w, _) in enumerate(A_GROUPS):
    out_shape.append(jax.ShapeDtypeStruct((batch, w, 8, 128), F32))
    out_specs.append(pl.BlockSpec(memory_space=pl.ANY))
  return pl.pallas_call(
      functools.partial(_attn_sample_kernel, tq=tq),
      grid=(batch // nb,),
      in_specs=in_specs,
      out_specs=out_specs,
      out_shape=out_shape,
      input_output_aliases={rolled_in + g: 1 + g for g in range(len(A_GROUPS))},
      scratch_shapes=[pltpu.SemaphoreType.DMA((nb * len(A_GROUPS),)),
                      pltpu.VMEM((len(A_GROUPS), -(-(N_DIST + tq) * A_HEADS // 128) * 128, 128),
                                 F32)],
      compiler_params=_params(("arbitrary",)),
      name="attn_sample",
  )(*args)


def _retention_kernel(x_ref, wqk_ref, wv_ref, cos_ref, sin_ref, dmat_ref, qdec_ref, kdec_ref,
                      cdec_ref, s0_ref, gn_ref, o_ref, sout_ref, state, qk_ref, v_ref, *, chunk):
  j = pl.program_id(1)
  nb, tc, _ = x_ref.shape

  @pl.when(j == 0)
  def _():
    state[...] = s0_ref[...]

  xb = x_ref[...].reshape(nb * tc, D_MODEL).astype(BF16)
  qk_ref[...] = _dot(xb, wqk_ref[...])
  v_ref[...] = _dot(xb, wv_ref[...]).astype(BF16)
  kscale = R_KEY_DIM ** -0.5
  for bi in range(nb):
    for ci in range(tc // chunk):
      rows = pl.ds(bi * tc + ci * chunk, chunk)
      seq_rows = pl.ds(ci * chunk, chunk)
      cosv = cos_ref[seq_rows, :]
      sinv = sin_ref[seq_rows, :]
      for h in range(R_HEADS):
        qh = qk_ref[rows, h * 128:(h + 1) * 128]
        kh = qk_ref[rows, R_QK + h * 128:R_QK + (h + 1) * 128]
        qrot = qh * cosv + pltpu.roll(qh, 64, 1) * sinv
        krot = (kh * cosv + pltpu.roll(kh, 64, 1) * sinv) * kscale
        qb = qrot.astype(BF16)
        vh = v_ref[rows, h * 256:(h + 1) * 256]
        att = _dot_nt(qb, krot.astype(BF16)) * dmat_ref[h]
        sh = state[bi, h]
        o = _dot(att.astype(BF16), vh) + _dot(qb, sh.astype(BF16)) * qdec_ref[h]
        state[bi, h] = cdec_ref[h] * sh + _dot_tn((krot * kdec_ref[h]).astype(BF16), vh)
        mu = jnp.mean(o, axis=-1, keepdims=True)
        var = jnp.mean(jnp.square(o - mu), axis=-1, keepdims=True)
        on = (o - mu) * lax.rsqrt(var + EPS) * gn_ref[:, h * 256:(h + 1) * 256]
        o_ref[bi, seq_rows, h * 256:(h + 1) * 256] = on

  @pl.when(j == pl.num_programs(1) - 1)
  def _():
    sout_ref[...] = state[...]


def _retention(x, w16, cos, sin, tables, s0, gn, rows_per_step, chunk, nb):
  batch, t, _ = x.shape
  dmat, qdec, kdec, cdec = tables
  tc = rows_per_step
  const3 = lambda b, j: (0, 0, 0)
  wslab = lambda off: pl.BlockSpec((D_MODEL, R_V), lambda b, j: (0, off // R_V),
                                   pipeline_mode=pl.Buffered(1))
  state_spec = pl.BlockSpec((nb, R_HEADS, R_KEY_DIM, R_VAL_DIM), lambda b, j: (b, 0, 0, 0))
  return pl.pallas_call(
      functools.partial(_retention_kernel, chunk=chunk),
      grid=(batch // nb, t // tc),
      in_specs=[pl.BlockSpec((nb, tc, D_MODEL), lambda b, j: (b, j, 0)),
                wslab(_OFF[4]), wslab(_OFF[6]),
                pl.BlockSpec((tc, 128), lambda b, j: (j, 0)),
                pl.BlockSpec((tc, 128), lambda b, j: (j, 0)),
                pl.BlockSpec(dmat.shape, const3),
                pl.BlockSpec(qdec.shape, const3),
                pl.BlockSpec(kdec.shape, const3),
                pl.BlockSpec(cdec.shape, const3),
                state_spec,
                pl.BlockSpec((1, R_V), lambda b, j: (0, 0))],
      out_specs=[pl.BlockSpec((nb, tc, R_V), lambda b, j: (b, j, 0)), state_spec],
      out_shape=[jax.ShapeDtypeStruct((batch, t, R_V), F32),
                 jax.ShapeDtypeStruct((batch, R_HEADS, R_KEY_DIM, R_VAL_DIM), F32)],
      scratch_shapes=[pltpu.VMEM((nb, R_HEADS, R_KEY_DIM, R_VAL_DIM), F32),
                      pltpu.VMEM((nb * tc, 2 * R_QK), F32), pltpu.VMEM((nb * tc, R_V), BF16)],
      compiler_params=_params(("parallel", "arbitrary")),
      name="retention",
  )(x, w16, w16, cos, sin, dmat, qdec, kdec, cdec, s0, gn)


def _retention_tables(c_true, c_pad):
  lg = jnp.log1p(-jnp.exp2(-5.0 - jnp.arange(R_HEADS, dtype=F32)))
  i = jnp.arange(c_pad, dtype=F32)
  live = i < c_true
  diff = i[:, None] - i[None, :]
  causal = (diff >= 0) & live[:, None] & live[None, :]
  dmat = jnp.where(causal[None], jnp.exp(jnp.where(causal, diff, 0.0)[None] * lg[:, None, None]), 0.0)
  qdec = jnp.where(live[None, :], jnp.exp((i[None, :] + 1.0) * lg[:, None]), 0.0)
  kdec = jnp.where(live[None, :], jnp.exp((c_true - 1.0 - i)[None, :] * lg[:, None]), 0.0)
  cdec = jnp.exp(c_true * lg)
  qdec = jnp.broadcast_to(qdec[:, :, None], (R_HEADS, c_pad, R_VAL_DIM))
  kdec = jnp.broadcast_to(kdec[:, :, None], (R_HEADS, c_pad, R_KEY_DIM))
  cdec = jnp.broadcast_to(cdec[:, None, None], (R_HEADS, R_KEY_DIM, R_VAL_DIM))
  return dmat, qdec, kdec, cdec


def _rope_tables(pos):
  half = R_KEY_DIM // 2
  inv = 1.0 / (ROPE_BASE ** jnp.linspace(0.0, 1.0, half, dtype=F32))
  ang = pos.astype(F32)[:, None] * inv[None, :]
  cos = jnp.cos(ang)
  sin = jnp.sin(ang)
  return jnp.concatenate([cos, cos], axis=-1), jnp.concatenate([-sin, sin], axis=-1)


def _output_kernel(x_ref, oa_ref, or_ref, wza_ref, wzr_ref, wga_ref, wgb_ref, wpa_ref, wpb_ref,
                   wo_ref, lng_ref, lnb_ref, y_ref):
  tm = x_ref.shape[0]
  nsub = 2 if tm % 32 == 0 else 1
  sub = tm // nsub
  for si in range(nsub):
    rows = pl.ds(si * sub, sub)
    x = x_ref[rows, :]
    xb = x.astype(BF16)
    za = _dot(xb, wza_ref[...])
    ya = _dot((jax.nn.silu(za) * oa_ref[rows, :]).astype(BF16), wpa_ref[...])
    zr = _dot(xb, wzr_ref[...])
    yb = _dot((jax.nn.silu(zr) * or_ref[rows, :]).astype(BF16), wpb_ref[...])
    ga = _dot(xb, wga_ref[...])
    gb = _dot(xb, wgb_ref[...])
    mix = jax.nn.sigmoid(ga) * ya + jax.nn.sigmoid(gb) * yb
    out = _dot(mix.astype(BF16), wo_ref[...])
    zz = DEEPNORM_ALPHA * x + out
    zm = jnp.mean(zz, axis=-1, keepdims=True)
    zv = jnp.mean(jnp.square(zz - zm), axis=-1, keepdims=True)
    y_ref[rows, :] = (zz - zm) * lax.rsqrt(zv + EPS) * lng_ref[...] + lnb_ref[...]


def _output(x, oa, orr, w16, wpa, wpb, wo, lng, lnb, tm):
  m = x.shape[0]
  row = lambda w: pl.BlockSpec((tm, w), lambda i: (i, 0))
  full = lambda a: pl.BlockSpec(a.shape, lambda i: (0, 0), pipeline_mode=pl.Buffered(1))
  wcol = lambda off, width: pl.BlockSpec((D_MODEL, width), lambda i: (0, off // width),
                                         pipeline_mode=pl.Buffered(1))
  in_specs = [row(D_MODEL), row(A_OUT), row(R_V),
              wcol(_OFF[3], A_OUT), wcol(_OFF[7], R_V), wcol(_OFF[8], D_MODEL),
              wcol(_OFF[9], D_MODEL),
              full(wpa), full(wpb), full(wo), full(lng), full(lnb)]
  args = [x, oa, orr, w16, w16, w16, w16, wpa, wpb, wo, lng, lnb]
  return pl.pallas_call(
      _output_kernel,
      grid=(m // tm,),
      in_specs=in_specs,
      out_specs=row(D_MODEL),
      out_shape=jax.ShapeDtypeStruct((m, D_MODEL), F32),
      compiler_params=_params(("parallel",)),
      name="gates_output",
  )(*args)


def _alibi_slopes():
  n = len(A_GROUPS) * A_HEADS
  return jnp.exp2(-8.0 * jnp.arange(1, n + 1, dtype=F32) / n).reshape(len(A_GROUPS), A_HEADS)


def kernel(x_prompt, x_sample, cache_kv_w128, cache_kv_w512, cache_kv_w2048, state_ret,
           w_in, w_pa, w_pb, w_o, gn_g, ln_g, ln_b):
  bp, tp, _ = x_prompt.shape
  bs, ts, _ = x_sample.shape
  slopes = _alibi_slopes()
  w16 = w_in.astype(BF16)
  wpa = w_pa.astype(BF16)
  wpb = w_pb.astype(BF16)
  wo = w_o.astype(BF16)
  gn = gn_g.reshape(1, R_V)
  lng = ln_g.reshape(1, D_MODEL)
  lnb = ln_b.reshape(1, D_MODEL)

  xp2 = x_prompt.reshape(bp * tp, D_MODEL)
  xs2 = x_sample.reshape(bs * ts, D_MODEL)
  qkvs, kv_p, tails, q_s, news = [], [], [], [], []
  for g, (w, _) in enumerate(A_GROUPS):
    qkv, tail, qs_g, ns_g = _proj_group(xp2, xs2, w16, g, bp, tp)
    qkvs.append(qkv)
    tails.append(tail)
    kv_p.append(tail.reshape(bp, min(w, tp), 2, A_HEADS, A_HEAD_DIM))
    q_s.append(qs_g.reshape(bs, ts * A_HEADS, A_HEAD_DIM))
    news.append(ns_g.reshape(bs, ts, 2 * A_HEADS, A_HEAD_DIM))
  oa_p = _attn_prompt(qkvs, slopes, bp, tp).reshape(bp * tp, A_OUT)

  cos_p, sin_p = _rope_tables(jnp.arange(tp, dtype=jnp.int32))
  or_p, s_p = _retention(
      x_prompt, w16, cos_p, sin_p, _retention_tables(R_CHUNK, R_CHUNK),
      jnp.zeros((bp, R_HEADS, R_KEY_DIM, R_VAL_DIM), F32), gn, RET_ROWS, R_CHUNK, 1)
  y_p = _output(xp2, oa_p, or_p.reshape(bp * tp, R_V), w16, wpa, wpb, wo, lng, lnb, OUT_ROWS)
  caches = (cache_kv_w128, cache_kv_w512, cache_kv_w2048)
  caches8 = [c.reshape(bs, c.shape[1], 8, 128) for c in caches]
  cpad = SAMPLE_CHUNK_PAD
  xs_pad = jnp.pad(x_sample, ((0, 0), (0, cpad - ts), (0, 0)))
  cos_s, sin_s = _rope_tables(PAST_LEN + jnp.arange(ts, dtype=jnp.int32))
  padtab = lambda a: jnp.pad(a, ((0, cpad - ts), (0, 0)))
  after = [t[0, :8] for t in tails]
  oa_s, u0, u1, u2 = _attn_sample(q_s, caches8, news, slopes, ts, after)
  oa_s = oa_s[:, :ts * A_HEADS].reshape(bs * ts, A_OUT)
  kv_s = [u.reshape(bs, u.shape[1], 2, A_HEADS, A_HEAD_DIM) for u in (u0, u1, u2)]
  or_s, s_s = _retention(
      xs_pad, w16, padtab(cos_s), padtab(sin_s),
      _retention_tables(ts, cpad), state_ret.astype(F32), gn, cpad, cpad, SAMPLE_RET_NB)
  or_s = or_s[:, :ts].reshape(bs * ts, R_V)
  y_s = _output(xs2, oa_s, or_s, w16, wpa, wpb, wo, lng, lnb, SAMPLE_OUT_ROWS)

  return (y_p.reshape(bp, tp, D_MODEL), y_s.reshape(bs, ts, D_MODEL),
          kv_p[0], kv_p[1], kv_p[2], s_p,
          kv_s[0], kv_s[1], kv_s[2], s_s)
```
